```python
import math
import jax
import jax.numpy as jnp
from jax import lax
import numpy as np

D_MODEL = 1024
BATCH = 8
SEQ = 4096
DEPTH = 2

GRID_W = 64
CTX_LEN = 256
N_EVEN = (DEPTH + 1) // 2
N_ODD = DEPTH // 2
NORM_EPS = 1e-6
CONV_W = 3
CHUNK = 64
SC_WIDTH = 1024
SSD_HEADS = 16
SSD_HEAD_DIM = 64
SSD_WIDTH = SSD_HEADS * SSD_HEAD_DIM
SSD_GROUPS = 4
SSD_STATE = 128
SSD_CONV_DIM = SSD_WIDTH + 2 * SSD_GROUPS * SSD_STATE
EVEN_IN = 3 * SC_WIDTH + SSD_WIDTH + SSD_CONV_DIM + 2 * SSD_HEADS
EVEN_MIX = SC_WIDTH + SSD_WIDTH
HG_EXPAND = 128
HG_HEADS = D_MODEL // HG_EXPAND
HG_KDIM = HG_EXPAND
HG_VDIM = D_MODEL // HG_HEADS
HG_FWIDTH = HG_HEADS * HG_KDIM
HG_IWIDTH = HG_HEADS * HG_VDIM
ODD_IN = 3 * HG_FWIDTH + 2 * HG_IWIDTH
PEER_HEADS = 8
PEER_NKEYS = 128
PEER_EXPERTS = PEER_NKEYS * PEER_NKEYS
PEER_QDIM = 256
PEER_HALF = PEER_QDIM // 2
PEER_TOPK = 16
PEER_BLOCK = 128

kernel_name = "hybrid_conv_ssd_hgrn2_peer_dit"


def rmsnorm(x, w):
    xf = x.astype(jnp.float32)
    y = xf * lax.rsqrt(jnp.mean(xf * xf, axis=-1, keepdims=True) + NORM_EPS)
    return (y * w.astype(jnp.float32)).astype(x.dtype)


def group_rmsnorm(y, w, groups):
    shp = y.shape
    yg = y.astype(jnp.float32).reshape(*shp[:-1], groups, shp[-1] // groups)
    yg = yg * lax.rsqrt(jnp.mean(yg * yg, axis=-1, keepdims=True) + NORM_EPS)
    return yg.reshape(shp) * w.astype(jnp.float32)


def modulate(h, shift, scale):
    return h * (1 + scale) + shift


def dwconv(u, w):
    L = u.shape[1]
    pad = CONV_W // 2
    up = jnp.pad(u, ((0, 0), (pad, pad), (0, 0)))
    return sum(up[:, k:k + L] * w[k] for k in range(CONV_W))


def conv_grid(u, w):
    Bsz, S, C = u.shape
    rows = S // GRID_W
    return dwconv(u.reshape(Bsz * rows, GRID_W, C), w).reshape(Bsz, S, C)


def chunk_scan(decay, contrib, s0, keep_starts):
    def step(s, inp):
        d, u = inp
        return d * s + u, (s if keep_starts else None)
    s_fin, starts = lax.scan(step, s0, (jnp.moveaxis(decay, 1, 0), jnp.moveaxis(contrib, 1, 0)))
    return (jnp.moveaxis(starts, 0, 1) if keep_starts else None), s_fin


def ssd_scan(xh, dt, a, bm, cm, s0, with_out):
    Bsz, T, H, P = xh.shape
    G, N = bm.shape[2], bm.shape[3]
    hpg = H // G
    nc = T // CHUNK
    xdt = (xh * dt[..., None]).reshape(Bsz, nc, CHUNK, G, hpg, P)
    bm = bm.reshape(Bsz, nc, CHUNK, G, N)
    la = jnp.cumsum((dt * a).reshape(Bsz, nc, CHUNK, G, hpg), axis=2)
    to_end = jnp.exp(la[:, :, -1:] - la)
    contrib = jnp.einsum('bcsgn,bcsgh,bcsghp->bcghnp', bm, to_end, xdt).reshape(Bsz, nc, H, N, P)
    chunk_decay = jnp.exp(la[:, :, -1]).reshape(Bsz, nc, H, 1, 1)
    starts, s_fin = chunk_scan(chunk_decay, contrib, s0, with_out)
    if not with_out:
        return None, s_fin
    cm = cm.reshape(Bsz, nc, CHUNK, G, N)
    mask = jnp.tril(jnp.ones((CHUNK, CHUNK), dtype=bool))[:, :, None, None]
    seg = la[:, :, :, None] - la[:, :, None]
    lmat = jnp.exp(jnp.where(mask, seg, -jnp.inf))
    cb = jnp.einsum('bctgn,bcsgn->bctsg', cm, bm)
    y = jnp.einsum('bctsg,bctsgh,bcsghp->bctghp', cb, lmat, xdt)
    y = y + jnp.einsum('bctgn,bcghnp,bctgh->bctghp', cm, starts.reshape(Bsz, nc, G, hpg, N, P), jnp.exp(la))
    return y.reshape(Bsz, T, H, P), s_fin


def gla_scan(q, k, v, g, s0, with_out):
    Bsz, T, H, K = q.shape
    V = v.shape[-1]
    nc = T // CHUNK
    q = q.reshape(Bsz, nc, CHUNK, H, K)
    k = k.reshape(Bsz, nc, CHUNK, H, K)
    v = v.reshape(Bsz, nc, CHUNK, H, V)
    gc = jnp.cumsum(g.reshape(Bsz, nc, CHUNK, H, K), axis=2)
    g_end = gc[:, :, -1]
    contrib = jnp.einsum('bcshk,bcshv->bchkv', k * jnp.exp(g_end[:, :, None] - gc), v)
    starts, s_fin = chunk_scan(jnp.exp(g_end)[..., None], contrib, s0, with_out)
    if not with_out:
        return None, s_fin
    qg = q * jnp.exp(gc)
    kg = k * jnp.exp(-gc)
    mask = jnp.tril(jnp.ones((CHUNK, CHUNK), dtype=bool))
    att = jnp.where(mask, jnp.einsum('bcthk,bcshk->bchts', qg, kg), 0.0)
    o = jnp.einsum('bchts,bcshv->bcthv', att, v) + jnp.einsum('bcthk,bchkv->bcthv', qg, starts)
    return o.reshape(Bsz, T, H, V), s_fin


def _flipper(d):
    return (lambda t: jnp.flip(t, axis=1)) if d == 1 else (lambda t: t)


def even_mixer(hx, hc, w_in, w_out, sc_conv_w, ssd_conv_w, ssd_conv_b, dt_bias, a_log, d_skip,
               ssd_norm_w, need_ctx):
    a = -jnp.exp(a_log.astype(jnp.float32))
    dt_bias = dt_bias.astype(jnp.float32)

    def project(h, conv):
        Bsz, T, _ = h.shape
        sc_x, sc_b, sc_c, z, xbc, dt_raw = jnp.split(
            h @ w_in,
            [SC_WIDTH, 2 * SC_WIDTH, 3 * SC_WIDTH, 3 * SC_WIDTH + SSD_WIDTH,
             3 * SC_WIDTH + SSD_WIDTH + SSD_CONV_DIM], axis=-1)
        xbc = jax.nn.silu(conv(xbc, ssd_conv_w) + ssd_conv_b).astype(jnp.float32)
        xs, bm, cm = jnp.split(xbc, [SSD_WIDTH, SSD_WIDTH + SSD_GROUPS * SSD_STATE], axis=-1)
        ssd_in = (xs.reshape(Bsz, T, SSD_HEADS, SSD_HEAD_DIM),
                  bm.reshape(Bsz, T, SSD_GROUPS, SSD_STATE),
                  cm.reshape(Bsz, T, SSD_GROUPS, SSD_STATE),
                  dt_raw.astype(jnp.float32).reshape(Bsz, T, 2, SSD_HEADS))
        return (sc_x, sc_b, sc_c, z), ssd_in

    def run(ssd_in, d, s0, with_out):
        xs, bm, cm, dt_raw = ssd_in
        dt = jax.nn.softplus(dt_raw[:, :, d] + dt_bias[d])
        fl = _flipper(d)
        y, s = ssd_scan(fl(xs), fl(dt), a[d], fl(bm), fl(cm), s0, with_out)
        return (fl(y) if with_out else None), s

    def finish(gates, ssd_in, y_ssd, conv):
        sc_x, sc_b, sc_c, z = gates
        Bsz, T = z.shape[:2]
        y_a = sc_b * conv(sc_c * sc_x, sc_conv_w)
        y_b = (y_ssd + d_skip.astype(jnp.float32)[:, None] * ssd_in[0]).reshape(Bsz, T, SSD_WIDTH)
        y_b = group_rmsnorm(y_b * jax.nn.silu(z.astype(jnp.float32)), ssd_norm_w, SSD_GROUPS)
        return jnp.concatenate([y_a, y_b.astype(y_a.dtype)], axis=-1) @ w_out

    gates_x, ssd_x = project(hx, conv_grid)
    gates_c, ssd_c = project(hc, dwconv)
    Bsz = hx.shape[0]
    ys_x, ys_c = [], []
    for d in range(2):
        s0 = jnp.zeros((Bsz, SSD_HEADS, SSD_STATE, SSD_HEAD_DIM), jnp.float32)
        yc_d, s_ctx = run(ssd_c, d, s0, need_ctx)
        yx_d, _ = run(ssd_x, d, s_ctx, True)
        ys_x.append(yx_d)
        ys_c.append(yc_d)
    out_x = finish(gates_x, ssd_x, ys_x[0] + ys_x[1], conv_grid)
    out_c = finish(gates_c, ssd_c, ys_c[0] + ys_c[1], dwconv) if need_ctx else None
    return out_x, out_c


def odd_mixer(hx, hc, w_in, w_out, lower_bound, norm_w, need_ctx):
    lb = lower_bound.reshape(2, 1, 1, HG_HEADS, HG_KDIM)

    def project(h):
        Bsz, T, _ = h.shape
        q, f_fwd, f_bwd, i, g = jnp.split(
            h @ w_in, [HG_FWIDTH, 2 * HG_FWIDTH, 3 * HG_FWIDTH, 3 * HG_FWIDTH + HG_IWIDTH], axis=-1)
        heads = lambda t, dim: t.astype(jnp.float32).reshape(Bsz, T, HG_HEADS, dim)
        q = heads(jax.nn.silu(q), HG_KDIM) * (HG_KDIM ** -0.5)
        return q, (heads(f_fwd, HG_KDIM), heads(f_bwd, HG_KDIM)), heads(i, HG_VDIM), g

    def run(p, d, s0, with_out):
        q, fs, i, _ = p
        f = lb[d] + (1.0 - lb[d]) * jax.nn.sigmoid(fs[d])
        fl = _flipper(d)
        o, s = gla_scan(fl(q), fl(1.0 - f), fl(i), fl(jnp.log(f)), s0, with_out)
        return (fl(o) if with_out else None), s

    def finish(o, g):
        Bsz, T = g.shape[:2]
        o = group_rmsnorm(o.reshape(Bsz, T, HG_IWIDTH), norm_w, HG_HEADS) * jax.nn.silu(g.astype(jnp.float32))
        return o.astype(g.dtype) @ w_out

    lat, cp = project(hx), project(hc)
    Bsz = hx.shape[0]
    os_x, os_c = [], []
    for d in range(2):
        s0 = jnp.zeros((Bsz, HG_HEADS, HG_KDIM, HG_VDIM), jnp.float32)
        oc_d, s_ctx = run(cp, d, s0, need_ctx)
        ox_d, _ = run(lat, d, s_ctx, True)
        os_x.append(ox_d)
        os_c.append(oc_d)
    out_x = finish(os_x[0] + os_x[1], lat[3])
    out_c = finish(os_c[0] + os_c[1], cp[3]) if need_ctx else None
    return out_x, out_c


def peer_ffn(h, wq, keys, u_tab, v_tab):
    Bsz, T, D = h.shape
    n = Bsz * T
    hf = h.reshape(n, D)
    q = (hf @ wq).reshape(n, PEER_HEADS, 2, PEER_HALF)
    sub = jnp.einsum('nhad,hakd->nhak', q, keys)
    s_top, i_top = lax.top_k(sub, PEER_TOPK)
    cand_s = (s_top[:, :, 0, :, None] + s_top[:, :, 1, None, :]).reshape(n, PEER_HEADS, PEER_TOPK * PEER_TOPK)
    cand_i = (i_top[:, :, 0, :, None] * PEER_NKEYS + i_top[:, :, 1, None, :]).reshape(n, PEER_HEADS, PEER_TOPK * PEER_TOPK)
    best_s, pos = lax.top_k(cand_s, PEER_TOPK)
    idx = jnp.take_along_axis(cand_i, pos, axis=-1)
    gate = jax.nn.softmax(best_s.astype(jnp.float32), axis=-1).astype(h.dtype)
    nb = n // PEER_BLOCK
    idx = idx.reshape(nb, PEER_BLOCK, PEER_HEADS * PEER_TOPK)
    gate = gate.reshape(nb, PEER_BLOCK, PEER_HEADS * PEER_TOPK)

    def block(args):
        hb, ib, gb = args
        act = jax.nn.gelu(jnp.einsum('td,tkd->tk', hb, u_tab[ib]), approximate=False) * gb
        return jnp.einsum('tk,tkd->td', act, v_tab[ib])

    out = lax.map(block, (hf.reshape(nb, PEER_BLOCK, D), idx, gate))
    return out.reshape(Bsz, T, D)


def setup_inputs(seed: int = 0) -> dict:
    key = jax.random.key(seed)
    ks = iter(jax.random.split(key, 32))
    nrm = lambda shape, scale: jax.random.normal(next(ks), shape, jnp.float32) * scale
    gain = lambda shape: 1.0 + 0.02 * jax.random.normal(next(ks), shape, jnp.float32)
    D = D_MODEL
    x = nrm((BATCH, SEQ, D), 1.0)
    c = nrm((BATCH, D), 1.0)
    ctx = nrm((BATCH, CTX_LEN, D), 1.0)
    c_ctx = nrm((D,), 1.0)
    ada_w = nrm((DEPTH, D, 6 * D), 0.5 * D ** -0.5)
    ada_b = nrm((DEPTH, 6 * D), 0.02)
    norm_mix_w = gain((DEPTH, D))
    norm_ffn_w = gain((DEPTH, D))
    norm_f_w = gain((D,))
    ev_w_in = nrm((N_EVEN, D, EVEN_IN), D ** -0.5)
    ev_w_out = nrm((N_EVEN, EVEN_MIX, D), EVEN_MIX ** -0.5)
    sc_conv_w = nrm((N_EVEN, CONV_W, SC_WIDTH), CONV_W ** -0.5)
    ssd_conv_w = nrm((N_EVEN, CONV_W, SSD_CONV_DIM), CONV_W ** -0.5)
    ssd_conv_b = nrm((N_EVEN, SSD_CONV_DIM), 0.02)
    log_dt = jax.random.uniform(next(ks), (N_EVEN, 2, SSD_HEADS), jnp.float32, math.log(1e-3), math.log(1e-1))
    dt0 = jnp.exp(log_dt)
    ssd_dt_bias = dt0 + jnp.log(-jnp.expm1(-dt0))
    ssd_a_log = jnp.log(jax.random.uniform(next(ks), (N_EVEN, 2, SSD_HEADS), jnp.float32, 1.0, 16.0))
    ssd_d = gain((N_EVEN, SSD_HEADS))
    ssd_norm_w = gain((N_EVEN, SSD_WIDTH))
    od_w_in = nrm((N_ODD, D, ODD_IN), D ** -0.5)
    od_w_out = nrm((N_ODD, HG_IWIDTH, D), HG_IWIDTH ** -0.5)
    hg_lb_logits = 1.0 + nrm((DEPTH, 2, HG_FWIDTH), 0.1)
    hg_norm_w = gain((N_ODD, HG_IWIDTH))
    peer_wq = nrm((DEPTH, D, PEER_HEADS * PEER_QDIM), D ** -0.5)
    peer_keys = nrm((DEPTH, PEER_HEADS, 2, PEER_NKEYS, PEER_HALF), PEER_HALF ** -0.5)
    peer_u = nrm((DEPTH, PEER_EXPERTS, D), D ** -0.5)
    peer_v = nrm((DEPTH, PEER_EXPERTS, D), 0.5)
    return {"x": x, "c": c, "ctx": ctx, "c_ctx": c_ctx, "ada_w": ada_w, "ada_b": ada_b,
            "norm_mix_w": norm_mix_w, "norm_ffn_w": norm_ffn_w, "norm_f_w": norm_f_w,
            "ev_w_in": ev_w_in, "ev_w_out": ev_w_out, "sc_conv_w": sc_conv_w,
            "ssd_conv_w": ssd_conv_w, "ssd_conv_b": ssd_conv_b, "ssd_dt_bias": ssd_dt_bias,
            "ssd_a_log": ssd_a_log, "ssd_d": ssd_d, "ssd_norm_w": ssd_norm_w,
            "od_w_in": od_w_in, "od_w_out": od_w_out, "hg_lb_logits": hg_lb_logits,
            "hg_norm_w": hg_norm_w, "peer_wq": peer_wq, "peer_keys": peer_keys,
            "peer_u": peer_u, "peer_v": peer_v}


def reference(x, c, ctx, c_ctx, ada_w, ada_b, norm_mix_w, norm_ffn_w, norm_f_w, ev_w_in, ev_w_out,
              sc_conv_w, ssd_conv_w, ssd_conv_b, ssd_dt_bias, ssd_a_log, ssd_d, ssd_norm_w,
              od_w_in, od_w_out, hg_lb_logits, hg_norm_w, peer_wq, peer_keys, peer_u, peer_v):
    lb_sm = jax.nn.softmax(hg_lb_logits.astype(jnp.float32), axis=0)
    lower_bounds = jnp.cumsum(lb_sm, axis=0) - lb_sm[0]
    cond_x = jax.nn.silu(c)[:, None, :]
    cond_c = jax.nn.silu(c_ctx)
    for layer in range(DEPTH):
        last = layer == DEPTH - 1
        j = layer // 2
        mx = jnp.split(cond_x @ ada_w[layer] + ada_b[layer], 6, axis=-1)
        mc = jnp.split(cond_c @ ada_w[layer] + ada_b[layer], 6, axis=-1)
        hx = modulate(rmsnorm(x, norm_mix_w[layer]), mx[0], mx[1])
        hc = modulate(rmsnorm(ctx, norm_mix_w[layer]), mc[0], mc[1])
        if layer % 2 == 0:
            ox, oc = even_mixer(hx, hc, ev_w_in[j], ev_w_out[j], sc_conv_w[j], ssd_conv_w[j],
                                ssd_conv_b[j], ssd_dt_bias[j], ssd_a_log[j], ssd_d[j], ssd_norm_w[j],
                                not last)
        else:
            ox, oc = odd_mixer(hx, hc, od_w_in[j], od_w_out[j], lower_bounds[layer], hg_norm_w[j],
                               not last)
        x = x + mx[2] * ox
        x = x + mx[5] * peer_ffn(modulate(rmsnorm(x, norm_ffn_w[layer]), mx[3], mx[4]),
                                 peer_wq[layer], peer_keys[layer], peer_u[layer], peer_v[layer])
        if not last:
            ctx = ctx + mc[2] * oc
            ctx = ctx + mc[5] * peer_ffn(modulate(rmsnorm(ctx, norm_ffn_w[layer]), mc[3], mc[4]),
                                         peer_wq[layer], peer_keys[layer], peer_u[layer], peer_v[layer])
    return rmsnorm(x, norm_f_w)
```

```python
import math
import functools

import jax
import jax.numpy as jnp
from jax import lax
import numpy as np
from jax.experimental import pallas as pl
from jax.experimental.pallas import tpu as pltpu

D_MODEL = 1024
BATCH = 8
SEQ = 4096
DEPTH = 2

GRID_W = 64
CTX_LEN = 256
N_EVEN = (DEPTH + 1) // 2
N_ODD = DEPTH // 2
NORM_EPS = 1e-6
CONV_W = 3
CHUNK = 64
SC_WIDTH = 1024
SSD_HEADS = 16
SSD_HEAD_DIM = 64
SSD_WIDTH = SSD_HEADS * SSD_HEAD_DIM
SSD_GROUPS = 4
SSD_STATE = 128
SSD_CONV_DIM = SSD_WIDTH + 2 * SSD_GROUPS * SSD_STATE
EVEN_IN = 3 * SC_WIDTH + SSD_WIDTH + SSD_CONV_DIM + 2 * SSD_HEADS
EVEN_MIX = SC_WIDTH + SSD_WIDTH
HG_EXPAND = 128
HG_HEADS = D_MODEL // HG_EXPAND
HG_KDIM = HG_EXPAND
HG_VDIM = D_MODEL // HG_HEADS
HG_FWIDTH = HG_HEADS * HG_KDIM
HG_IWIDTH = HG_HEADS * HG_VDIM
ODD_IN = 3 * HG_FWIDTH + 2 * HG_IWIDTH
PEER_HEADS = 8
PEER_NKEYS = 128
PEER_EXPERTS = PEER_NKEYS * PEER_NKEYS
PEER_QDIM = 256
PEER_HALF = PEER_QDIM // 2
PEER_TOPK = 16
PEER_BLOCK = 128


def rmsnorm(x, w):
    xf = x.astype(jnp.float32)
    y = xf * lax.rsqrt(jnp.mean(xf * xf, axis=-1, keepdims=True) + NORM_EPS)
    return (y * w.astype(jnp.float32)).astype(x.dtype)


def group_rmsnorm(y, w, groups):
    shp = y.shape
    yg = y.astype(jnp.float32).reshape(*shp[:-1], groups, shp[-1] // groups)
    yg = yg * lax.rsqrt(jnp.mean(yg * yg, axis=-1, keepdims=True) + NORM_EPS)
    return yg.reshape(shp) * w.astype(jnp.float32)


def modulate(h, shift, scale):
    return h * (1 + scale) + shift


def dwconv(u, w):
    L = u.shape[1]
    pad = CONV_W // 2
    up = jnp.pad(u, ((0, 0), (pad, pad), (0, 0)))
    return sum(up[:, k:k + L] * w[k] for k in range(CONV_W))


def conv_grid(u, w):
    Bsz, S, C = u.shape
    rows = S // GRID_W
    return dwconv(u.reshape(Bsz * rows, GRID_W, C), w).reshape(Bsz, S, C)


def chunk_scan(decay, contrib, s0, keep_starts):
    def step(s, inp):
        d, u = inp
        return d * s + u, (s if keep_starts else None)
    s_fin, starts = lax.scan(step, s0, (jnp.moveaxis(decay, 1, 0), jnp.moveaxis(contrib, 1, 0)))
    return (jnp.moveaxis(starts, 0, 1) if keep_starts else None), s_fin


def ssd_scan(xh, dt, a, bm, cm, s0, with_out):
    Bsz, T, H, P = xh.shape
    G, N = bm.shape[2], bm.shape[3]
    hpg = H // G
    nc = T // CHUNK
    xdt = (xh * dt[..., None]).reshape(Bsz, nc, CHUNK, G, hpg, P)
    bm = bm.reshape(Bsz, nc, CHUNK, G, N)
    la = jnp.cumsum((dt * a).reshape(Bsz, nc, CHUNK, G, hpg), axis=2)
    to_end = jnp.exp(la[:, :, -1:] - la)
    contrib = jnp.einsum('bcsgn,bcsgh,bcsghp->bcghnp', bm, to_end, xdt).reshape(Bsz, nc, H, N, P)
    chunk_decay = jnp.exp(la[:, :, -1]).reshape(Bsz, nc, H, 1, 1)
    starts, s_fin = chunk_scan(chunk_decay, contrib, s0, with_out)
    if not with_out:
        return None, s_fin
    cm = cm.reshape(Bsz, nc, CHUNK, G, N)
    mask = jnp.tril(jnp.ones((CHUNK, CHUNK), dtype=bool))[:, :, None, None]
    seg = la[:, :, :, None] - la[:, :, None]
    lmat = jnp.exp(jnp.where(mask, seg, -jnp.inf))
    cb = jnp.einsum('bctgn,bcsgn->bctsg', cm, bm)
    y = jnp.einsum('bctsg,bctsgh,bcsghp->bctghp', cb, lmat, xdt)
    y = y + jnp.einsum('bctgn,bcghnp,bctgh->bctghp', cm, starts.reshape(Bsz, nc, G, hpg, N, P), jnp.exp(la))
    return y.reshape(Bsz, T, H, P), s_fin


def gla_scan(q, k, v, g, s0, with_out):
    Bsz, T, H, K = q.shape
    V = v.shape[-1]
    nc = T // CHUNK
    q = q.reshape(Bsz, nc, CHUNK, H, K)
    k = k.reshape(Bsz, nc, CHUNK, H, K)
    v = v.reshape(Bsz, nc, CHUNK, H, V)
    gc = jnp.cumsum(g.reshape(Bsz, nc, CHUNK, H, K), axis=2)
    g_end = gc[:, :, -1]
    contrib = jnp.einsum('bcshk,bcshv->bchkv', k * jnp.exp(g_end[:, :, None] - gc), v)
    starts, s_fin = chunk_scan(jnp.exp(g_end)[..., None], contrib, s0, with_out)
    if not with_out:
        return None, s_fin
    qg = q * jnp.exp(gc)
    kg = k * jnp.exp(-gc)
    mask = jnp.tril(jnp.ones((CHUNK, CHUNK), dtype=bool))
    att = jnp.where(mask, jnp.einsum('bcthk,bcshk->bchts', qg, kg), 0.0)
    o = jnp.einsum('bchts,bcshv->bcthv', att, v) + jnp.einsum('bcthk,bchkv->bcthv', qg, starts)
    return o.reshape(Bsz, T, H, V), s_fin


def _flipper(d):
    return (lambda t: jnp.flip(t, axis=1)) if d == 1 else (lambda t: t)


def even_mixer(hx, hc, w_in, w_out, sc_conv_w, ssd_conv_w, ssd_conv_b, dt_bias, a_log, d_skip,
               ssd_norm_w, need_ctx):
    a = -jnp.exp(a_log.astype(jnp.float32))
    dt_bias = dt_bias.astype(jnp.float32)

    def project(h, conv):
        Bsz, T, _ = h.shape
        sc_x, sc_b, sc_c, z, xbc, dt_raw = jnp.split(
            h @ w_in,
            [SC_WIDTH, 2 * SC_WIDTH, 3 * SC_WIDTH, 3 * SC_WIDTH + SSD_WIDTH,
             3 * SC_WIDTH + SSD_WIDTH + SSD_CONV_DIM], axis=-1)
        xbc = jax.nn.silu(conv(xbc, ssd_conv_w) + ssd_conv_b).astype(jnp.float32)
        xs, bm, cm = jnp.split(xbc, [SSD_WIDTH, SSD_WIDTH + SSD_GROUPS * SSD_STATE], axis=-1)
        ssd_in = (xs.reshape(Bsz, T, SSD_HEADS, SSD_HEAD_DIM),
                  bm.reshape(Bsz, T, SSD_GROUPS, SSD_STATE),
                  cm.reshape(Bsz, T, SSD_GROUPS, SSD_STATE),
                  dt_raw.astype(jnp.float32).reshape(Bsz, T, 2, SSD_HEADS))
        return (sc_x, sc_b, sc_c, z), ssd_in

    def run(ssd_in, d, s0, with_out):
        xs, bm, cm, dt_raw = ssd_in
        dt = jax.nn.softplus(dt_raw[:, :, d] + dt_bias[d])
        fl = _flipper(d)
        y, s = ssd_scan(fl(xs), fl(dt), a[d], fl(bm), fl(cm), s0, with_out)
        return (fl(y) if with_out else None), s

    def finish(gates, ssd_in, y_ssd, conv):
        sc_x, sc_b, sc_c, z = gates
        Bsz, T = z.shape[:2]
        y_a = sc_b * conv(sc_c * sc_x, sc_conv_w)
        y_b = (y_ssd + d_skip.astype(jnp.float32)[:, None] * ssd_in[0]).reshape(Bsz, T, SSD_WIDTH)
        y_b = group_rmsnorm(y_b * jax.nn.silu(z.astype(jnp.float32)), ssd_norm_w, SSD_GROUPS)
        return jnp.concatenate([y_a, y_b.astype(y_a.dtype)], axis=-1) @ w_out

    gates_x, ssd_x = project(hx, conv_grid)
    gates_c, ssd_c = project(hc, dwconv)
    Bsz = hx.shape[0]
    ys_x, ys_c = [], []
    for d in range(2):
        s0 = jnp.zeros((Bsz, SSD_HEADS, SSD_STATE, SSD_HEAD_DIM), jnp.float32)
        yc_d, s_ctx = run(ssd_c, d, s0, need_ctx)
        yx_d, _ = run(ssd_x, d, s_ctx, True)
        ys_x.append(yx_d)
        ys_c.append(yc_d)
    out_x = finish(gates_x, ssd_x, ys_x[0] + ys_x[1], conv_grid)
    out_c = finish(gates_c, ssd_c, ys_c[0] + ys_c[1], dwconv) if need_ctx else None
    return out_x, out_c


def odd_mixer(hx, hc, w_in, w_out, lower_bound, norm_w, need_ctx):
    lb = lower_bound.reshape(2, 1, 1, HG_HEADS, HG_KDIM)

    def project(h):
        Bsz, T, _ = h.shape
        q, f_fwd, f_bwd, i, g = jnp.split(
            h @ w_in, [HG_FWIDTH, 2 * HG_FWIDTH, 3 * HG_FWIDTH, 3 * HG_FWIDTH + HG_IWIDTH], axis=-1)
        heads = lambda t, dim: t.astype(jnp.float32).reshape(Bsz, T, HG_HEADS, dim)
        q = heads(jax.nn.silu(q), HG_KDIM) * (HG_KDIM ** -0.5)
        return q, (heads(f_fwd, HG_KDIM), heads(f_bwd, HG_KDIM)), heads(i, HG_VDIM), g

    def run(p, d, s0, with_out):
        q, fs, i, _ = p
        f = lb[d] + (1.0 - lb[d]) * jax.nn.sigmoid(fs[d])
        fl = _flipper(d)
        o, s = gla_scan(fl(q), fl(1.0 - f), fl(i), fl(jnp.log(f)), s0, with_out)
        return (fl(o) if with_out else None), s

    def finish(o, g):
        Bsz, T = g.shape[:2]
        o = group_rmsnorm(o.reshape(Bsz, T, HG_IWIDTH), norm_w, HG_HEADS) * jax.nn.silu(g.astype(jnp.float32))
        return o.astype(g.dtype) @ w_out

    lat, cp = project(hx), project(hc)
    Bsz = hx.shape[0]
    os_x, os_c = [], []
    for d in range(2):
        s0 = jnp.zeros((Bsz, HG_HEADS, HG_KDIM, HG_VDIM), jnp.float32)
        oc_d, s_ctx = run(cp, d, s0, need_ctx)
        ox_d, _ = run(lat, d, s_ctx, True)
        os_x.append(ox_d)
        os_c.append(oc_d)
    out_x = finish(os_x[0] + os_x[1], lat[3])
    out_c = finish(os_c[0] + os_c[1], cp[3]) if need_ctx else None
    return out_x, out_c


def peer_ffn(h, wq, keys, u_tab, v_tab):
    Bsz, T, D = h.shape
    n = Bsz * T
    hf = h.reshape(n, D)
    q = (hf @ wq).reshape(n, PEER_HEADS, 2, PEER_HALF)
    sub = jnp.einsum('nhad,hakd->nhak', q, keys)
    s_top, i_top = lax.top_k(sub, PEER_TOPK)
    cand_s = (s_top[:, :, 0, :, None] + s_top[:, :, 1, None, :]).reshape(n, PEER_HEADS, PEER_TOPK * PEER_TOPK)
    cand_i = (i_top[:, :, 0, :, None] * PEER_NKEYS + i_top[:, :, 1, None, :]).reshape(n, PEER_HEADS, PEER_TOPK * PEER_TOPK)
    best_s, pos = lax.top_k(cand_s, PEER_TOPK)
    idx = jnp.take_along_axis(cand_i, pos, axis=-1)
    gate = jax.nn.softmax(best_s.astype(jnp.float32), axis=-1).astype(h.dtype)
    nb = n // PEER_BLOCK
    idx = idx.reshape(nb, PEER_BLOCK, PEER_HEADS * PEER_TOPK)
    gate = gate.reshape(nb, PEER_BLOCK, PEER_HEADS * PEER_TOPK)

    def block(args):
        hb, ib, gb = args
        act = jax.nn.gelu(jnp.einsum('td,tkd->tk', hb, u_tab[ib]), approximate=False) * gb
        return jnp.einsum('tk,tkd->td', act, v_tab[ib])

    out = lax.map(block, (hf.reshape(nb, PEER_BLOCK, D), idx, gate))
    return out.reshape(Bsz, T, D)


def _final_norm_body(x_ref, w_ref, o_ref):
    xf = x_ref[...]
    y = xf * lax.rsqrt(jnp.mean(xf * xf, axis=-1, keepdims=True) + NORM_EPS)
    o_ref[...] = y * w_ref[...]


def final_rmsnorm(x, w):
    Bsz, S, D = x.shape
    n = Bsz * S
    tm = 1024
    out = pl.pallas_call(
        _final_norm_body,
        grid=(n // tm,),
        in_specs=[pl.BlockSpec((tm, D), lambda i: (i, 0)), pl.BlockSpec((1, D), lambda i: (0, 0))],
        out_specs=pl.BlockSpec((tm, D), lambda i: (i, 0)),
        out_shape=jax.ShapeDtypeStruct((n, D), x.dtype),
        name="final_rmsnorm",
    )(x.reshape(n, D), w.reshape(1, D))
    return out.reshape(Bsz, S, D)


def kernel(x, c, ctx, c_ctx, ada_w, ada_b, norm_mix_w, norm_ffn_w, norm_f_w, ev_w_in, ev_w_out,
           sc_conv_w, ssd_conv_w, ssd_conv_b, ssd_dt_bias, ssd_a_log, ssd_d, ssd_norm_w,
           od_w_in, od_w_out, hg_lb_logits, hg_norm_w, peer_wq, peer_keys, peer_u, peer_v):
    lb_sm = jax.nn.softmax(hg_lb_logits.astype(jnp.float32), axis=0)
    lower_bounds = jnp.cumsum(lb_sm, axis=0) - lb_sm[0]
    cond_x = jax.nn.silu(c)[:, None, :]
    cond_c = jax.nn.silu(c_ctx)
    for layer in range(DEPTH):
        last = layer == DEPTH - 1
        j = layer // 2
        mx = jnp.split(cond_x @ ada_w[layer] + ada_b[layer], 6, axis=-1)
        mc = jnp.split(cond_c @ ada_w[layer] + ada_b[layer], 6, axis=-1)
        hx = modulate(rmsnorm(x, norm_mix_w[layer]), mx[0], mx[1])
        hc = modulate(rmsnorm(ctx, norm_mix_w[layer]), mc[0], mc[1])
        if layer % 2 == 0:
            ox, oc = even_mixer(hx, hc, ev_w_in[j], ev_w_out[j], sc_conv_w[j], ssd_conv_w[j],
                                ssd_conv_b[j], ssd_dt_bias[j], ssd_a_log[j], ssd_d[j], ssd_norm_w[j],
                                not last)
        else:
            ox, oc = odd_mixer(hx, hc, od_w_in[j], od_w_out[j], lower_bounds[layer], hg_norm_w[j],
                               not last)
        x = x + mx[2] * ox
        x = x + mx[5] * peer_ffn(modulate(rmsnorm(x, norm_ffn_w[layer]), mx[3], mx[4]),
                                 peer_wq[layer], peer_keys[layer], peer_u[layer], peer_v[layer])
        if not last:
            ctx = ctx + mc[2] * oc
            ctx = ctx + mc[5] * peer_ffn(modulate(rmsnorm(ctx, norm_ffn_w[layer]), mc[3], mc[4]),
                                         peer_wq[layer], peer_keys[layer], peer_u[layer], peer_v[layer])
    return final_rmsnorm(x, norm_f_w)
```

```python
import math
import functools

import jax
import jax.numpy as jnp
from jax import lax
import numpy as np
from jax.experimental import pallas as pl
from jax.experimental.pallas import tpu as pltpu

D_MODEL = 1024
BATCH = 8
SEQ = 4096
DEPTH = 2

GRID_W = 64
CTX_LEN = 256
N_EVEN = (DEPTH + 1) // 2
N_ODD = DEPTH // 2
NORM_EPS = 1e-6
CONV_W = 3
CHUNK = 64
SC_WIDTH = 1024
SSD_HEADS = 16
SSD_HEAD_DIM = 64
SSD_WIDTH = SSD_HEADS * SSD_HEAD_DIM
SSD_GROUPS = 4
SSD_STATE = 128
SSD_CONV_DIM = SSD_WIDTH + 2 * SSD_GROUPS * SSD_STATE
EVEN_IN = 3 * SC_WIDTH + SSD_WIDTH + SSD_CONV_DIM + 2 * SSD_HEADS
EVEN_MIX = SC_WIDTH + SSD_WIDTH
HG_EXPAND = 128
HG_HEADS = D_MODEL // HG_EXPAND
HG_KDIM = HG_EXPAND
HG_VDIM = D_MODEL // HG_HEADS
HG_FWIDTH = HG_HEADS * HG_KDIM
HG_IWIDTH = HG_HEADS * HG_VDIM
ODD_IN = 3 * HG_FWIDTH + 2 * HG_IWIDTH
PEER_HEADS = 8
PEER_NKEYS = 128
PEER_EXPERTS = PEER_NKEYS * PEER_NKEYS
PEER_QDIM = 256
PEER_HALF = PEER_QDIM // 2
PEER_TOPK = 16
PEER_BLOCK = 128


def rmsnorm(x, w):
    xf = x.astype(jnp.float32)
    y = xf * lax.rsqrt(jnp.mean(xf * xf, axis=-1, keepdims=True) + NORM_EPS)
    return (y * w.astype(jnp.float32)).astype(x.dtype)


def group_rmsnorm(y, w, groups):
    shp = y.shape
    yg = y.astype(jnp.float32).reshape(*shp[:-1], groups, shp[-1] // groups)
    yg = yg * lax.rsqrt(jnp.mean(yg * yg, axis=-1, keepdims=True) + NORM_EPS)
    return yg.reshape(shp) * w.astype(jnp.float32)


def modulate(h, shift, scale):
    return h * (1 + scale) + shift


def dwconv(u, w):
    L = u.shape[1]
    pad = CONV_W // 2
    up = jnp.pad(u, ((0, 0), (pad, pad), (0, 0)))
    return sum(up[:, k:k + L] * w[k] for k in range(CONV_W))


def conv_grid(u, w):
    Bsz, S, C = u.shape
    rows = S // GRID_W
    return dwconv(u.reshape(Bsz * rows, GRID_W, C), w).reshape(Bsz, S, C)


def chunk_scan(decay, contrib, s0, keep_starts):
    def step(s, inp):
        d, u = inp
        return d * s + u, (s if keep_starts else None)
    s_fin, starts = lax.scan(step, s0, (jnp.moveaxis(decay, 1, 0), jnp.moveaxis(contrib, 1, 0)))
    return (jnp.moveaxis(starts, 0, 1) if keep_starts else None), s_fin


def ssd_scan(xh, dt, a, bm, cm, s0, with_out):
    Bsz, T, H, P = xh.shape
    G, N = bm.shape[2], bm.shape[3]
    hpg = H // G
    nc = T // CHUNK
    xdt = (xh * dt[..., None]).reshape(Bsz, nc, CHUNK, G, hpg, P)
    bm = bm.reshape(Bsz, nc, CHUNK, G, N)
    la = jnp.cumsum((dt * a).reshape(Bsz, nc, CHUNK, G, hpg), axis=2)
    to_end = jnp.exp(la[:, :, -1:] - la)
    contrib = jnp.einsum('bcsgn,bcsgh,bcsghp->bcghnp', bm, to_end, xdt).reshape(Bsz, nc, H, N, P)
    chunk_decay = jnp.exp(la[:, :, -1]).reshape(Bsz, nc, H, 1, 1)
    starts, s_fin = chunk_scan(chunk_decay, contrib, s0, with_out)
    if not with_out:
        return None, s_fin
    cm = cm.reshape(Bsz, nc, CHUNK, G, N)
    mask = jnp.tril(jnp.ones((CHUNK, CHUNK), dtype=bool))[:, :, None, None]
    seg = la[:, :, :, None] - la[:, :, None]
    lmat = jnp.exp(jnp.where(mask, seg, -jnp.inf))
    cb = jnp.einsum('bctgn,bcsgn->bctsg', cm, bm)
    y = jnp.einsum('bctsg,bctsgh,bcsghp->bctghp', cb, lmat, xdt)
    y = y + jnp.einsum('bctgn,bcghnp,bctgh->bctghp', cm, starts.reshape(Bsz, nc, G, hpg, N, P), jnp.exp(la))
    return y.reshape(Bsz, T, H, P), s_fin


def gla_scan(q, k, v, g, s0, with_out):
    Bsz, T, H, K = q.shape
    V = v.shape[-1]
    nc = T // CHUNK
    q = q.reshape(Bsz, nc, CHUNK, H, K)
    k = k.reshape(Bsz, nc, CHUNK, H, K)
    v = v.reshape(Bsz, nc, CHUNK, H, V)
    gc = jnp.cumsum(g.reshape(Bsz, nc, CHUNK, H, K), axis=2)
    g_end = gc[:, :, -1]
    contrib = jnp.einsum('bcshk,bcshv->bchkv', k * jnp.exp(g_end[:, :, None] - gc), v)
    starts, s_fin = chunk_scan(jnp.exp(g_end)[..., None], contrib, s0, with_out)
    if not with_out:
        return None, s_fin
    qg = q * jnp.exp(gc)
    kg = k * jnp.exp(-gc)
    mask = jnp.tril(jnp.ones((CHUNK, CHUNK), dtype=bool))
    att = jnp.where(mask, jnp.einsum('bcthk,bcshk->bchts', qg, kg), 0.0)
    o = jnp.einsum('bchts,bcshv->bcthv', att, v) + jnp.einsum('bcthk,bchkv->bcthv', qg, starts)
    return o.reshape(Bsz, T, H, V), s_fin


def _flipper(d):
    return (lambda t: jnp.flip(t, axis=1)) if d == 1 else (lambda t: t)


def even_mixer(hx, hc, w_in, w_out, sc_conv_w, ssd_conv_w, ssd_conv_b, dt_bias, a_log, d_skip,
               ssd_norm_w, need_ctx):
    a = -jnp.exp(a_log.astype(jnp.float32))
    dt_bias = dt_bias.astype(jnp.float32)

    def project(h, conv):
        Bsz, T, _ = h.shape
        sc_x, sc_b, sc_c, z, xbc, dt_raw = jnp.split(
            h @ w_in,
            [SC_WIDTH, 2 * SC_WIDTH, 3 * SC_WIDTH, 3 * SC_WIDTH + SSD_WIDTH,
             3 * SC_WIDTH + SSD_WIDTH + SSD_CONV_DIM], axis=-1)
        xbc = jax.nn.silu(conv(xbc, ssd_conv_w) + ssd_conv_b).astype(jnp.float32)
        xs, bm, cm = jnp.split(xbc, [SSD_WIDTH, SSD_WIDTH + SSD_GROUPS * SSD_STATE], axis=-1)
        ssd_in = (xs.reshape(Bsz, T, SSD_HEADS, SSD_HEAD_DIM),
                  bm.reshape(Bsz, T, SSD_GROUPS, SSD_STATE),
                  cm.reshape(Bsz, T, SSD_GROUPS, SSD_STATE),
                  dt_raw.astype(jnp.float32).reshape(Bsz, T, 2, SSD_HEADS))
        return (sc_x, sc_b, sc_c, z), ssd_in

    def run(ssd_in, d, s0, with_out):
        xs, bm, cm, dt_raw = ssd_in
        dt = jax.nn.softplus(dt_raw[:, :, d] + dt_bias[d])
        fl = _flipper(d)
        y, s = ssd_scan(fl(xs), fl(dt), a[d], fl(bm), fl(cm), s0, with_out)
        return (fl(y) if with_out else None), s

    def finish(gates, ssd_in, y_ssd, conv):
        sc_x, sc_b, sc_c, z = gates
        Bsz, T = z.shape[:2]
        y_a = sc_b * conv(sc_c * sc_x, sc_conv_w)
        y_b = (y_ssd + d_skip.astype(jnp.float32)[:, None] * ssd_in[0]).reshape(Bsz, T, SSD_WIDTH)
        y_b = group_rmsnorm(y_b * jax.nn.silu(z.astype(jnp.float32)), ssd_norm_w, SSD_GROUPS)
        return jnp.concatenate([y_a, y_b.astype(y_a.dtype)], axis=-1) @ w_out

    gates_x, ssd_x = project(hx, conv_grid)
    gates_c, ssd_c = project(hc, dwconv)
    Bsz = hx.shape[0]
    ys_x, ys_c = [], []
    for d in range(2):
        s0 = jnp.zeros((Bsz, SSD_HEADS, SSD_STATE, SSD_HEAD_DIM), jnp.float32)
        yc_d, s_ctx = run(ssd_c, d, s0, need_ctx)
        yx_d, _ = run(ssd_x, d, s_ctx, True)
        ys_x.append(yx_d)
        ys_c.append(yc_d)
    out_x = finish(gates_x, ssd_x, ys_x[0] + ys_x[1], conv_grid)
    out_c = finish(gates_c, ssd_c, ys_c[0] + ys_c[1], dwconv) if need_ctx else None
    return out_x, out_c


def odd_mixer(hx, hc, w_in, w_out, lower_bound, norm_w, need_ctx):
    lb = lower_bound.reshape(2, 1, 1, HG_HEADS, HG_KDIM)

    def project(h):
        Bsz, T, _ = h.shape
        q, f_fwd, f_bwd, i, g = jnp.split(
            h @ w_in, [HG_FWIDTH, 2 * HG_FWIDTH, 3 * HG_FWIDTH, 3 * HG_FWIDTH + HG_IWIDTH], axis=-1)
        heads = lambda t, dim: t.astype(jnp.float32).reshape(Bsz, T, HG_HEADS, dim)
        q = heads(jax.nn.silu(q), HG_KDIM) * (HG_KDIM ** -0.5)
        return q, (heads(f_fwd, HG_KDIM), heads(f_bwd, HG_KDIM)), heads(i, HG_VDIM), g

    def run(p, d, s0, with_out):
        q, fs, i, _ = p
        f = lb[d] + (1.0 - lb[d]) * jax.nn.sigmoid(fs[d])
        fl = _flipper(d)
        o, s = gla_scan(fl(q), fl(1.0 - f), fl(i), fl(jnp.log(f)), s0, with_out)
        return (fl(o) if with_out else None), s

    def finish(o, g):
        Bsz, T = g.shape[:2]
        o = group_rmsnorm(o.reshape(Bsz, T, HG_IWIDTH), norm_w, HG_HEADS) * jax.nn.silu(g.astype(jnp.float32))
        return o.astype(g.dtype) @ w_out

    lat, cp = project(hx), project(hc)
    Bsz = hx.shape[0]
    os_x, os_c = [], []
    for d in range(2):
        s0 = jnp.zeros((Bsz, HG_HEADS, HG_KDIM, HG_VDIM), jnp.float32)
        oc_d, s_ctx = run(cp, d, s0, need_ctx)
        ox_d, _ = run(lat, d, s_ctx, True)
        os_x.append(ox_d)
        os_c.append(oc_d)
    out_x = finish(os_x[0] + os_x[1], lat[3])
    out_c = finish(os_c[0] + os_c[1], cp[3]) if need_ctx else None
    return out_x, out_c


def peer_ffn(h, wq, keys, u_tab, v_tab):
    Bsz, T, D = h.shape
    n = Bsz * T
    hf = h.reshape(n, D)
    q = (hf @ wq).reshape(n, PEER_HEADS, 2, PEER_HALF)
    sub = jnp.einsum('nhad,hakd->nhak', q, keys)
    s_top, i_top = lax.top_k(sub, PEER_TOPK)
    cand_s = (s_top[:, :, 0, :, None] + s_top[:, :, 1, None, :]).reshape(n, PEER_HEADS, PEER_TOPK * PEER_TOPK)
    cand_i = (i_top[:, :, 0, :, None] * PEER_NKEYS + i_top[:, :, 1, None, :]).reshape(n, PEER_HEADS, PEER_TOPK * PEER_TOPK)
    best_s, pos = lax.top_k(cand_s, PEER_TOPK)
    idx = jnp.take_along_axis(cand_i, pos, axis=-1)
    gate = jax.nn.softmax(best_s.astype(jnp.float32), axis=-1).astype(h.dtype)
    out = peer_experts(hf, idx.reshape(n, PEER_SEL) * ROW_WORDS, gate.reshape(n, PEER_SEL), u_tab, v_tab)
    return out.reshape(Bsz, T, D)


PEER_SEL = PEER_HEADS * PEER_TOPK
LANES = 128
ROW_WORDS = D_MODEL // (2 * LANES)
TILE_STRIDE = PEER_SEL + 8
PEER_GROUP = 4
PEER_TM = 256
PEER_VMEM_BYTES = 48 * 1024 * 1024


def pack_table(tab):
    E = tab.shape[0]
    bits = lax.bitcast_convert_type(tab.astype(jnp.bfloat16), jnp.uint16).astype(jnp.uint32)
    words = bits[:, :D_MODEL // 2] | (bits[:, D_MODEL // 2:] << 16)
    return lax.bitcast_convert_type(words, jnp.int32).reshape(E * ROW_WORDS, LANES)


def _gather_rows(idx_ref, tab_ref, tile_ref, t):
    row = idx_ref.at[t]
    for k in range(PEER_SEL):
        e = pl.multiple_of(row[k], ROW_WORDS)
        tile_ref[pl.ds(k, ROW_WORDS, stride=TILE_STRIDE), :] = tab_ref[pl.ds(e, ROW_WORDS), :]


def _unpack_rows(tile_ref):
    lo, hi = [], []
    for j in range(ROW_WORDS):
        w = tile_ref[j * TILE_STRIDE:j * TILE_STRIDE + PEER_SEL, :]
        lo.append(pltpu.bitcast(w << 16, jnp.float32))
        hi.append(pltpu.bitcast(w & jnp.int32(-65536), jnp.float32))
    return jnp.concatenate(lo + hi, axis=1).astype(jnp.bfloat16)


def _peer_u_body(idx_ref, h_ref, gate_ref, tab_ref, act_ref, *tiles):
    def group(g, carry):
        for i in range(PEER_GROUP):
            _gather_rows(idx_ref, tab_ref, tiles[i], g * PEER_GROUP + i)
        for i in range(PEER_GROUP):
            t = g * PEER_GROUP + i
            hb = jnp.broadcast_to(h_ref[pl.ds(t, 1), :], (8, D_MODEL)).astype(jnp.bfloat16)
            s = lax.dot_general(hb, _unpack_rows(tiles[i]), (((1,), (1,)), ((), ())),
                                preferred_element_type=jnp.float32)
            act_ref[pl.ds(t, 1), :] = s[0:1]
        return carry

    lax.fori_loop(0, h_ref.shape[0] // PEER_GROUP, group, 0)
    d = act_ref[...]
    act_ref[...] = 0.5 * d * (1.0 + lax.erf(d * (2.0 ** -0.5))) * gate_ref[...]


def _peer_v_body(idx_ref, act_ref, tab_ref, out_ref, *tiles):
    def group(g, carry):
        for i in range(PEER_GROUP):
            _gather_rows(idx_ref, tab_ref, tiles[i], g * PEER_GROUP + i)
        for i in range(PEER_GROUP):
            t = g * PEER_GROUP + i
            ab = jnp.broadcast_to(act_ref[pl.ds(t, 1), :], (8, PEER_SEL)).astype(jnp.bfloat16)
            o = jnp.dot(ab, _unpack_rows(tiles[i]), preferred_element_type=jnp.float32)
            out_ref[pl.ds(t, 1), :] = o[0:1]
        return carry

    lax.fori_loop(0, act_ref.shape[0] // PEER_GROUP, group, 0)


def peer_experts(h, idx_words, gate, u_tab, v_tab):
    n = h.shape[0]
    assert n % PEER_TM == 0
    grid = (n // PEER_TM,)
    tiles = [pltpu.VMEM((ROW_WORDS * TILE_STRIDE, LANES), jnp.int32)] * PEER_GROUP
    params = pltpu.CompilerParams(vmem_limit_bytes=PEER_VMEM_BYTES)
    idx_spec = pl.BlockSpec((PEER_TM, PEER_SEL), lambda i: (i, 0), memory_space=pltpu.SMEM)
    sel_spec = pl.BlockSpec((PEER_TM, PEER_SEL), lambda i: (i, 0))
    tok_spec = pl.BlockSpec((PEER_TM, D_MODEL), lambda i: (i, 0))
    table_spec = pl.BlockSpec(memory_space=pltpu.VMEM)
    act = pl.pallas_call(
        _peer_u_body, grid=grid,
        in_specs=[idx_spec, tok_spec, sel_spec, table_spec],
        out_specs=sel_spec,
        out_shape=jax.ShapeDtypeStruct((n, PEER_SEL), jnp.float32),
        scratch_shapes=tiles, compiler_params=params, name="peer_u",
    )(idx_words, h, gate, pack_table(u_tab))
    return pl.pallas_call(
        _peer_v_body, grid=grid,
        in_specs=[idx_spec, sel_spec, table_spec],
        out_specs=tok_spec,
        out_shape=jax.ShapeDtypeStruct((n, D_MODEL), jnp.float32),
        scratch_shapes=tiles, compiler_params=params, name="peer_v",
    )(idx_words, act, pack_table(v_tab))


def _final_norm_body(x_ref, w_ref, o_ref):
    xf = x_ref[...]
    y = xf * lax.rsqrt(jnp.mean(xf * xf, axis=-1, keepdims=True) + NORM_EPS)
    o_ref[...] = y * w_ref[...]


def final_rmsnorm(x, w):
    Bsz, S, D = x.shape
    n = Bsz * S
    tm = 1024
    out = pl.pallas_call(
        _final_norm_body,
        grid=(n // tm,),
        in_specs=[pl.BlockSpec((tm, D), lambda i: (i, 0)), pl.BlockSpec((1, D), lambda i: (0, 0))],
        out_specs=pl.BlockSpec((tm, D), lambda i: (i, 0)),
        out_shape=jax.ShapeDtypeStruct((n, D), x.dtype),
        name="final_rmsnorm",
    )(x.reshape(n, D), w.reshape(1, D))
    return out.reshape(Bsz, S, D)


def kernel(x, c, ctx, c_ctx, ada_w, ada_b, norm_mix_w, norm_ffn_w, norm_f_w, ev_w_in, ev_w_out,
           sc_conv_w, ssd_conv_w, ssd_conv_b, ssd_dt_bias, ssd_a_log, ssd_d, ssd_norm_w,
           od_w_in, od_w_out, hg_lb_logits, hg_norm_w, peer_wq, peer_keys, peer_u, peer_v):
    lb_sm = jax.nn.softmax(hg_lb_logits.astype(jnp.float32), axis=0)
    lower_bounds = jnp.cumsum(lb_sm, axis=0) - lb_sm[0]
    cond_x = jax.nn.silu(c)[:, None, :]
    cond_c = jax.nn.silu(c_ctx)
    for layer in range(DEPTH):
        last = layer == DEPTH - 1
        j = layer // 2
        mx = jnp.split(cond_x @ ada_w[layer] + ada_b[layer], 6, axis=-1)
        mc = jnp.split(cond_c @ ada_w[layer] + ada_b[layer], 6, axis=-1)
        hx = modulate(rmsnorm(x, norm_mix_w[layer]), mx[0], mx[1])
        hc = modulate(rmsnorm(ctx, norm_mix_w[layer]), mc[0], mc[1])
        if layer % 2 == 0:
            ox, oc = even_mixer(hx, hc, ev_w_in[j], ev_w_out[j], sc_conv_w[j], ssd_conv_w[j],
                                ssd_conv_b[j], ssd_dt_bias[j], ssd_a_log[j], ssd_d[j], ssd_norm_w[j],
                                not last)
        else:
            ox, oc = odd_mixer(hx, hc, od_w_in[j], od_w_out[j], lower_bounds[layer], hg_norm_w[j],
                               not last)
        x = x + mx[2] * ox
        x = x + mx[5] * peer_ffn(modulate(rmsnorm(x, norm_ffn_w[layer]), mx[3], mx[4]),
                                 peer_wq[layer], peer_keys[layer], peer_u[layer], peer_v[layer])
        if not last:
            ctx = ctx + mc[2] * oc
            ctx = ctx + mc[5] * peer_ffn(modulate(rmsnorm(ctx, norm_ffn_w[layer]), mc[3], mc[4]),
                                         peer_wq[layer], peer_keys[layer], peer_u[layer], peer_v[layer])
    return final_rmsnorm(x, norm_f_w)
```

```python
import math
import functools

import jax
import jax.numpy as jnp
from jax import lax
import numpy as np
from jax.experimental import pallas as pl
from jax.experimental.pallas import tpu as pltpu

D_MODEL = 1024
BATCH = 8
SEQ = 4096
DEPTH = 2

GRID_W = 64
CTX_LEN = 256
N_EVEN = (DEPTH + 1) // 2
N_ODD = DEPTH // 2
NORM_EPS = 1e-6
CONV_W = 3
CHUNK = 64
SC_WIDTH = 1024
SSD_HEADS = 16
SSD_HEAD_DIM = 64
SSD_WIDTH = SSD_HEADS * SSD_HEAD_DIM
SSD_GROUPS = 4
SSD_STATE = 128
SSD_CONV_DIM = SSD_WIDTH + 2 * SSD_GROUPS * SSD_STATE
EVEN_IN = 3 * SC_WIDTH + SSD_WIDTH + SSD_CONV_DIM + 2 * SSD_HEADS
EVEN_MIX = SC_WIDTH + SSD_WIDTH
HG_EXPAND = 128
HG_HEADS = D_MODEL // HG_EXPAND
HG_KDIM = HG_EXPAND
HG_VDIM = D_MODEL // HG_HEADS
HG_FWIDTH = HG_HEADS * HG_KDIM
HG_IWIDTH = HG_HEADS * HG_VDIM
ODD_IN = 3 * HG_FWIDTH + 2 * HG_IWIDTH
PEER_HEADS = 8
PEER_NKEYS = 128
PEER_EXPERTS = PEER_NKEYS * PEER_NKEYS
PEER_QDIM = 256
PEER_HALF = PEER_QDIM // 2
PEER_TOPK = 16
PEER_BLOCK = 128


def rmsnorm(x, w):
    xf = x.astype(jnp.float32)
    y = xf * lax.rsqrt(jnp.mean(xf * xf, axis=-1, keepdims=True) + NORM_EPS)
    return (y * w.astype(jnp.float32)).astype(x.dtype)


def group_rmsnorm(y, w, groups):
    shp = y.shape
    yg = y.astype(jnp.float32).reshape(*shp[:-1], groups, shp[-1] // groups)
    yg = yg * lax.rsqrt(jnp.mean(yg * yg, axis=-1, keepdims=True) + NORM_EPS)
    return yg.reshape(shp) * w.astype(jnp.float32)


def modulate(h, shift, scale):
    return h * (1 + scale) + shift


def dwconv(u, w):
    L = u.shape[1]
    pad = CONV_W // 2
    up = jnp.pad(u, ((0, 0), (pad, pad), (0, 0)))
    return sum(up[:, k:k + L] * w[k] for k in range(CONV_W))


def conv_grid(u, w):
    Bsz, S, C = u.shape
    rows = S // GRID_W
    return dwconv(u.reshape(Bsz * rows, GRID_W, C), w).reshape(Bsz, S, C)


def chunk_scan(decay, contrib, s0, keep_starts):
    def step(s, inp):
        d, u = inp
        return d * s + u, (s if keep_starts else None)
    s_fin, starts = lax.scan(step, s0, (jnp.moveaxis(decay, 1, 0), jnp.moveaxis(contrib, 1, 0)))
    return (jnp.moveaxis(starts, 0, 1) if keep_starts else None), s_fin


def ssd_scan(xh, dt, a, bm, cm, s0, with_out):
    Bsz, T, H, P = xh.shape
    G, N = bm.shape[2], bm.shape[3]
    hpg = H // G
    nc = T // CHUNK
    xdt = (xh * dt[..., None]).reshape(Bsz, nc, CHUNK, G, hpg, P)
    bm = bm.reshape(Bsz, nc, CHUNK, G, N)
    la = jnp.cumsum((dt * a).reshape(Bsz, nc, CHUNK, G, hpg), axis=2)
    to_end = jnp.exp(la[:, :, -1:] - la)
    contrib = jnp.einsum('bcsgn,bcsgh,bcsghp->bcghnp', bm, to_end, xdt).reshape(Bsz, nc, H, N, P)
    chunk_decay = jnp.exp(la[:, :, -1]).reshape(Bsz, nc, H, 1, 1)
    starts, s_fin = chunk_scan(chunk_decay, contrib, s0, with_out)
    if not with_out:
        return None, s_fin
    cm = cm.reshape(Bsz, nc, CHUNK, G, N)
    mask = jnp.tril(jnp.ones((CHUNK, CHUNK), dtype=bool))[:, :, None, None]
    seg = la[:, :, :, None] - la[:, :, None]
    lmat = jnp.exp(jnp.where(mask, seg, -jnp.inf))
    cb = jnp.einsum('bctgn,bcsgn->bctsg', cm, bm)
    y = jnp.einsum('bctsg,bctsgh,bcsghp->bctghp', cb, lmat, xdt)
    y = y + jnp.einsum('bctgn,bcghnp,bctgh->bctghp', cm, starts.reshape(Bsz, nc, G, hpg, N, P), jnp.exp(la))
    return y.reshape(Bsz, T, H, P), s_fin


def gla_scan(q, k, v, g, s0, with_out):
    Bsz, T, H, K = q.shape
    V = v.shape[-1]
    nc = T // CHUNK
    q = q.reshape(Bsz, nc, CHUNK, H, K)
    k = k.reshape(Bsz, nc, CHUNK, H, K)
    v = v.reshape(Bsz, nc, CHUNK, H, V)
    gc = jnp.cumsum(g.reshape(Bsz, nc, CHUNK, H, K), axis=2)
    g_end = gc[:, :, -1]
    contrib = jnp.einsum('bcshk,bcshv->bchkv', k * jnp.exp(g_end[:, :, None] - gc), v)
    starts, s_fin = chunk_scan(jnp.exp(g_end)[..., None], contrib, s0, with_out)
    if not with_out:
        return None, s_fin
    qg = q * jnp.exp(gc)
    kg = k * jnp.exp(-gc)
    mask = jnp.tril(jnp.ones((CHUNK, CHUNK), dtype=bool))
    att = jnp.where(mask, jnp.einsum('bcthk,bcshk->bchts', qg, kg), 0.0)
    o = jnp.einsum('bchts,bcshv->bcthv', att, v) + jnp.einsum('bcthk,bchkv->bcthv', qg, starts)
    return o.reshape(Bsz, T, H, V), s_fin


def _flipper(d):
    return (lambda t: jnp.flip(t, axis=1)) if d == 1 else (lambda t: t)


def even_mixer(hx, hc, w_in, w_out, sc_conv_w, ssd_conv_w, ssd_conv_b, dt_bias, a_log, d_skip,
               ssd_norm_w, need_ctx):
    a = -jnp.exp(a_log.astype(jnp.float32))
    dt_bias = dt_bias.astype(jnp.float32)

    def project(h, conv):
        Bsz, T, _ = h.shape
        sc_x, sc_b, sc_c, z, xbc, dt_raw = jnp.split(
            h @ w_in,
            [SC_WIDTH, 2 * SC_WIDTH, 3 * SC_WIDTH, 3 * SC_WIDTH + SSD_WIDTH,
             3 * SC_WIDTH + SSD_WIDTH + SSD_CONV_DIM], axis=-1)
        xbc = jax.nn.silu(conv(xbc, ssd_conv_w) + ssd_conv_b).astype(jnp.float32)
        xs, bm, cm = jnp.split(xbc, [SSD_WIDTH, SSD_WIDTH + SSD_GROUPS * SSD_STATE], axis=-1)
        ssd_in = (xs.reshape(Bsz, T, SSD_HEADS, SSD_HEAD_DIM),
                  bm.reshape(Bsz, T, SSD_GROUPS, SSD_STATE),
                  cm.reshape(Bsz, T, SSD_GROUPS, SSD_STATE),
                  dt_raw.astype(jnp.float32).reshape(Bsz, T, 2, SSD_HEADS))
        return (sc_x, sc_b, sc_c, z), ssd_in

    def run(ssd_in, d, s0, with_out):
        xs, bm, cm, dt_raw = ssd_in
        dt = jax.nn.softplus(dt_raw[:, :, d] + dt_bias[d])
        fl = _flipper(d)
        y, s = ssd_scan(fl(xs), fl(dt), a[d], fl(bm), fl(cm), s0, with_out)
        return (fl(y) if with_out else None), s

    def finish(gates, ssd_in, y_ssd, conv):
        sc_x, sc_b, sc_c, z = gates
        Bsz, T = z.shape[:2]
        y_a = sc_b * conv(sc_c * sc_x, sc_conv_w)
        y_b = (y_ssd + d_skip.astype(jnp.float32)[:, None] * ssd_in[0]).reshape(Bsz, T, SSD_WIDTH)
        y_b = group_rmsnorm(y_b * jax.nn.silu(z.astype(jnp.float32)), ssd_norm_w, SSD_GROUPS)
        return jnp.concatenate([y_a, y_b.astype(y_a.dtype)], axis=-1) @ w_out

    gates_x, ssd_x = project(hx, conv_grid)
    gates_c, ssd_c = project(hc, dwconv)
    Bsz = hx.shape[0]
    ys_x, ys_c = [], []
    for d in range(2):
        s0 = jnp.zeros((Bsz, SSD_HEADS, SSD_STATE, SSD_HEAD_DIM), jnp.float32)
        yc_d, s_ctx = run(ssd_c, d, s0, need_ctx)
        yx_d, _ = run(ssd_x, d, s_ctx, True)
        ys_x.append(yx_d)
        ys_c.append(yc_d)
    out_x = finish(gates_x, ssd_x, ys_x[0] + ys_x[1], conv_grid)
    out_c = finish(gates_c, ssd_c, ys_c[0] + ys_c[1], dwconv) if need_ctx else None
    return out_x, out_c


def odd_mixer(hx, hc, w_in, w_out, lower_bound, norm_w, need_ctx):
    lb = lower_bound.reshape(2, 1, 1, HG_HEADS, HG_KDIM)

    def project(h):
        Bsz, T, _ = h.shape
        q, f_fwd, f_bwd, i, g = jnp.split(
            h @ w_in, [HG_FWIDTH, 2 * HG_FWIDTH, 3 * HG_FWIDTH, 3 * HG_FWIDTH + HG_IWIDTH], axis=-1)
        heads = lambda t, dim: t.astype(jnp.float32).reshape(Bsz, T, HG_HEADS, dim)
        q = heads(jax.nn.silu(q), HG_KDIM) * (HG_KDIM ** -0.5)
        return q, (heads(f_fwd, HG_KDIM), heads(f_bwd, HG_KDIM)), heads(i, HG_VDIM), g

    def run(p, d, s0, with_out):
        q, fs, i, _ = p
        f = lb[d] + (1.0 - lb[d]) * jax.nn.sigmoid(fs[d])
        fl = _flipper(d)
        o, s = gla_scan(fl(q), fl(1.0 - f), fl(i), fl(jnp.log(f)), s0, with_out)
        return (fl(o) if with_out else None), s

    def finish(o, g):
        Bsz, T = g.shape[:2]
        o = group_rmsnorm(o.reshape(Bsz, T, HG_IWIDTH), norm_w, HG_HEADS) * jax.nn.silu(g.astype(jnp.float32))
        return o.astype(g.dtype) @ w_out

    lat, cp = project(hx), project(hc)
    Bsz = hx.shape[0]
    os_x, os_c = [], []
    for d in range(2):
        s0 = jnp.zeros((Bsz, HG_HEADS, HG_KDIM, HG_VDIM), jnp.float32)
        oc_d, s_ctx = run(cp, d, s0, need_ctx)
        ox_d, _ = run(lat, d, s_ctx, True)
        os_x.append(ox_d)
        os_c.append(oc_d)
    out_x = finish(os_x[0] + os_x[1], lat[3])
    out_c = finish(os_c[0] + os_c[1], cp[3]) if need_ctx else None
    return out_x, out_c


PEER_SEL = PEER_HEADS * PEER_TOPK
PEER_QW = PEER_HEADS * PEER_QDIM
LANES = 128
ROW_WORDS = D_MODEL // (2 * LANES)
ROUTE_VMEM_BYTES = 40 * 1024 * 1024


def peer_ffn(x, shift, scale, norm_w, wq, keys, u_tab, v_tab):
    Bsz, T, D = x.shape
    n = Bsz * T
    tiles_per_mod = n // (shift.shape[0] * PEER_TM)
    h, idx_words, gate = peer_route(x.reshape(n, D), shift, scale, norm_w, wq, keys, tiles_per_mod)
    return peer_experts(h, idx_words, gate, u_tab, v_tab).reshape(Bsz, T, D)


def _top16(vals, n_rows):
    rows = lax.broadcasted_iota(jnp.int32, vals.shape, 0)
    top_s, top_i = [], []
    for _ in range(PEER_TOPK):
        m = jnp.max(vals, axis=0, keepdims=True)
        pos = jnp.min(jnp.where(vals == m, rows, n_rows), axis=0, keepdims=True)
        vals = jnp.where(rows == pos, -jnp.inf, vals)
        top_s.append(m)
        top_i.append(pos)
    return jnp.concatenate(top_s, axis=0), jnp.concatenate(top_i, axis=0)


def _pick_row(table, row):
    r = lax.broadcasted_iota(jnp.int32, table.shape, 0)
    return jnp.sum(jnp.where(r == row, table, 0), axis=0, keepdims=True)


def _route_body(x_ref, shift_ref, scale_ref, nw_ref, wq_ref, keys_ref, h_ref, idx_ref, gate_ref,
                q_scr, idx_scr, gate_scr):
    xf = x_ref[...]
    y = xf * lax.rsqrt(jnp.mean(xf * xf, axis=-1, keepdims=True) + NORM_EPS) * nw_ref[...]
    h = y * (1.0 + scale_ref[0]) + shift_ref[0]
    h_ref[...] = h
    q_scr[...] = jnp.dot(h.astype(jnp.bfloat16), wq_ref[...], preferred_element_type=jnp.float32)

    def head(it, carry):
        tok0 = pl.multiple_of((it // PEER_HEADS) * LANES, LANES)
        hd = it % PEER_HEADS
        tops = []
        for a in range(2):
            col = pl.multiple_of(hd * PEER_QDIM + a * PEER_HALF, PEER_HALF)
            qs = q_scr[pl.ds(tok0, LANES), pl.ds(col, PEER_HALF)].astype(jnp.bfloat16)
            sc = lax.dot_general(keys_ref[hd * 2 + a], qs, (((1,), (1,)), ((), ())),
                                 preferred_element_type=jnp.float32)
            tops.append(_top16(sc, PEER_NKEYS))
        (s0, i0), (s1, i1) = tops
        cand = jnp.concatenate([s0[a:a + 1] + s1 for a in range(PEER_TOPK)], axis=0)
        best, pos = _top16(cand, PEER_TOPK * PEER_TOPK)
        ids = jnp.concatenate(
            [_pick_row(i0, pos[r:r + 1] >> 4) * PEER_NKEYS + _pick_row(i1, pos[r:r + 1] & (PEER_TOPK - 1))
             for r in range(PEER_TOPK)], axis=0)
        e = jnp.exp(best - best[0:1])
        r0 = pl.multiple_of(hd * PEER_TOPK, PEER_TOPK)
        idx_scr[pl.ds(r0, PEER_TOPK), pl.ds(tok0, LANES)] = ids * ROW_WORDS
        gate_scr[pl.ds(r0, PEER_TOPK), pl.ds(tok0, LANES)] = e / jnp.sum(e, axis=0, keepdims=True)
        return carry

    lax.fori_loop(0, (x_ref.shape[0] // LANES) * PEER_HEADS, head, 0)
    idx_ref[...] = idx_scr[...].T
    gate_ref[...] = gate_scr[...].T


def peer_route(x, shift, scale, norm_w, wq, keys, tiles_per_mod):
    n = x.shape[0]
    assert n % PEER_TM == 0
    tok = pl.BlockSpec((PEER_TM, D_MODEL), lambda i: (i, 0))
    mod = pl.BlockSpec((1, 1, D_MODEL), lambda i: (i // tiles_per_mod, 0, 0))
    sel = pl.BlockSpec((PEER_TM, PEER_SEL), lambda i: (i, 0))
    return pl.pallas_call(
        _route_body, grid=(n // PEER_TM,),
        in_specs=[tok, mod, mod, pl.BlockSpec((1, D_MODEL), lambda i: (0, 0)),
                  pl.BlockSpec((D_MODEL, PEER_QW), lambda i: (0, 0)),
                  pl.BlockSpec((PEER_HEADS * 2, PEER_NKEYS, PEER_HALF), lambda i: (0, 0, 0))],
        out_specs=[tok, sel, sel],
        out_shape=[jax.ShapeDtypeStruct((n, D_MODEL), jnp.float32),
                   jax.ShapeDtypeStruct((n, PEER_SEL), jnp.int32),
                   jax.ShapeDtypeStruct((n, PEER_SEL), jnp.float32)],
        scratch_shapes=[pltpu.VMEM((PEER_TM, PEER_QW), jnp.float32),
                        pltpu.VMEM((PEER_SEL, PEER_TM), jnp.int32),
                        pltpu.VMEM((PEER_SEL, PEER_TM), jnp.float32)],
        compiler_params=pltpu.CompilerParams(vmem_limit_bytes=ROUTE_VMEM_BYTES),
        name="peer_route",
    )(x, shift, scale, norm_w.reshape(1, D_MODEL), wq.astype(jnp.bfloat16),
      keys.reshape(PEER_HEADS * 2, PEER_NKEYS, PEER_HALF).astype(jnp.bfloat16))
TILE_STRIDE = PEER_SEL + 8
PEER_GROUP = 4
PEER_TM = 256
PEER_VMEM_BYTES = 48 * 1024 * 1024


def pack_table(tab):
    E = tab.shape[0]
    bits = lax.bitcast_convert_type(tab.astype(jnp.bfloat16), jnp.uint16).astype(jnp.uint32)
    words = bits[:, :D_MODEL // 2] | (bits[:, D_MODEL // 2:] << 16)
    return lax.bitcast_convert_type(words, jnp.int32).reshape(E * ROW_WORDS, LANES)


def _gather_rows(idx_ref, tab_ref, tile_ref, t):
    row = idx_ref.at[t]
    for k in range(PEER_SEL):
        e = pl.multiple_of(row[k], ROW_WORDS)
        tile_ref[pl.ds(k, ROW_WORDS, stride=TILE_STRIDE), :] = tab_ref[pl.ds(e, ROW_WORDS), :]


def _unpack_rows(tile_ref):
    lo, hi = [], []
    for j in range(ROW_WORDS):
        w = tile_ref[j * TILE_STRIDE:j * TILE_STRIDE + PEER_SEL, :]
        lo.append(pltpu.bitcast(w << 16, jnp.float32))
        hi.append(pltpu.bitcast(w & jnp.int32(-65536), jnp.float32))
    return jnp.concatenate(lo + hi, axis=1).astype(jnp.bfloat16)


def _peer_u_body(idx_ref, h_ref, gate_ref, tab_ref, act_ref, *tiles):
    def group(g, carry):
        for i in range(PEER_GROUP):
            _gather_rows(idx_ref, tab_ref, tiles[i], g * PEER_GROUP + i)
        for i in range(PEER_GROUP):
            t = g * PEER_GROUP + i
            hb = jnp.broadcast_to(h_ref[pl.ds(t, 1), :], (8, D_MODEL)).astype(jnp.bfloat16)
            s = lax.dot_general(hb, _unpack_rows(tiles[i]), (((1,), (1,)), ((), ())),
                                preferred_element_type=jnp.float32)
            act_ref[pl.ds(t, 1), :] = s[0:1]
        return carry

    lax.fori_loop(0, h_ref.shape[0] // PEER_GROUP, group, 0)
    d = act_ref[...]
    act_ref[...] = 0.5 * d * (1.0 + lax.erf(d * (2.0 ** -0.5))) * gate_ref[...]


def _peer_v_body(idx_ref, act_ref, tab_ref, out_ref, *tiles):
    def group(g, carry):
        for i in range(PEER_GROUP):
            _gather_rows(idx_ref, tab_ref, tiles[i], g * PEER_GROUP + i)
        for i in range(PEER_GROUP):
            t = g * PEER_GROUP + i
            ab = jnp.broadcast_to(act_ref[pl.ds(t, 1), :], (8, PEER_SEL)).astype(jnp.bfloat16)
            o = jnp.dot(ab, _unpack_rows(tiles[i]), preferred_element_type=jnp.float32)
            out_ref[pl.ds(t, 1), :] = o[0:1]
        return carry

    lax.fori_loop(0, act_ref.shape[0] // PEER_GROUP, group, 0)


def peer_experts(h, idx_words, gate, u_tab, v_tab):
    n = h.shape[0]
    assert n % PEER_TM == 0
    grid = (n // PEER_TM,)
    tiles = [pltpu.VMEM((ROW_WORDS * TILE_STRIDE, LANES), jnp.int32)] * PEER_GROUP
    params = pltpu.CompilerParams(vmem_limit_bytes=PEER_VMEM_BYTES)
    idx_spec = pl.BlockSpec((PEER_TM, PEER_SEL), lambda i: (i, 0), memory_space=pltpu.SMEM)
    sel_spec = pl.BlockSpec((PEER_TM, PEER_SEL), lambda i: (i, 0))
    tok_spec = pl.BlockSpec((PEER_TM, D_MODEL), lambda i: (i, 0))
    table_spec = pl.BlockSpec(memory_space=pltpu.VMEM)
    act = pl.pallas_call(
        _peer_u_body, grid=grid,
        in_specs=[idx_spec, tok_spec, sel_spec, table_spec],
        out_specs=sel_spec,
        out_shape=jax.ShapeDtypeStruct((n, PEER_SEL), jnp.float32),
        scratch_shapes=tiles, compiler_params=params, name="peer_u",
    )(idx_words, h, gate, pack_table(u_tab))
    return pl.pallas_call(
        _peer_v_body, grid=grid,
        in_specs=[idx_spec, sel_spec, table_spec],
        out_specs=tok_spec,
        out_shape=jax.ShapeDtypeStruct((n, D_MODEL), jnp.float32),
        scratch_shapes=tiles, compiler_params=params, name="peer_v",
    )(idx_words, act, pack_table(v_tab))


def _final_norm_body(x_ref, w_ref, o_ref):
    xf = x_ref[...]
    y = xf * lax.rsqrt(jnp.mean(xf * xf, axis=-1, keepdims=True) + NORM_EPS)
    o_ref[...] = y * w_ref[...]


def final_rmsnorm(x, w):
    Bsz, S, D = x.shape
    n = Bsz * S
    tm = 1024
    out = pl.pallas_call(
        _final_norm_body,
        grid=(n // tm,),
        in_specs=[pl.BlockSpec((tm, D), lambda i: (i, 0)), pl.BlockSpec((1, D), lambda i: (0, 0))],
        out_specs=pl.BlockSpec((tm, D), lambda i: (i, 0)),
        out_shape=jax.ShapeDtypeStruct((n, D), x.dtype),
        name="final_rmsnorm",
    )(x.reshape(n, D), w.reshape(1, D))
    return out.reshape(Bsz, S, D)


def kernel(x, c, ctx, c_ctx, ada_w, ada_b, norm_mix_w, norm_ffn_w, norm_f_w, ev_w_in, ev_w_out,
           sc_conv_w, ssd_conv_w, ssd_conv_b, ssd_dt_bias, ssd_a_log, ssd_d, ssd_norm_w,
           od_w_in, od_w_out, hg_lb_logits, hg_norm_w, peer_wq, peer_keys, peer_u, peer_v):
    lb_sm = jax.nn.softmax(hg_lb_logits.astype(jnp.float32), axis=0)
    lower_bounds = jnp.cumsum(lb_sm, axis=0) - lb_sm[0]
    cond_x = jax.nn.silu(c)[:, None, :]
    cond_c = jax.nn.silu(c_ctx)
    for layer in range(DEPTH):
        last = layer == DEPTH - 1
        j = layer // 2
        mx = jnp.split(cond_x @ ada_w[layer] + ada_b[layer], 6, axis=-1)
        mc = jnp.split(cond_c @ ada_w[layer] + ada_b[layer], 6, axis=-1)
        hx = modulate(rmsnorm(x, norm_mix_w[layer]), mx[0], mx[1])
        hc = modulate(rmsnorm(ctx, norm_mix_w[layer]), mc[0], mc[1])
        if layer % 2 == 0:
            ox, oc = even_mixer(hx, hc, ev_w_in[j], ev_w_out[j], sc_conv_w[j], ssd_conv_w[j],
                                ssd_conv_b[j], ssd_dt_bias[j], ssd_a_log[j], ssd_d[j], ssd_norm_w[j],
                                not last)
        else:
            ox, oc = odd_mixer(hx, hc, od_w_in[j], od_w_out[j], lower_bounds[layer], hg_norm_w[j],
                               not last)
        x = x + mx[2] * ox
        x = x + mx[5] * peer_ffn(x, mx[3], mx[4], norm_ffn_w[layer],
                                 peer_wq[layer], peer_keys[layer], peer_u[layer], peer_v[layer])
        if not last:
            ctx = ctx + mc[2] * oc
            ctx = ctx + mc[5] * peer_ffn(ctx, mc[3].reshape(1, 1, D_MODEL), mc[4].reshape(1, 1, D_MODEL),
                                         norm_ffn_w[layer],
                                         peer_wq[layer], peer_keys[layer], peer_u[layer], peer_v[layer])
    return final_rmsnorm(x, norm_f_w)
```

```python
import functools

import jax
import jax.numpy as jnp
from jax import lax
from jax.experimental import pallas as pl
from jax.experimental.pallas import tpu as pltpu

D_MODEL = 1024
DEPTH = 2
GRID_W = 64
CTX_LEN = 256
NORM_EPS = 1e-6
CONV_W = 3
SC_WIDTH = 1024
SSD_HEADS = 16
SSD_HEAD_DIM = 64
SSD_WIDTH = SSD_HEADS * SSD_HEAD_DIM
SSD_GROUPS = 4
SSD_STATE = 128
SSD_BC = SSD_GROUPS * SSD_STATE
SSD_CONV_DIM = SSD_WIDTH + 2 * SSD_BC
EVEN_MIX = SC_WIDTH + SSD_WIDTH
HG_HEADS = 8
HG_KDIM = 128
HG_VDIM = 128
HG_WIDTH = HG_HEADS * HG_KDIM
PEER_HEADS = 8
PEER_NKEYS = 128
PEER_QDIM = 256
PEER_HALF = PEER_QDIM // 2
PEER_TOPK = 16
PEER_SEL = PEER_HEADS * PEER_TOPK
PEER_QW = PEER_HEADS * PEER_QDIM

LANES = 128
TOKEN_TILE = 256
SSD_CHUNK = 128
GLA_CHUNK = 64
ROW_WORDS = D_MODEL // (2 * LANES)
TILE_STRIDE = PEER_SEL + 8
PEER_GROUP = 4
MIB = 1024 * 1024
PEER_VMEM_BYTES = 48 * MIB
MIX_VMEM_BYTES = 48 * MIB
HI = lax.Precision.HIGHEST
BF = jnp.bfloat16
F32 = jnp.float32


def _mod_spec(tiles_per_seq, ctx_tiles):
    return pl.BlockSpec((1, 1, 1, D_MODEL),
                        lambda i: (i // tiles_per_seq, jnp.where(i % tiles_per_seq < ctx_tiles, 0, 1), 0, 0))


def _tok_spec(width):
    return pl.BlockSpec((TOKEN_TILE, width), lambda i: (i, 0))


def _row_spec(width):
    return pl.BlockSpec((1, width), lambda i: (0, 0))


_RESIDENT = pl.BlockSpec(memory_space=pltpu.VMEM)


def _norm_mod(x, nw, shift, scale):
    y = x * lax.rsqrt(jnp.mean(x * x, axis=-1, keepdims=True) + NORM_EPS) * nw
    return y * (1.0 + scale) + shift


def _silu(x):
    return x * jax.nn.sigmoid(x)


def _ada_body(c_ref, w_ref, b_ref, o_ref):
    cond = _silu(c_ref[...]).astype(BF)
    o_ref[...] = jnp.dot(cond, w_ref[...].astype(BF), preferred_element_type=F32) + b_ref[...]


def ada_modulation(cond, w, b):
    rows = cond.shape[0]
    return pl.pallas_call(
        _ada_body, grid=(6,),
        in_specs=[pl.BlockSpec((rows, D_MODEL), lambda j: (0, 0)),
                  pl.BlockSpec((D_MODEL, D_MODEL), lambda j: (0, j)),
                  pl.BlockSpec((1, D_MODEL), lambda j: (0, j))],
        out_specs=pl.BlockSpec((rows, D_MODEL), lambda j: (0, j)),
        out_shape=jax.ShapeDtypeStruct((rows, 6 * D_MODEL), F32),
        name="ada_modulation",
    )(cond, w, b.reshape(1, 6 * D_MODEL))


EVEN_W_COLS = 3 * SC_WIDTH + SSD_WIDTH + SSD_CONV_DIM + 2 * LANES


def _even_in_body(x_ref, shift_ref, scale_ref, nw_ref, w_ref, cw_sc_ref, cw_ssd_ref, cb_ssd_ref,
                  ya_ref, z_ref, xs_ref, bm_ref, cm_ref, dt_ref, *, tiles_per_seq, ctx_tiles):
    T = TOKEN_TILE
    h = _norm_mod(x_ref[...], nw_ref[...], shift_ref[0, 0], scale_ref[0, 0]).astype(BF)
    is_ctx = (pl.program_id(0) % tiles_per_seq) < ctx_tiles
    row_len = jnp.where(is_ctx, T, GRID_W)
    pos = lax.broadcasted_iota(jnp.int32, (T, 1), 0) & (row_len - 1)
    first, last = pos == 0, pos == row_len - 1

    def conv(u, cw_ref):
        prev = jnp.where(first, 0.0, pltpu.roll(u, 1, 0))
        nxt = jnp.where(last, 0.0, pltpu.roll(u, T - 1, 0))
        return prev * cw_ref[0:1, :] + u * cw_ref[1:2, :] + nxt * cw_ref[2:3, :]

    def proj(lo, width):
        return jnp.dot(h, w_ref[:, lo:lo + width], preferred_element_type=F32)

    sc_x, sc_b, sc_c = proj(0, SC_WIDTH), proj(SC_WIDTH, SC_WIDTH), proj(2 * SC_WIDTH, SC_WIDTH)
    ya_ref[...] = sc_b * conv(sc_c * sc_x, cw_sc_ref)
    z_ref[...] = proj(3 * SC_WIDTH, SSD_WIDTH)
    xbc = _silu(conv(proj(3 * SC_WIDTH + SSD_WIDTH, SSD_CONV_DIM), cw_ssd_ref) + cb_ssd_ref[...])
    xs_ref[...] = xbc[:, :SSD_WIDTH]
    bm_ref[...] = xbc[:, SSD_WIDTH:SSD_WIDTH + SSD_BC]
    cm_ref[...] = xbc[:, SSD_WIDTH + SSD_BC:]
    dt_ref[...] = proj(3 * SC_WIDTH + SSD_WIDTH + SSD_CONV_DIM, 2 * LANES)


def even_in(x, mod_shift, mod_scale, norm_w, w_in, sc_conv_w, ssd_conv_w, ssd_conv_b, tiles_per_seq, ctx_tiles):
    n = x.shape[0]
    main = 3 * SC_WIDTH + SSD_WIDTH + SSD_CONV_DIM
    w_dt = jnp.zeros((D_MODEL, 2, LANES), F32).at[:, :, :SSD_HEADS].set(w_in[:, main:].reshape(D_MODEL, 2, SSD_HEADS))
    w = jnp.concatenate([w_in[:, :main], w_dt.reshape(D_MODEL, 2 * LANES)], axis=1).astype(BF)
    mod = _mod_spec(tiles_per_seq, ctx_tiles)
    widths = (SC_WIDTH, SSD_WIDTH, SSD_WIDTH, SSD_BC, SSD_BC, 2 * LANES)
    return pl.pallas_call(
        functools.partial(_even_in_body, tiles_per_seq=tiles_per_seq, ctx_tiles=ctx_tiles),
        grid=(n // TOKEN_TILE,),
        in_specs=[_tok_spec(D_MODEL), mod, mod, _row_spec(D_MODEL), _RESIDENT,
                  pl.BlockSpec((CONV_W, SC_WIDTH), lambda i: (0, 0)),
                  pl.BlockSpec((CONV_W, SSD_CONV_DIM), lambda i: (0, 0)), _row_spec(SSD_CONV_DIM)],
        out_specs=[_tok_spec(wd) for wd in widths],
        out_shape=[jax.ShapeDtypeStruct((n, wd), F32) for wd in widths],
        compiler_params=pltpu.CompilerParams(vmem_limit_bytes=MIX_VMEM_BYTES),
        name="even_in",
    )(x, mod_shift, mod_scale, norm_w.reshape(1, D_MODEL), w, sc_conv_w, ssd_conv_w,
      ssd_conv_b.reshape(1, SSD_CONV_DIM))


def _scan_row(n_chunks, ctx_chunks):
    def row(b, d, c):
        back = jnp.where(c < ctx_chunks, ctx_chunks - 1 - c, n_chunks - 1 + ctx_chunks - c)
        return b * n_chunks + jnp.where(d == 0, c, back)
    return row


def _causal_mask(L, d):
    r = lax.broadcasted_iota(jnp.int32, (L, L), 0)
    c = lax.broadcasted_iota(jnp.int32, (L, L), 1)
    return jnp.where(d == 0, c - r, r - c) <= 0


SSD_GW = (SSD_HEADS // SSD_GROUPS) * SSD_HEAD_DIM


def _ssd_body(xs_ref, bm_ref, cm_ref, dt_ref, bias_ref, a_ref, y_ref, state):
    d = pl.program_id(1)
    L = SSD_CHUNK

    @pl.when(pl.program_id(2) == 0)
    def _():
        state[...] = jnp.zeros_like(state)

    mask = _causal_mask(L, d)
    v = dt_ref[...] + bias_ref[0]
    dt = jnp.maximum(v, 0.0) + jnp.log1p(jnp.exp(-jnp.abs(v)))
    la = jnp.dot(mask.astype(F32), dt * a_ref[0], precision=HI, preferred_element_type=F32)
    la_t = la.T
    head_of_lane = lax.broadcasted_iota(jnp.int32, (LANES, SSD_WIDTH), 1) // SSD_HEAD_DIM
    expand = (lax.broadcasted_iota(jnp.int32, (LANES, SSD_WIDTH), 0) == head_of_lane).astype(F32)
    dt_x = jnp.dot(dt, expand, precision=HI, preferred_element_type=F32)
    la_x = jnp.dot(la, expand, precision=HI, preferred_element_type=F32)
    end_x = jnp.where(d == 0, la_x[L - 1:L], la_x[0:1])
    xdt = xs_ref[...] * dt_x
    w_state = (xdt * jnp.exp(end_x - la_x)).astype(BF)
    xdt_b = xdt.astype(BF)
    e_la = jnp.exp(la_x)
    dec = jnp.exp(end_x)
    for g in range(SSD_GROUPS):
        bm = bm_ref[:, g * SSD_STATE:(g + 1) * SSD_STATE].astype(BF)
        cm = cm_ref[:, g * SSD_STATE:(g + 1) * SSD_STATE].astype(BF)
        cb = lax.dot_general(cm, bm, (((1,), (1,)), ((), ())), preferred_element_type=F32)
        lanes = slice(g * SSD_GW, (g + 1) * SSD_GW)
        s_old = state[g]
        y_in = jnp.dot(cm, s_old.astype(BF), preferred_element_type=F32) * e_la[:, lanes]
        ys = []
        for j in range(SSD_HEADS // SSD_GROUPS):
            h = g * (SSD_HEADS // SSD_GROUPS) + j
            seg = la[:, h:h + 1] - la_t[h:h + 1, :]
            m = (cb * jnp.exp(jnp.where(mask, seg, -jnp.inf))).astype(BF)
            ys.append(jnp.dot(m, xdt_b[:, h * SSD_HEAD_DIM:(h + 1) * SSD_HEAD_DIM], preferred_element_type=F32))
        y_ref[0, :, lanes] = y_in + jnp.concatenate(ys, axis=1)
        state[g] = dec[:, lanes] * s_old + jnp.dot(bm.T, w_state[:, lanes], preferred_element_type=F32)


def ssd_scan(xs, bm, cm, dt_raw, dt_bias, a, n_seq, ctx_chunks):
    n = xs.shape[0]
    n_chunks = n // n_seq // SSD_CHUNK
    row = _scan_row(n_chunks, ctx_chunks)
    pad = lambda p: jnp.zeros((2, 1, LANES), F32).at[:, 0, :SSD_HEADS].set(p)
    tok = lambda width: pl.BlockSpec((SSD_CHUNK, width), lambda b, d, c: (row(b, d, c), 0))
    per_dir = pl.BlockSpec((1, 1, LANES), lambda b, d, c: (d, 0, 0))
    return pl.pallas_call(
        _ssd_body, grid=(n_seq, 2, n_chunks),
        in_specs=[tok(SSD_WIDTH), tok(SSD_BC), tok(SSD_BC),
                  pl.BlockSpec((SSD_CHUNK, LANES), lambda b, d, c: (row(b, d, c), d)), per_dir, per_dir],
        out_specs=pl.BlockSpec((1, SSD_CHUNK, SSD_WIDTH), lambda b, d, c: (d, row(b, d, c), 0)),
        out_shape=jax.ShapeDtypeStruct((2, n, SSD_WIDTH), F32),
        scratch_shapes=[pltpu.VMEM((SSD_GROUPS, SSD_STATE, SSD_GW), F32)],
        compiler_params=pltpu.CompilerParams(dimension_semantics=("arbitrary", "arbitrary", "arbitrary")),
        name="ssd_scan",
    )(xs, bm, cm, dt_raw, pad(dt_bias), pad(a))


def _gla_body(q_ref, f_ref, v_ref, lb_ref, o_ref, state):
    d = pl.program_id(1)
    L = GLA_CHUNK

    @pl.when(pl.program_id(2) == 0)
    def _():
        state[...] = jnp.zeros_like(state)

    mask = _causal_mask(L, d)
    lb = lb_ref[0]
    f = lb + (1.0 - lb) * jax.nn.sigmoid(f_ref[0])
    gc = jnp.dot(mask.astype(F32), jnp.log(f), precision=HI, preferred_element_type=F32)
    g_end = jnp.where(d == 0, gc[L - 1:L], gc[0:1])
    k = 1.0 - f
    qg = (q_ref[...] * jnp.exp(gc)).astype(BF)
    kg = (k * jnp.exp(-gc)).astype(BF)
    k_end = (k * jnp.exp(g_end - gc)).astype(BF)
    dec = jnp.exp(g_end)
    vb = v_ref[...].astype(BF)
    for h in range(HG_HEADS):
        lanes = slice(h * HG_KDIM, (h + 1) * HG_KDIM)
        att = lax.dot_general(qg[:, lanes], kg[:, lanes], (((1,), (1,)), ((), ())), preferred_element_type=F32)
        att = jnp.where(mask, att, 0.0).astype(BF)
        s_old = state[h]
        o = jnp.dot(att, vb[:, lanes], preferred_element_type=F32)
        o_ref[0, :, lanes] = o + lax.dot_general(qg[:, lanes], s_old.astype(BF), (((1,), (1,)), ((), ())),
                                                 preferred_element_type=F32)
        state[h] = s_old * dec[:, lanes] + jnp.dot(vb[:, lanes].T, k_end[:, lanes], preferred_element_type=F32)


def gla_scan(q, f_raw, v, lb, n_seq, ctx_chunks):
    n = q.shape[0]
    n_chunks = n // n_seq // GLA_CHUNK
    row = _scan_row(n_chunks, ctx_chunks)
    tok = pl.BlockSpec((GLA_CHUNK, HG_WIDTH), lambda b, d, c: (row(b, d, c), 0))
    dtok = pl.BlockSpec((1, GLA_CHUNK, HG_WIDTH), lambda b, d, c: (d, row(b, d, c), 0))
    return pl.pallas_call(
        _gla_body, grid=(n_seq, 2, n_chunks),
        in_specs=[tok, dtok, tok, pl.BlockSpec((1, 1, HG_WIDTH), lambda b, d, c: (d, 0, 0))],
        out_specs=dtok,
        out_shape=jax.ShapeDtypeStruct((2, n, HG_WIDTH), F32),
        scratch_shapes=[pltpu.VMEM((HG_HEADS, HG_VDIM, HG_KDIM), F32)],
        compiler_params=pltpu.CompilerParams(dimension_semantics=("arbitrary", "arbitrary", "arbitrary")),
        name="gla_scan",
    )(q, f_raw, v, lb)


def _group_rmsnorm(y, groups):
    width = y.shape[1] // groups
    parts = []
    for g in range(groups):
        seg = y[:, g * width:(g + 1) * width]
        parts.append(seg * lax.rsqrt(jnp.mean(seg * seg, axis=-1, keepdims=True) + NORM_EPS))
    return jnp.concatenate(parts, axis=1)


def _even_out_body(x_ref, gate_ref, ya_ref, z_ref, xs_ref, y0_ref, y1_ref, dskip_ref, nw_ref, w_ref, o_ref):
    yb = (y0_ref[0] + y1_ref[0] + dskip_ref[...] * xs_ref[...]) * _silu(z_ref[...])
    yb = _group_rmsnorm(yb, SSD_GROUPS) * nw_ref[...]
    mix = jnp.concatenate([ya_ref[...], yb], axis=1).astype(BF)
    o_ref[...] = x_ref[...] + gate_ref[0, 0] * jnp.dot(mix, w_ref[...], preferred_element_type=F32)


def even_out(x, mod_gate, ya, z, xs, ys, d_skip, norm_w, w_out, tiles_per_seq, ctx_tiles):
    n = x.shape[0]
    ydir = lambda d: pl.BlockSpec((1, TOKEN_TILE, SSD_WIDTH), lambda i: (d, i, 0))
    return pl.pallas_call(
        _even_out_body, grid=(n // TOKEN_TILE,),
        in_specs=[_tok_spec(D_MODEL), _mod_spec(tiles_per_seq, ctx_tiles), _tok_spec(SC_WIDTH), _tok_spec(SSD_WIDTH),
                  _tok_spec(SSD_WIDTH), ydir(0), ydir(1), _row_spec(SSD_WIDTH), _row_spec(SSD_WIDTH), _RESIDENT],
        out_specs=_tok_spec(D_MODEL),
        out_shape=jax.ShapeDtypeStruct((n, D_MODEL), F32),
        compiler_params=pltpu.CompilerParams(vmem_limit_bytes=MIX_VMEM_BYTES),
        name="even_out",
    )(x, mod_gate, ya, z, xs, ys, ys, jnp.repeat(d_skip, SSD_HEAD_DIM).reshape(1, SSD_WIDTH),
      norm_w.reshape(1, SSD_WIDTH), w_out.astype(BF))


def _odd_in_body(x_ref, shift_ref, scale_ref, nw_ref, w_ref, q_ref, f_ref, v_ref, g_ref):
    h = _norm_mod(x_ref[...], nw_ref[...], shift_ref[0, 0], scale_ref[0, 0]).astype(BF)
    proj = lambda j: jnp.dot(h, w_ref[:, j * HG_WIDTH:(j + 1) * HG_WIDTH], preferred_element_type=F32)
    q_ref[...] = _silu(proj(0)) * (HG_KDIM ** -0.5)
    f_ref[0] = proj(1)
    f_ref[1] = proj(2)
    v_ref[...] = proj(3)
    g_ref[...] = proj(4)


def odd_in(x, mod_shift, mod_scale, norm_w, w_in, tiles_per_seq, ctx_tiles):
    n = x.shape[0]
    mod = _mod_spec(tiles_per_seq, ctx_tiles)
    tok = _tok_spec(HG_WIDTH)
    sds = jax.ShapeDtypeStruct((n, HG_WIDTH), F32)
    return pl.pallas_call(
        _odd_in_body, grid=(n // TOKEN_TILE,),
        in_specs=[_tok_spec(D_MODEL), mod, mod, _row_spec(D_MODEL), _RESIDENT],
        out_specs=[tok, pl.BlockSpec((2, TOKEN_TILE, HG_WIDTH), lambda i: (0, i, 0)), tok, tok],
        out_shape=[sds, jax.ShapeDtypeStruct((2, n, HG_WIDTH), F32), sds, sds],
        compiler_params=pltpu.CompilerParams(vmem_limit_bytes=MIX_VMEM_BYTES),
        name="odd_in",
    )(x, mod_shift, mod_scale, norm_w.reshape(1, D_MODEL), w_in.astype(BF))


def _odd_out_body(x_ref, gate_ref, o0_ref, o1_ref, g_ref, nw_ref, w_ref, out_ref):
    o = _group_rmsnorm(o0_ref[0] + o1_ref[0], HG_HEADS) * nw_ref[...] * _silu(g_ref[...])
    out_ref[...] = x_ref[...] + gate_ref[0, 0] * jnp.dot(o.astype(BF), w_ref[...], preferred_element_type=F32)


def odd_out(x, mod_gate, os_, g, norm_w, w_out, tiles_per_seq, ctx_tiles):
    n = x.shape[0]
    odir = lambda d: pl.BlockSpec((1, TOKEN_TILE, HG_WIDTH), lambda i: (d, i, 0))
    return pl.pallas_call(
        _odd_out_body, grid=(n // TOKEN_TILE,),
        in_specs=[_tok_spec(D_MODEL), _mod_spec(tiles_per_seq, ctx_tiles), odir(0), odir(1), _tok_spec(HG_WIDTH),
                  _row_spec(HG_WIDTH), _RESIDENT],
        out_specs=_tok_spec(D_MODEL),
        out_shape=jax.ShapeDtypeStruct((n, D_MODEL), F32),
        compiler_params=pltpu.CompilerParams(vmem_limit_bytes=MIX_VMEM_BYTES),
        name="odd_out",
    )(x, mod_gate, os_, os_, g, norm_w.reshape(1, HG_WIDTH), w_out.astype(BF))


def _top16(vals, n_rows):
    rows = lax.broadcasted_iota(jnp.int32, vals.shape, 0)
    top_s, top_i = [], []
    for _ in range(PEER_TOPK):
        m = jnp.max(vals, axis=0, keepdims=True)
        pos = jnp.min(jnp.where(vals == m, rows, n_rows), axis=0, keepdims=True)
        vals = jnp.where(rows == pos, -jnp.inf, vals)
        top_s.append(m)
        top_i.append(pos)
    return jnp.concatenate(top_s, axis=0), jnp.concatenate(top_i, axis=0)


def _pick_row(table, row):
    r = lax.broadcasted_iota(jnp.int32, table.shape, 0)
    return jnp.sum(jnp.where(r == row, table, 0), axis=0, keepdims=True)


def _route_body(x_ref, shift_ref, scale_ref, nw_ref, wq_ref, keys_ref, h_ref, idx_ref, gate_ref,
                q_scr, idx_scr, gate_scr):
    h = _norm_mod(x_ref[...], nw_ref[...], shift_ref[0, 0], scale_ref[0, 0])
    h_ref[...] = h
    q_scr[...] = jnp.dot(h.astype(BF), wq_ref[...], preferred_element_type=F32)

    def head(it, carry):
        tok0 = pl.multiple_of((it // PEER_HEADS) * LANES, LANES)
        hd = it % PEER_HEADS
        tops = []
        for a in range(2):
            col = pl.multiple_of(hd * PEER_QDIM + a * PEER_HALF, PEER_HALF)
            qs = q_scr[pl.ds(tok0, LANES), pl.ds(col, PEER_HALF)].astype(BF)
            sc = lax.dot_general(keys_ref[hd * 2 + a], qs, (((1,), (1,)), ((), ())),
                                 preferred_element_type=F32)
            tops.append(_top16(sc, PEER_NKEYS))
        (s0, i0), (s1, i1) = tops
        cand = jnp.concatenate([s0[a:a + 1] + s1 for a in range(PEER_TOPK)], axis=0)
        best, pos = _top16(cand, PEER_TOPK * PEER_TOPK)
        ids = jnp.concatenate(
            [_pick_row(i0, pos[r:r + 1] >> 4) * PEER_NKEYS + _pick_row(i1, pos[r:r + 1] & (PEER_TOPK - 1))
             for r in range(PEER_TOPK)], axis=0)
        e = jnp.exp(best - best[0:1])
        r0 = pl.multiple_of(hd * PEER_TOPK, PEER_TOPK)
        idx_scr[pl.ds(r0, PEER_TOPK), pl.ds(tok0, LANES)] = ids * ROW_WORDS
        gate_scr[pl.ds(r0, PEER_TOPK), pl.ds(tok0, LANES)] = e / jnp.sum(e, axis=0, keepdims=True)
        return carry

    lax.fori_loop(0, (TOKEN_TILE // LANES) * PEER_HEADS, head, 0)
    idx_ref[...] = idx_scr[...].T
    gate_ref[...] = gate_scr[...].T


def peer_route(x, mod_shift, mod_scale, norm_w, wq, keys, tiles_per_seq, ctx_tiles):
    n = x.shape[0]
    mod = _mod_spec(tiles_per_seq, ctx_tiles)
    sel = _tok_spec(PEER_SEL)
    return pl.pallas_call(
        _route_body, grid=(n // TOKEN_TILE,),
        in_specs=[_tok_spec(D_MODEL), mod, mod, _row_spec(D_MODEL),
                  pl.BlockSpec((D_MODEL, PEER_QW), lambda i: (0, 0)),
                  pl.BlockSpec((PEER_HEADS * 2, PEER_NKEYS, PEER_HALF), lambda i: (0, 0, 0))],
        out_specs=[_tok_spec(D_MODEL), sel, sel],
        out_shape=[jax.ShapeDtypeStruct((n, D_MODEL), F32),
                   jax.ShapeDtypeStruct((n, PEER_SEL), jnp.int32),
                   jax.ShapeDtypeStruct((n, PEER_SEL), F32)],
        scratch_shapes=[pltpu.VMEM((TOKEN_TILE, PEER_QW), F32),
                        pltpu.VMEM((PEER_SEL, TOKEN_TILE), jnp.int32),
                        pltpu.VMEM((PEER_SEL, TOKEN_TILE), F32)],
        compiler_params=pltpu.CompilerParams(vmem_limit_bytes=MIX_VMEM_BYTES),
        name="peer_route",
    )(x, mod_shift, mod_scale, norm_w.reshape(1, D_MODEL), wq.astype(BF),
      keys.reshape(PEER_HEADS * 2, PEER_NKEYS, PEER_HALF).astype(BF))


def pack_table(tab):
    E = tab.shape[0]
    bits = lax.bitcast_convert_type(tab.astype(BF), jnp.uint16).astype(jnp.uint32)
    words = bits[:, :D_MODEL // 2] | (bits[:, D_MODEL // 2:] << 16)
    return lax.bitcast_convert_type(words, jnp.int32).reshape(E * ROW_WORDS, LANES)


def _gather_rows(idx_ref, tab_ref, tile_ref, t):
    row = idx_ref.at[t]
    for k in range(PEER_SEL):
        e = pl.multiple_of(row[k], ROW_WORDS)
        tile_ref[pl.ds(k, ROW_WORDS, stride=TILE_STRIDE), :] = tab_ref[pl.ds(e, ROW_WORDS), :]


def _unpack_rows(tile_ref):
    lo, hi = [], []
    for j in range(ROW_WORDS):
        w = tile_ref[j * TILE_STRIDE:j * TILE_STRIDE + PEER_SEL, :]
        lo.append(pltpu.bitcast(w << 16, F32))
        hi.append(pltpu.bitcast(w & jnp.int32(-65536), F32))
    return jnp.concatenate(lo + hi, axis=1).astype(BF)


def _peer_u_body(idx_ref, h_ref, gate_ref, tab_ref, act_ref, *tiles):
    def group(g, carry):
        for i in range(PEER_GROUP):
            _gather_rows(idx_ref, tab_ref, tiles[i], g * PEER_GROUP + i)
        for i in range(PEER_GROUP):
            t = g * PEER_GROUP + i
            hb = jnp.broadcast_to(h_ref[pl.ds(t, 1), :], (8, D_MODEL)).astype(BF)
            s = lax.dot_general(hb, _unpack_rows(tiles[i]), (((1,), (1,)), ((), ())), preferred_element_type=F32)
            act_ref[pl.ds(t, 1), :] = s[0:1]
        return carry

    lax.fori_loop(0, TOKEN_TILE // PEER_GROUP, group, 0)
    d = act_ref[...]
    act_ref[...] = 0.5 * d * (1.0 + lax.erf(d * (2.0 ** -0.5))) * gate_ref[...]


def _peer_v_body(idx_ref, act_ref, x_ref, gate_ref, tab_ref, out_ref, *tiles):
    def group(g, carry):
        for i in range(PEER_GROUP):
            _gather_rows(idx_ref, tab_ref, tiles[i], g * PEER_GROUP + i)
        for i in range(PEER_GROUP):
            t = g * PEER_GROUP + i
            ab = jnp.broadcast_to(act_ref[pl.ds(t, 1), :], (8, PEER_SEL)).astype(BF)
            o = jnp.dot(ab, _unpack_rows(tiles[i]), preferred_element_type=F32)
            out_ref[pl.ds(t, 1), :] = o[0:1]
        return carry

    lax.fori_loop(0, TOKEN_TILE // PEER_GROUP, group, 0)
    out_ref[...] = x_ref[...] + gate_ref[0, 0] * out_ref[...]


def peer_experts(x, h, idx_words, gate, mod_gate, u_pack, v_pack, tiles_per_seq, ctx_tiles):
    n = h.shape[0]
    grid = (n // TOKEN_TILE,)
    tiles = [pltpu.VMEM((ROW_WORDS * TILE_STRIDE, LANES), jnp.int32)] * PEER_GROUP
    params = pltpu.CompilerParams(vmem_limit_bytes=PEER_VMEM_BYTES)
    idx_spec = pl.BlockSpec((TOKEN_TILE, PEER_SEL), lambda i: (i, 0), memory_space=pltpu.SMEM)
    sel_spec = _tok_spec(PEER_SEL)
    act = pl.pallas_call(
        _peer_u_body, grid=grid,
        in_specs=[idx_spec, _tok_spec(D_MODEL), sel_spec, _RESIDENT],
        out_specs=sel_spec,
        out_shape=jax.ShapeDtypeStruct((n, PEER_SEL), F32),
        scratch_shapes=tiles, compiler_params=params, name="peer_u",
    )(idx_words, h, gate, u_pack)
    return pl.pallas_call(
        _peer_v_body, grid=grid,
        in_specs=[idx_spec, sel_spec, _tok_spec(D_MODEL), _mod_spec(tiles_per_seq, ctx_tiles), _RESIDENT],
        out_specs=_tok_spec(D_MODEL),
        out_shape=jax.ShapeDtypeStruct((n, D_MODEL), F32),
        scratch_shapes=tiles, compiler_params=params, name="peer_v",
    )(idx_words, act, x, mod_gate, v_pack)


def peer_ffn(x, mods, norm_w, wq, keys, u_tab, v_tab, tiles_per_seq, ctx_tiles):
    h, idx_words, gate = peer_route(x, mods[3], mods[4], norm_w, wq, keys, tiles_per_seq, ctx_tiles)
    return peer_experts(x, h, idx_words, gate, mods[5], pack_table(u_tab), pack_table(v_tab), tiles_per_seq, ctx_tiles)


def _final_norm_body(x_ref, w_ref, o_ref):
    xf = x_ref[...]
    o_ref[...] = xf * lax.rsqrt(jnp.mean(xf * xf, axis=-1, keepdims=True) + NORM_EPS) * w_ref[...]


def final_rmsnorm(x, w):
    n = x.shape[0]
    return pl.pallas_call(
        _final_norm_body, grid=(n // TOKEN_TILE,),
        in_specs=[_tok_spec(D_MODEL), _row_spec(D_MODEL)],
        out_specs=_tok_spec(D_MODEL),
        out_shape=jax.ShapeDtypeStruct((n, D_MODEL), F32),
        name="final_rmsnorm",
    )(x, w.reshape(1, D_MODEL))


def even_layer(stream, mods, n_seq, tiles_per_seq, ctx_tiles, norm_w, w_in, w_out, sc_conv_w, ssd_conv_w,
               ssd_conv_b, dt_bias, a_log, d_skip, ssd_norm_w):
    ya, z, xs, bm, cm, dt_raw = even_in(stream, mods[0], mods[1], norm_w, w_in, sc_conv_w, ssd_conv_w, ssd_conv_b,
                                        tiles_per_seq, ctx_tiles)
    ys = ssd_scan(xs, bm, cm, dt_raw, dt_bias, -jnp.exp(a_log), n_seq, ctx_tiles * TOKEN_TILE // SSD_CHUNK)
    return even_out(stream, mods[2], ya, z, xs, ys, d_skip, ssd_norm_w, w_out, tiles_per_seq, ctx_tiles)


def odd_layer(stream, mods, n_seq, tiles_per_seq, ctx_tiles, norm_w, w_in, w_out, lower_bound, hg_norm_w):
    q, f_raw, v, g = odd_in(stream, mods[0], mods[1], norm_w, w_in, tiles_per_seq, ctx_tiles)
    os_ = gla_scan(q, f_raw, v, lower_bound.reshape(2, 1, HG_WIDTH), n_seq, ctx_tiles * TOKEN_TILE // GLA_CHUNK)
    return odd_out(stream, mods[2], os_, g, hg_norm_w, w_out, tiles_per_seq, ctx_tiles)


def kernel(x, c, ctx, c_ctx, ada_w, ada_b, norm_mix_w, norm_ffn_w, norm_f_w, ev_w_in, ev_w_out,
           sc_conv_w, ssd_conv_w, ssd_conv_b, ssd_dt_bias, ssd_a_log, ssd_d, ssd_norm_w,
           od_w_in, od_w_out, hg_lb_logits, hg_norm_w, peer_wq, peer_keys, peer_u, peer_v):
    Bsz, seq, D = x.shape
    ctx_len = ctx.shape[1]
    assert D == D_MODEL and ctx_len == TOKEN_TILE and seq % TOKEN_TILE == 0 and TOKEN_TILE % GRID_W == 0
    lb_sm = jax.nn.softmax(hg_lb_logits.astype(F32), axis=0)
    lower_bounds = jnp.cumsum(lb_sm, axis=0) - lb_sm[0]
    cond = jnp.concatenate([c, c_ctx[None, :], jnp.zeros((7 - Bsz % 8, D), F32)], axis=0)

    ctx_tiles = ctx_len // TOKEN_TILE
    tiles_per_seq = ctx_tiles + seq // TOKEN_TILE
    stream = jnp.concatenate([ctx, x], axis=1).reshape(Bsz * (ctx_len + seq), D)
    for layer in range(DEPTH):
        last = layer == DEPTH - 1
        j = layer // 2
        m = ada_modulation(cond, ada_w[layer], ada_b[layer]).reshape(-1, 6, D)
        mods = [jnp.stack([jnp.broadcast_to(m[Bsz, k], (Bsz, D)), m[:Bsz, k]], axis=1)[:, :, None, :] for k in range(6)]
        if layer % 2 == 0:
            stream = even_layer(stream, mods, Bsz, tiles_per_seq, ctx_tiles, norm_mix_w[layer], ev_w_in[j], ev_w_out[j],
                                sc_conv_w[j], ssd_conv_w[j], ssd_conv_b[j], ssd_dt_bias[j], ssd_a_log[j], ssd_d[j],
                                ssd_norm_w[j])
        else:
            stream = odd_layer(stream, mods, Bsz, tiles_per_seq, ctx_tiles, norm_mix_w[layer], od_w_in[j], od_w_out[j],
                               lower_bounds[layer], hg_norm_w[j])
        if last:
            stream = stream.reshape(Bsz, ctx_len + seq, D)[:, ctx_len:].reshape(Bsz * seq, D)
            tiles_per_seq, ctx_tiles = seq // TOKEN_TILE, 0
        stream = peer_ffn(stream, mods, norm_ffn_w[layer], peer_wq[layer], peer_keys[layer], peer_u[layer],
                          peer_v[layer], tiles_per_seq, ctx_tiles)
    return final_rmsnorm(stream, norm_f_w).reshape(Bsz, seq, D)
```

```python
import functools

import jax
import jax.numpy as jnp
from jax import lax
from jax.experimental import pallas as pl
from jax.experimental.pallas import tpu as pltpu

D_MODEL = 1024
DEPTH = 2
GRID_W = 64
CTX_LEN = 256
NORM_EPS = 1e-6
CONV_W = 3
SC_WIDTH = 1024
SSD_HEADS = 16
SSD_HEAD_DIM = 64
SSD_WIDTH = SSD_HEADS * SSD_HEAD_DIM
SSD_GROUPS = 4
SSD_STATE = 128
SSD_BC = SSD_GROUPS * SSD_STATE
SSD_CONV_DIM = SSD_WIDTH + 2 * SSD_BC
EVEN_MIX = SC_WIDTH + SSD_WIDTH
HG_HEADS = 8
HG_KDIM = 128
HG_VDIM = 128
HG_WIDTH = HG_HEADS * HG_KDIM
PEER_HEADS = 8
PEER_NKEYS = 128
PEER_QDIM = 256
PEER_HALF = PEER_QDIM // 2
PEER_TOPK = 16
PEER_SEL = PEER_HEADS * PEER_TOPK
PEER_QW = PEER_HEADS * PEER_QDIM

LANES = 128
TOKEN_TILE = 256
SSD_CHUNK = 128
GLA_CHUNK = 64
ROW_WORDS = D_MODEL // (2 * LANES)
TILE_STRIDE = PEER_SEL + 8
PEER_GROUP = 8
MIB = 1024 * 1024
PEER_VMEM_BYTES = 48 * MIB
MIX_VMEM_BYTES = 48 * MIB
HI = lax.Precision.HIGHEST
BF = jnp.bfloat16
F32 = jnp.float32


def _mod_spec(tiles_per_seq, ctx_tiles):
    return pl.BlockSpec((1, 1, 1, D_MODEL),
                        lambda i: (i // tiles_per_seq, jnp.where(i % tiles_per_seq < ctx_tiles, 0, 1), 0, 0))


def _tok_spec(width):
    return pl.BlockSpec((TOKEN_TILE, width), lambda i: (i, 0))


def _row_spec(width):
    return pl.BlockSpec((1, width), lambda i: (0, 0))


_RESIDENT = pl.BlockSpec(memory_space=pltpu.VMEM)


def _norm_mod(x, nw, shift, scale):
    y = x * lax.rsqrt(jnp.mean(x * x, axis=-1, keepdims=True) + NORM_EPS) * nw
    return y * (1.0 + scale) + shift


def _silu(x):
    return x * jax.nn.sigmoid(x)


def _ada_body(c_ref, w_ref, b_ref, o_ref):
    cond = _silu(c_ref[...]).astype(BF)
    o_ref[...] = jnp.dot(cond, w_ref[...].astype(BF), preferred_element_type=F32) + b_ref[...]


def ada_modulation(cond, w, b):
    rows = cond.shape[0]
    return pl.pallas_call(
        _ada_body, grid=(6,),
        in_specs=[pl.BlockSpec((rows, D_MODEL), lambda j: (0, 0)),
                  pl.BlockSpec((D_MODEL, D_MODEL), lambda j: (0, j)),
                  pl.BlockSpec((1, D_MODEL), lambda j: (0, j))],
        out_specs=pl.BlockSpec((rows, D_MODEL), lambda j: (0, j)),
        out_shape=jax.ShapeDtypeStruct((rows, 6 * D_MODEL), F32),
        name="ada_modulation",
    )(cond, w, b.reshape(1, 6 * D_MODEL))


EVEN_W_COLS = 3 * SC_WIDTH + SSD_WIDTH + SSD_CONV_DIM + 2 * LANES


def _even_in_body(x_ref, shift_ref, scale_ref, nw_ref, w_ref, cw_sc_ref, cw_ssd_ref, cb_ssd_ref,
                  ya_ref, z_ref, xs_ref, bm_ref, cm_ref, dt_ref, *, tiles_per_seq, ctx_tiles):
    T = TOKEN_TILE
    h = _norm_mod(x_ref[...], nw_ref[...], shift_ref[0, 0], scale_ref[0, 0]).astype(BF)
    is_ctx = (pl.program_id(0) % tiles_per_seq) < ctx_tiles
    row_len = jnp.where(is_ctx, T, GRID_W)
    pos = lax.broadcasted_iota(jnp.int32, (T, 1), 0) & (row_len - 1)
    first, last = pos == 0, pos == row_len - 1

    def conv(u, cw_ref):
        prev = jnp.where(first, 0.0, pltpu.roll(u, 1, 0))
        nxt = jnp.where(last, 0.0, pltpu.roll(u, T - 1, 0))
        return prev * cw_ref[0:1, :] + u * cw_ref[1:2, :] + nxt * cw_ref[2:3, :]

    def proj(lo, width):
        return jnp.dot(h, w_ref[:, lo:lo + width], preferred_element_type=F32)

    sc_x, sc_b, sc_c = proj(0, SC_WIDTH), proj(SC_WIDTH, SC_WIDTH), proj(2 * SC_WIDTH, SC_WIDTH)
    ya_ref[...] = sc_b * conv(sc_c * sc_x, cw_sc_ref)
    z_ref[...] = proj(3 * SC_WIDTH, SSD_WIDTH)
    xbc = _silu(conv(proj(3 * SC_WIDTH + SSD_WIDTH, SSD_CONV_DIM), cw_ssd_ref) + cb_ssd_ref[...])
    xs_ref[...] = xbc[:, :SSD_WIDTH]
    bm_ref[...] = xbc[:, SSD_WIDTH:SSD_WIDTH + SSD_BC]
    cm_ref[...] = xbc[:, SSD_WIDTH + SSD_BC:]
    dt_ref[...] = proj(3 * SC_WIDTH + SSD_WIDTH + SSD_CONV_DIM, 2 * LANES)


def even_in(x, mod_shift, mod_scale, norm_w, w_in, sc_conv_w, ssd_conv_w, ssd_conv_b, tiles_per_seq, ctx_tiles):
    n = x.shape[0]
    main = 3 * SC_WIDTH + SSD_WIDTH + SSD_CONV_DIM
    w_dt = jnp.zeros((D_MODEL, 2, LANES), F32).at[:, :, :SSD_HEADS].set(w_in[:, main:].reshape(D_MODEL, 2, SSD_HEADS))
    w = jnp.concatenate([w_in[:, :main], w_dt.reshape(D_MODEL, 2 * LANES)], axis=1).astype(BF)
    mod = _mod_spec(tiles_per_seq, ctx_tiles)
    widths = (SC_WIDTH, SSD_WIDTH, SSD_WIDTH, SSD_BC, SSD_BC, 2 * LANES)
    return pl.pallas_call(
        functools.partial(_even_in_body, tiles_per_seq=tiles_per_seq, ctx_tiles=ctx_tiles),
        grid=(n // TOKEN_TILE,),
        in_specs=[_tok_spec(D_MODEL), mod, mod, _row_spec(D_MODEL), _RESIDENT,
                  pl.BlockSpec((CONV_W, SC_WIDTH), lambda i: (0, 0)),
                  pl.BlockSpec((CONV_W, SSD_CONV_DIM), lambda i: (0, 0)), _row_spec(SSD_CONV_DIM)],
        out_specs=[_tok_spec(wd) for wd in widths],
        out_shape=[jax.ShapeDtypeStruct((n, wd), F32) for wd in widths],
        compiler_params=pltpu.CompilerParams(vmem_limit_bytes=MIX_VMEM_BYTES),
        name="even_in",
    )(x, mod_shift, mod_scale, norm_w.reshape(1, D_MODEL), w, sc_conv_w, ssd_conv_w,
      ssd_conv_b.reshape(1, SSD_CONV_DIM))


def _scan_row(n_chunks, ctx_chunks):
    def row(b, d, c):
        back = jnp.where(c < ctx_chunks, ctx_chunks - 1 - c, n_chunks - 1 + ctx_chunks - c)
        return b * n_chunks + jnp.where(d == 0, c, back)
    return row


def _causal_mask(L, d):
    r = lax.broadcasted_iota(jnp.int32, (L, L), 0)
    c = lax.broadcasted_iota(jnp.int32, (L, L), 1)
    return jnp.where(d == 0, c - r, r - c) <= 0


SSD_GW = (SSD_HEADS // SSD_GROUPS) * SSD_HEAD_DIM


def _ssd_body(xs_ref, bm_ref, cm_ref, dt_ref, bias_ref, a_ref, y_ref, state):
    d = pl.program_id(1)
    L = SSD_CHUNK

    @pl.when(pl.program_id(2) == 0)
    def _():
        state[...] = jnp.zeros_like(state)

    mask = _causal_mask(L, d)
    v = dt_ref[...] + bias_ref[0]
    dt = jnp.maximum(v, 0.0) + jnp.log1p(jnp.exp(-jnp.abs(v)))
    la = jnp.dot(mask.astype(F32), dt * a_ref[0], precision=HI, preferred_element_type=F32)
    la_t = la.T
    head_of_lane = lax.broadcasted_iota(jnp.int32, (LANES, SSD_WIDTH), 1) // SSD_HEAD_DIM
    expand = (lax.broadcasted_iota(jnp.int32, (LANES, SSD_WIDTH), 0) == head_of_lane).astype(F32)
    dt_x = jnp.dot(dt, expand, precision=HI, preferred_element_type=F32)
    la_x = jnp.dot(la, expand, precision=HI, preferred_element_type=F32)
    end_x = jnp.where(d == 0, la_x[L - 1:L], la_x[0:1])
    xdt = xs_ref[...] * dt_x
    w_state = (xdt * jnp.exp(end_x - la_x)).astype(BF)
    xdt_b = xdt.astype(BF)
    e_la = jnp.exp(la_x)
    dec = jnp.exp(end_x)
    for g in range(SSD_GROUPS):
        bm = bm_ref[:, g * SSD_STATE:(g + 1) * SSD_STATE].astype(BF)
        cm = cm_ref[:, g * SSD_STATE:(g + 1) * SSD_STATE].astype(BF)
        cb = lax.dot_general(cm, bm, (((1,), (1,)), ((), ())), preferred_element_type=F32)
        lanes = slice(g * SSD_GW, (g + 1) * SSD_GW)
        s_old = state[g]
        y_in = jnp.dot(cm, s_old.astype(BF), preferred_element_type=F32) * e_la[:, lanes]
        ys = []
        for j in range(SSD_HEADS // SSD_GROUPS):
            h = g * (SSD_HEADS // SSD_GROUPS) + j
            seg = la[:, h:h + 1] - la_t[h:h + 1, :]
            m = (cb * jnp.exp(jnp.where(mask, seg, -jnp.inf))).astype(BF)
            ys.append(jnp.dot(m, xdt_b[:, h * SSD_HEAD_DIM:(h + 1) * SSD_HEAD_DIM], preferred_element_type=F32))
        y_ref[0, :, lanes] = y_in + jnp.concatenate(ys, axis=1)
        state[g] = dec[:, lanes] * s_old + jnp.dot(bm.T, w_state[:, lanes], preferred_element_type=F32)


def ssd_scan(xs, bm, cm, dt_raw, dt_bias, a, n_seq, ctx_chunks):
    n = xs.shape[0]
    n_chunks = n // n_seq // SSD_CHUNK
    row = _scan_row(n_chunks, ctx_chunks)
    pad = lambda p: jnp.zeros((2, 1, LANES), F32).at[:, 0, :SSD_HEADS].set(p)
    tok = lambda width: pl.BlockSpec((SSD_CHUNK, width), lambda b, d, c: (row(b, d, c), 0))
    per_dir = pl.BlockSpec((1, 1, LANES), lambda b, d, c: (d, 0, 0))
    return pl.pallas_call(
        _ssd_body, grid=(n_seq, 2, n_chunks),
        in_specs=[tok(SSD_WIDTH), tok(SSD_BC), tok(SSD_BC),
                  pl.BlockSpec((SSD_CHUNK, LANES), lambda b, d, c: (row(b, d, c), d)), per_dir, per_dir],
        out_specs=pl.BlockSpec((1, SSD_CHUNK, SSD_WIDTH), lambda b, d, c: (d, row(b, d, c), 0)),
        out_shape=jax.ShapeDtypeStruct((2, n, SSD_WIDTH), F32),
        scratch_shapes=[pltpu.VMEM((SSD_GROUPS, SSD_STATE, SSD_GW), F32)],
        compiler_params=pltpu.CompilerParams(dimension_semantics=("arbitrary", "arbitrary", "arbitrary")),
        name="ssd_scan",
    )(xs, bm, cm, dt_raw, pad(dt_bias), pad(a))


def _gla_body(q_ref, f_ref, v_ref, lb_ref, o_ref, state):
    d = pl.program_id(1)
    L = GLA_CHUNK

    @pl.when(pl.program_id(2) == 0)
    def _():
        state[...] = jnp.zeros_like(state)

    mask = _causal_mask(L, d)
    lb = lb_ref[0]
    f = lb + (1.0 - lb) * jax.nn.sigmoid(f_ref[0])
    gc = jnp.dot(mask.astype(F32), jnp.log(f), precision=HI, preferred_element_type=F32)
    g_end = jnp.where(d == 0, gc[L - 1:L], gc[0:1])
    k = 1.0 - f
    qg = (q_ref[...] * jnp.exp(gc)).astype(BF)
    kg = (k * jnp.exp(-gc)).astype(BF)
    k_end = (k * jnp.exp(g_end - gc)).astype(BF)
    dec = jnp.exp(g_end)
    vb = v_ref[...].astype(BF)
    for h in range(HG_HEADS):
        lanes = slice(h * HG_KDIM, (h + 1) * HG_KDIM)
        att = lax.dot_general(qg[:, lanes], kg[:, lanes], (((1,), (1,)), ((), ())), preferred_element_type=F32)
        att = jnp.where(mask, att, 0.0).astype(BF)
        s_old = state[h]
        o = jnp.dot(att, vb[:, lanes], preferred_element_type=F32)
        o_ref[0, :, lanes] = o + lax.dot_general(qg[:, lanes], s_old.astype(BF), (((1,), (1,)), ((), ())),
                                                 preferred_element_type=F32)
        state[h] = s_old * dec[:, lanes] + jnp.dot(vb[:, lanes].T, k_end[:, lanes], preferred_element_type=F32)


def gla_scan(q, f_raw, v, lb, n_seq, ctx_chunks):
    n = q.shape[0]
    n_chunks = n // n_seq // GLA_CHUNK
    row = _scan_row(n_chunks, ctx_chunks)
    tok = pl.BlockSpec((GLA_CHUNK, HG_WIDTH), lambda b, d, c: (row(b, d, c), 0))
    dtok = pl.BlockSpec((1, GLA_CHUNK, HG_WIDTH), lambda b, d, c: (d, row(b, d, c), 0))
    return pl.pallas_call(
        _gla_body, grid=(n_seq, 2, n_chunks),
        in_specs=[tok, dtok, tok, pl.BlockSpec((1, 1, HG_WIDTH), lambda b, d, c: (d, 0, 0))],
        out_specs=dtok,
        out_shape=jax.ShapeDtypeStruct((2, n, HG_WIDTH), F32),
        scratch_shapes=[pltpu.VMEM((HG_HEADS, HG_VDIM, HG_KDIM), F32)],
        compiler_params=pltpu.CompilerParams(dimension_semantics=("arbitrary", "arbitrary", "arbitrary")),
        name="gla_scan",
    )(q, f_raw, v, lb)


def _group_rmsnorm(y, groups):
    width = y.shape[1] // groups
    parts = []
    for g in range(groups):
        seg = y[:, g * width:(g + 1) * width]
        parts.append(seg * lax.rsqrt(jnp.mean(seg * seg, axis=-1, keepdims=True) + NORM_EPS))
    return jnp.concatenate(parts, axis=1)


def _even_out_body(x_ref, gate_ref, ya_ref, z_ref, xs_ref, y0_ref, y1_ref, dskip_ref, nw_ref, w_ref, o_ref):
    yb = (y0_ref[0] + y1_ref[0] + dskip_ref[...] * xs_ref[...]) * _silu(z_ref[...])
    yb = _group_rmsnorm(yb, SSD_GROUPS) * nw_ref[...]
    mix = jnp.concatenate([ya_ref[...], yb], axis=1).astype(BF)
    o_ref[...] = x_ref[...] + gate_ref[0, 0] * jnp.dot(mix, w_ref[...], preferred_element_type=F32)


def even_out(x, mod_gate, ya, z, xs, ys, d_skip, norm_w, w_out, tiles_per_seq, ctx_tiles):
    n = x.shape[0]
    ydir = lambda d: pl.BlockSpec((1, TOKEN_TILE, SSD_WIDTH), lambda i: (d, i, 0))
    return pl.pallas_call(
        _even_out_body, grid=(n // TOKEN_TILE,),
        in_specs=[_tok_spec(D_MODEL), _mod_spec(tiles_per_seq, ctx_tiles), _tok_spec(SC_WIDTH), _tok_spec(SSD_WIDTH),
                  _tok_spec(SSD_WIDTH), ydir(0), ydir(1), _row_spec(SSD_WIDTH), _row_spec(SSD_WIDTH), _RESIDENT],
        out_specs=_tok_spec(D_MODEL),
        out_shape=jax.ShapeDtypeStruct((n, D_MODEL), F32),
        compiler_params=pltpu.CompilerParams(vmem_limit_bytes=MIX_VMEM_BYTES),
        name="even_out",
    )(x, mod_gate, ya, z, xs, ys, ys, jnp.repeat(d_skip, SSD_HEAD_DIM).reshape(1, SSD_WIDTH),
      norm_w.reshape(1, SSD_WIDTH), w_out.astype(BF))


def _odd_in_body(x_ref, shift_ref, scale_ref, nw_ref, w_ref, q_ref, f_ref, v_ref, g_ref):
    h = _norm_mod(x_ref[...], nw_ref[...], shift_ref[0, 0], scale_ref[0, 0]).astype(BF)
    proj = lambda j: jnp.dot(h, w_ref[:, j * HG_WIDTH:(j + 1) * HG_WIDTH], preferred_element_type=F32)
    q_ref[...] = _silu(proj(0)) * (HG_KDIM ** -0.5)
    f_ref[0] = proj(1)
    f_ref[1] = proj(2)
    v_ref[...] = proj(3)
    g_ref[...] = proj(4)


def odd_in(x, mod_shift, mod_scale, norm_w, w_in, tiles_per_seq, ctx_tiles):
    n = x.shape[0]
    mod = _mod_spec(tiles_per_seq, ctx_tiles)
    tok = _tok_spec(HG_WIDTH)
    sds = jax.ShapeDtypeStruct((n, HG_WIDTH), F32)
    return pl.pallas_call(
        _odd_in_body, grid=(n // TOKEN_TILE,),
        in_specs=[_tok_spec(D_MODEL), mod, mod, _row_spec(D_MODEL), _RESIDENT],
        out_specs=[tok, pl.BlockSpec((2, TOKEN_TILE, HG_WIDTH), lambda i: (0, i, 0)), tok, tok],
        out_shape=[sds, jax.ShapeDtypeStruct((2, n, HG_WIDTH), F32), sds, sds],
        compiler_params=pltpu.CompilerParams(vmem_limit_bytes=MIX_VMEM_BYTES),
        name="odd_in",
    )(x, mod_shift, mod_scale, norm_w.reshape(1, D_MODEL), w_in.astype(BF))


def _odd_out_body(x_ref, gate_ref, o0_ref, o1_ref, g_ref, nw_ref, w_ref, out_ref):
    o = _group_rmsnorm(o0_ref[0] + o1_ref[0], HG_HEADS) * nw_ref[...] * _silu(g_ref[...])
    out_ref[...] = x_ref[...] + gate_ref[0, 0] * jnp.dot(o.astype(BF), w_ref[...], preferred_element_type=F32)


def odd_out(x, mod_gate, os_, g, norm_w, w_out, tiles_per_seq, ctx_tiles):
    n = x.shape[0]
    odir = lambda d: pl.BlockSpec((1, TOKEN_TILE, HG_WIDTH), lambda i: (d, i, 0))
    return pl.pallas_call(
        _odd_out_body, grid=(n // TOKEN_TILE,),
        in_specs=[_tok_spec(D_MODEL), _mod_spec(tiles_per_seq, ctx_tiles), odir(0), odir(1), _tok_spec(HG_WIDTH),
                  _row_spec(HG_WIDTH), _RESIDENT],
        out_specs=_tok_spec(D_MODEL),
        out_shape=jax.ShapeDtypeStruct((n, D_MODEL), F32),
        compiler_params=pltpu.CompilerParams(vmem_limit_bytes=MIX_VMEM_BYTES),
        name="odd_out",
    )(x, mod_gate, os_, os_, g, norm_w.reshape(1, HG_WIDTH), w_out.astype(BF))


def _top16(vals, n_rows):
    rows = lax.broadcasted_iota(jnp.int32, vals.shape, 0)
    top_s, top_i = [], []
    for _ in range(PEER_TOPK):
        m = jnp.max(vals, axis=0, keepdims=True)
        pos = jnp.min(jnp.where(vals == m, rows, n_rows), axis=0, keepdims=True)
        vals = jnp.where(rows == pos, -jnp.inf, vals)
        top_s.append(m)
        top_i.append(pos)
    return jnp.concatenate(top_s, axis=0), jnp.concatenate(top_i, axis=0)


def _pick_row(table, row):
    r = lax.broadcasted_iota(jnp.int32, table.shape, 0)
    return jnp.sum(jnp.where(r == row, table, 0), axis=0, keepdims=True)


def _route_body(x_ref, shift_ref, scale_ref, nw_ref, wq_ref, keys_ref, h_ref, idx_ref, gate_ref,
                q_scr, idx_scr, gate_scr):
    h = _norm_mod(x_ref[...], nw_ref[...], shift_ref[0, 0], scale_ref[0, 0])
    h_ref[...] = h
    q_scr[...] = jnp.dot(h.astype(BF), wq_ref[...], preferred_element_type=F32)

    def head(it, carry):
        tok0 = pl.multiple_of((it // PEER_HEADS) * LANES, LANES)
        hd = it % PEER_HEADS
        tops = []
        for a in range(2):
            col = pl.multiple_of(hd * PEER_QDIM + a * PEER_HALF, PEER_HALF)
            qs = q_scr[pl.ds(tok0, LANES), pl.ds(col, PEER_HALF)].astype(BF)
            sc = lax.dot_general(keys_ref[hd * 2 + a], qs, (((1,), (1,)), ((), ())),
                                 preferred_element_type=F32)
            tops.append(_top16(sc, PEER_NKEYS))
        (s0, i0), (s1, i1) = tops
        cand = jnp.concatenate([s0[a:a + 1] + s1 for a in range(PEER_TOPK)], axis=0)
        best, pos = _top16(cand, PEER_TOPK * PEER_TOPK)
        ids = jnp.concatenate(
            [_pick_row(i0, pos[r:r + 1] >> 4) * PEER_NKEYS + _pick_row(i1, pos[r:r + 1] & (PEER_TOPK - 1))
             for r in range(PEER_TOPK)], axis=0)
        e = jnp.exp(best - best[0:1])
        r0 = pl.multiple_of(hd * PEER_TOPK, PEER_TOPK)
        idx_scr[pl.ds(r0, PEER_TOPK), pl.ds(tok0, LANES)] = ids * ROW_WORDS
        gate_scr[pl.ds(r0, PEER_TOPK), pl.ds(tok0, LANES)] = e / jnp.sum(e, axis=0, keepdims=True)
        return carry

    lax.fori_loop(0, (TOKEN_TILE // LANES) * PEER_HEADS, head, 0)
    idx_ref[...] = idx_scr[...].T
    gate_ref[...] = gate_scr[...].T


def peer_route(x, mod_shift, mod_scale, norm_w, wq, keys, tiles_per_seq, ctx_tiles):
    n = x.shape[0]
    mod = _mod_spec(tiles_per_seq, ctx_tiles)
    sel = _tok_spec(PEER_SEL)
    return pl.pallas_call(
        _route_body, grid=(n // TOKEN_TILE,),
        in_specs=[_tok_spec(D_MODEL), mod, mod, _row_spec(D_MODEL),
                  pl.BlockSpec((D_MODEL, PEER_QW), lambda i: (0, 0)),
                  pl.BlockSpec((PEER_HEADS * 2, PEER_NKEYS, PEER_HALF), lambda i: (0, 0, 0))],
        out_specs=[_tok_spec(D_MODEL), sel, sel],
        out_shape=[jax.ShapeDtypeStruct((n, D_MODEL), F32),
                   jax.ShapeDtypeStruct((n, PEER_SEL), jnp.int32),
                   jax.ShapeDtypeStruct((n, PEER_SEL), F32)],
        scratch_shapes=[pltpu.VMEM((TOKEN_TILE, PEER_QW), F32),
                        pltpu.VMEM((PEER_SEL, TOKEN_TILE), jnp.int32),
                        pltpu.VMEM((PEER_SEL, TOKEN_TILE), F32)],
        compiler_params=pltpu.CompilerParams(vmem_limit_bytes=MIX_VMEM_BYTES),
        name="peer_route",
    )(x, mod_shift, mod_scale, norm_w.reshape(1, D_MODEL), wq.astype(BF),
      keys.reshape(PEER_HEADS * 2, PEER_NKEYS, PEER_HALF).astype(BF))


def pack_table(tab):
    E = tab.shape[0]
    bits = lax.bitcast_convert_type(tab.astype(BF), jnp.uint16).astype(jnp.uint32)
    words = bits[:, :D_MODEL // 2] | (bits[:, D_MODEL // 2:] << 16)
    return lax.bitcast_convert_type(words, jnp.int32).reshape(E * ROW_WORDS, LANES)


def _gather_group(idx_ref, tab_ref, tile_refs, tokens):
    rows = [idx_ref.at[t] for t in tokens]
    for k in range(PEER_SEL):
        for row, tile_ref in zip(rows, tile_refs):
            e = pl.multiple_of(row[k], ROW_WORDS)
            tile_ref[pl.ds(k, ROW_WORDS, stride=TILE_STRIDE), :] = tab_ref[pl.ds(e, ROW_WORDS), :]


def _unpack_rows(tile_ref):
    lo, hi = [], []
    for j in range(ROW_WORDS):
        w = tile_ref[j * TILE_STRIDE:j * TILE_STRIDE + PEER_SEL, :]
        lo.append(pltpu.bitcast(w << 16, F32))
        hi.append(pltpu.bitcast(w & jnp.int32(-65536), F32))
    return jnp.concatenate(lo + hi, axis=1).astype(BF)


def _for_each_token(idx_ref, tab_ref, tiles, compute):
    set_a, set_b = tiles[:PEER_GROUP], tiles[PEER_GROUP:]
    _gather_group(idx_ref, tab_ref, set_a, list(range(PEER_GROUP)))

    def step(s, carry):
        base = s * 2 * PEER_GROUP
        _gather_group(idx_ref, tab_ref, set_b, [base + PEER_GROUP + i for i in range(PEER_GROUP)])
        for i in range(PEER_GROUP):
            compute(set_a[i], base + i)
        for i in range(PEER_GROUP):
            compute(set_b[i], base + PEER_GROUP + i)
        _gather_group(idx_ref, tab_ref, set_a,
                      [jnp.minimum(base + 2 * PEER_GROUP + i, TOKEN_TILE - 1) for i in range(PEER_GROUP)])
        return carry

    lax.fori_loop(0, TOKEN_TILE // (2 * PEER_GROUP), step, 0)


def _peer_u_body(idx_ref, h_ref, gate_ref, tab_ref, act_ref, *tiles):
    def compute(tile, t):
        hb = jnp.broadcast_to(h_ref[pl.ds(t, 1), :], (8, D_MODEL)).astype(BF)
        s = lax.dot_general(hb, _unpack_rows(tile), (((1,), (1,)), ((), ())), preferred_element_type=F32)
        act_ref[pl.ds(t, 1), :] = s[0:1]

    _for_each_token(idx_ref, tab_ref, tiles, compute)
    d = act_ref[...]
    act_ref[...] = 0.5 * d * (1.0 + lax.erf(d * (2.0 ** -0.5))) * gate_ref[...]


def _peer_v_body(idx_ref, act_ref, x_ref, gate_ref, tab_ref, out_ref, *tiles):
    def compute(tile, t):
        ab = jnp.broadcast_to(act_ref[pl.ds(t, 1), :], (8, PEER_SEL)).astype(BF)
        o = jnp.dot(ab, _unpack_rows(tile), preferred_element_type=F32)
        out_ref[pl.ds(t, 1), :] = o[0:1]

    _for_each_token(idx_ref, tab_ref, tiles, compute)
    out_ref[...] = x_ref[...] + gate_ref[0, 0] * out_ref[...]


def peer_experts(x, h, idx_words, gate, mod_gate, u_pack, v_pack, tiles_per_seq, ctx_tiles):
    n = h.shape[0]
    grid = (n // TOKEN_TILE,)
    tiles = [pltpu.VMEM((ROW_WORDS * TILE_STRIDE, LANES), jnp.int32)] * (2 * PEER_GROUP)
    params = pltpu.CompilerParams(vmem_limit_bytes=PEER_VMEM_BYTES)
    idx_spec = pl.BlockSpec((TOKEN_TILE, PEER_SEL), lambda i: (i, 0), memory_space=pltpu.SMEM)
    sel_spec = _tok_spec(PEER_SEL)
    act = pl.pallas_call(
        _peer_u_body, grid=grid,
        in_specs=[idx_spec, _tok_spec(D_MODEL), sel_spec, _RESIDENT],
        out_specs=sel_spec,
        out_shape=jax.ShapeDtypeStruct((n, PEER_SEL), F32),
        scratch_shapes=tiles, compiler_params=params, name="peer_u",
    )(idx_words, h, gate, u_pack)
    return pl.pallas_call(
        _peer_v_body, grid=grid,
        in_specs=[idx_spec, sel_spec, _tok_spec(D_MODEL), _mod_spec(tiles_per_seq, ctx_tiles), _RESIDENT],
        out_specs=_tok_spec(D_MODEL),
        out_shape=jax.ShapeDtypeStruct((n, D_MODEL), F32),
        scratch_shapes=tiles, compiler_params=params, name="peer_v",
    )(idx_words, act, x, mod_gate, v_pack)


def peer_ffn(x, mods, norm_w, wq, keys, u_tab, v_tab, tiles_per_seq, ctx_tiles):
    h, idx_words, gate = peer_route(x, mods[3], mods[4], norm_w, wq, keys, tiles_per_seq, ctx_tiles)
    return peer_experts(x, h, idx_words, gate, mods[5], pack_table(u_tab), pack_table(v_tab), tiles_per_seq, ctx_tiles)


def _final_norm_body(x_ref, w_ref, o_ref):
    xf = x_ref[...]
    o_ref[...] = xf * lax.rsqrt(jnp.mean(xf * xf, axis=-1, keepdims=True) + NORM_EPS) * w_ref[...]


def final_rmsnorm(x, w):
    n = x.shape[0]
    return pl.pallas_call(
        _final_norm_body, grid=(n // TOKEN_TILE,),
        in_specs=[_tok_spec(D_MODEL), _row_spec(D_MODEL)],
        out_specs=_tok_spec(D_MODEL),
        out_shape=jax.ShapeDtypeStruct((n, D_MODEL), F32),
        name="final_rmsnorm",
    )(x, w.reshape(1, D_MODEL))


def even_layer(stream, mods, n_seq, tiles_per_seq, ctx_tiles, norm_w, w_in, w_out, sc_conv_w, ssd_conv_w,
               ssd_conv_b, dt_bias, a_log, d_skip, ssd_norm_w):
    ya, z, xs, bm, cm, dt_raw = even_in(stream, mods[0], mods[1], norm_w, w_in, sc_conv_w, ssd_conv_w, ssd_conv_b,
                                        tiles_per_seq, ctx_tiles)
    ys = ssd_scan(xs, bm, cm, dt_raw, dt_bias, -jnp.exp(a_log), n_seq, ctx_tiles * TOKEN_TILE // SSD_CHUNK)
    return even_out(stream, mods[2], ya, z, xs, ys, d_skip, ssd_norm_w, w_out, tiles_per_seq, ctx_tiles)


def odd_layer(stream, mods, n_seq, tiles_per_seq, ctx_tiles, norm_w, w_in, w_out, lower_bound, hg_norm_w):
    q, f_raw, v, g = odd_in(stream, mods[0], mods[1], norm_w, w_in, tiles_per_seq, ctx_tiles)
    os_ = gla_scan(q, f_raw, v, lower_bound.reshape(2, 1, HG_WIDTH), n_seq, ctx_tiles * TOKEN_TILE // GLA_CHUNK)
    return odd_out(stream, mods[2], os_, g, hg_norm_w, w_out, tiles_per_seq, ctx_tiles)


def kernel(x, c, ctx, c_ctx, ada_w, ada_b, norm_mix_w, norm_ffn_w, norm_f_w, ev_w_in, ev_w_out,
           sc_conv_w, ssd_conv_w, ssd_conv_b, ssd_dt_bias, ssd_a_log, ssd_d, ssd_norm_w,
           od_w_in, od_w_out, hg_lb_logits, hg_norm_w, peer_wq, peer_keys, peer_u, peer_v):
    Bsz, seq, D = x.shape
    ctx_len = ctx.shape[1]
    assert D == D_MODEL and ctx_len == TOKEN_TILE and seq % TOKEN_TILE == 0 and TOKEN_TILE % GRID_W == 0
    lb_sm = jax.nn.softmax(hg_lb_logits.astype(F32), axis=0)
    lower_bounds = jnp.cumsum(lb_sm, axis=0) - lb_sm[0]
    cond = jnp.concatenate([c, c_ctx[None, :], jnp.zeros((7 - Bsz % 8, D), F32)], axis=0)

    ctx_tiles = ctx_len // TOKEN_TILE
    tiles_per_seq = ctx_tiles + seq // TOKEN_TILE
    stream = jnp.concatenate([ctx, x], axis=1).reshape(Bsz * (ctx_len + seq), D)
    for layer in range(DEPTH):
        last = layer == DEPTH - 1
        j = layer // 2
        m = ada_modulation(cond, ada_w[layer], ada_b[layer]).reshape(-1, 6, D)
        mods = [jnp.stack([jnp.broadcast_to(m[Bsz, k], (Bsz, D)), m[:Bsz, k]], axis=1)[:, :, None, :] for k in range(6)]
        if layer % 2 == 0:
            stream = even_layer(stream, mods, Bsz, tiles_per_seq, ctx_tiles, norm_mix_w[layer], ev_w_in[j], ev_w_out[j],
                                sc_conv_w[j], ssd_conv_w[j], ssd_conv_b[j], ssd_dt_bias[j], ssd_a_log[j], ssd_d[j],
                                ssd_norm_w[j])
        else:
            stream = odd_layer(stream, mods, Bsz, tiles_per_seq, ctx_tiles, norm_mix_w[layer], od_w_in[j], od_w_out[j],
                               lower_bounds[layer], hg_norm_w[j])
        if last:
            stream = stream.reshape(Bsz, ctx_len + seq, D)[:, ctx_len:].reshape(Bsz * seq, D)
            tiles_per_seq, ctx_tiles = seq // TOKEN_TILE, 0
        stream = peer_ffn(stream, mods, norm_ffn_w[layer], peer_wq[layer], peer_keys[layer], peer_u[layer],
                          peer_v[layer], tiles_per_seq, ctx_tiles)
    return final_rmsnorm(stream, norm_f_w).reshape(Bsz, seq, D)
```

```python
import functools

import jax
import jax.numpy as jnp
from jax import lax
from jax.experimental import pallas as pl
from jax.experimental.pallas import tpu as pltpu

D_MODEL = 1024
DEPTH = 2
GRID_W = 64
CTX_LEN = 256
NORM_EPS = 1e-6
CONV_W = 3
SC_WIDTH = 1024
SSD_HEADS = 16
SSD_HEAD_DIM = 64
SSD_WIDTH = SSD_HEADS * SSD_HEAD_DIM
SSD_GROUPS = 4
SSD_STATE = 128
SSD_BC = SSD_GROUPS * SSD_STATE
SSD_CONV_DIM = SSD_WIDTH + 2 * SSD_BC
EVEN_MIX = SC_WIDTH + SSD_WIDTH
HG_HEADS = 8
HG_KDIM = 128
HG_VDIM = 128
HG_WIDTH = HG_HEADS * HG_KDIM
PEER_HEADS = 8
PEER_NKEYS = 128
PEER_QDIM = 256
PEER_HALF = PEER_QDIM // 2
PEER_TOPK = 16
PEER_SEL = PEER_HEADS * PEER_TOPK
PEER_QW = PEER_HEADS * PEER_QDIM

LANES = 128
TOKEN_TILE = 256
SSD_CHUNK = 128
GLA_CHUNK = 64
ROW_WORDS = D_MODEL // (2 * LANES)
TILE_STRIDE = PEER_SEL + 8
PEER_GROUP = 8
MIB = 1024 * 1024
PEER_VMEM_BYTES = 48 * MIB
MIX_VMEM_BYTES = 48 * MIB
HI = lax.Precision.HIGHEST
BF = jnp.bfloat16
F32 = jnp.float32


def _mod_spec(tiles_per_seq, ctx_tiles):
    return pl.BlockSpec((1, 1, 1, D_MODEL),
                        lambda i: (i // tiles_per_seq, jnp.where(i % tiles_per_seq < ctx_tiles, 0, 1), 0, 0))


def _tok_spec(width):
    return pl.BlockSpec((TOKEN_TILE, width), lambda i: (i, 0))


def _row_spec(width):
    return pl.BlockSpec((1, width), lambda i: (0, 0))


_RESIDENT = pl.BlockSpec(memory_space=pltpu.VMEM)


def _norm_mod(x, nw, shift, scale):
    y = x * lax.rsqrt(jnp.mean(x * x, axis=-1, keepdims=True) + NORM_EPS) * nw
    return y * (1.0 + scale) + shift


def _silu(x):
    return x * jax.nn.sigmoid(x)


def _ada_body(c_ref, w_ref, b_ref, o_ref):
    cond = _silu(c_ref[...]).astype(BF)
    o_ref[...] = jnp.dot(cond, w_ref[...].astype(BF), preferred_element_type=F32) + b_ref[...]


def ada_modulation(cond, w, b):
    rows = cond.shape[0]
    return pl.pallas_call(
        _ada_body, grid=(6,),
        in_specs=[pl.BlockSpec((rows, D_MODEL), lambda j: (0, 0)),
                  pl.BlockSpec((D_MODEL, D_MODEL), lambda j: (0, j)),
                  pl.BlockSpec((1, D_MODEL), lambda j: (0, j))],
        out_specs=pl.BlockSpec((rows, D_MODEL), lambda j: (0, j)),
        out_shape=jax.ShapeDtypeStruct((rows, 6 * D_MODEL), F32),
        name="ada_modulation",
    )(cond, w, b.reshape(1, 6 * D_MODEL))


EVEN_W_COLS = 3 * SC_WIDTH + SSD_WIDTH + SSD_CONV_DIM + 2 * LANES


def _even_in_body(x_ref, shift_ref, scale_ref, nw_ref, w_ref, cw_sc_ref, cw_ssd_ref, cb_ssd_ref,
                  ya_ref, z_ref, xs_ref, bm_ref, cm_ref, dt_ref, *, tiles_per_seq, ctx_tiles):
    T = TOKEN_TILE
    h = _norm_mod(x_ref[...], nw_ref[...], shift_ref[0, 0], scale_ref[0, 0]).astype(BF)
    is_ctx = (pl.program_id(0) % tiles_per_seq) < ctx_tiles
    row_len = jnp.where(is_ctx, T, GRID_W)
    pos = lax.broadcasted_iota(jnp.int32, (T, 1), 0) & (row_len - 1)
    first, last = pos == 0, pos == row_len - 1

    def conv(u, cw_ref):
        prev = jnp.where(first, 0.0, pltpu.roll(u, 1, 0))
        nxt = jnp.where(last, 0.0, pltpu.roll(u, T - 1, 0))
        return prev * cw_ref[0:1, :] + u * cw_ref[1:2, :] + nxt * cw_ref[2:3, :]

    def proj(lo, width):
        return jnp.dot(h, w_ref[:, lo:lo + width], preferred_element_type=F32)

    sc_x, sc_b, sc_c = proj(0, SC_WIDTH), proj(SC_WIDTH, SC_WIDTH), proj(2 * SC_WIDTH, SC_WIDTH)
    ya_ref[...] = sc_b * conv(sc_c * sc_x, cw_sc_ref)
    z_ref[...] = proj(3 * SC_WIDTH, SSD_WIDTH)
    xbc = _silu(conv(proj(3 * SC_WIDTH + SSD_WIDTH, SSD_CONV_DIM), cw_ssd_ref) + cb_ssd_ref[...])
    xs_ref[...] = xbc[:, :SSD_WIDTH]
    bm_ref[...] = xbc[:, SSD_WIDTH:SSD_WIDTH + SSD_BC]
    cm_ref[...] = xbc[:, SSD_WIDTH + SSD_BC:]
    dt_ref[...] = proj(3 * SC_WIDTH + SSD_WIDTH + SSD_CONV_DIM, 2 * LANES)


def even_in(x, mod_shift, mod_scale, norm_w, w_in, sc_conv_w, ssd_conv_w, ssd_conv_b, tiles_per_seq, ctx_tiles):
    n = x.shape[0]
    main = 3 * SC_WIDTH + SSD_WIDTH + SSD_CONV_DIM
    w_dt = jnp.zeros((D_MODEL, 2, LANES), F32).at[:, :, :SSD_HEADS].set(w_in[:, main:].reshape(D_MODEL, 2, SSD_HEADS))
    w = jnp.concatenate([w_in[:, :main], w_dt.reshape(D_MODEL, 2 * LANES)], axis=1).astype(BF)
    mod = _mod_spec(tiles_per_seq, ctx_tiles)
    widths = (SC_WIDTH, SSD_WIDTH, SSD_WIDTH, SSD_BC, SSD_BC, 2 * LANES)
    return pl.pallas_call(
        functools.partial(_even_in_body, tiles_per_seq=tiles_per_seq, ctx_tiles=ctx_tiles),
        grid=(n // TOKEN_TILE,),
        in_specs=[_tok_spec(D_MODEL), mod, mod, _row_spec(D_MODEL), _RESIDENT,
                  pl.BlockSpec((CONV_W, SC_WIDTH), lambda i: (0, 0)),
                  pl.BlockSpec((CONV_W, SSD_CONV_DIM), lambda i: (0, 0)), _row_spec(SSD_CONV_DIM)],
        out_specs=[_tok_spec(wd) for wd in widths],
        out_shape=[jax.ShapeDtypeStruct((n, wd), F32) for wd in widths],
        compiler_params=pltpu.CompilerParams(vmem_limit_bytes=MIX_VMEM_BYTES),
        name="even_in",
    )(x, mod_shift, mod_scale, norm_w.reshape(1, D_MODEL), w, sc_conv_w, ssd_conv_w,
      ssd_conv_b.reshape(1, SSD_CONV_DIM))


def _scan_row(n_chunks, ctx_chunks):
    def row(b, d, c):
        back = jnp.where(c < ctx_chunks, ctx_chunks - 1 - c, n_chunks - 1 + ctx_chunks - c)
        return b * n_chunks + jnp.where(d == 0, c, back)
    return row


def _causal_mask(L, d):
    r = lax.broadcasted_iota(jnp.int32, (L, L), 0)
    c = lax.broadcasted_iota(jnp.int32, (L, L), 1)
    return jnp.where(d == 0, c - r, r - c) <= 0


SSD_GW = (SSD_HEADS // SSD_GROUPS) * SSD_HEAD_DIM


def _ssd_body(xs_ref, bm_ref, cm_ref, dt_ref, bias_ref, a_ref, y_ref, state):
    d = pl.program_id(1)
    L = SSD_CHUNK

    @pl.when(pl.program_id(2) == 0)
    def _():
        state[...] = jnp.zeros_like(state)

    mask = _causal_mask(L, d)
    v = dt_ref[...] + bias_ref[0]
    dt = jnp.maximum(v, 0.0) + jnp.log1p(jnp.exp(-jnp.abs(v)))
    la = jnp.dot(mask.astype(F32), dt * a_ref[0], precision=HI, preferred_element_type=F32)
    la_t = la.T
    head_of_lane = lax.broadcasted_iota(jnp.int32, (LANES, SSD_WIDTH), 1) // SSD_HEAD_DIM
    expand = (lax.broadcasted_iota(jnp.int32, (LANES, SSD_WIDTH), 0) == head_of_lane).astype(F32)
    dt_x = jnp.dot(dt, expand, precision=HI, preferred_element_type=F32)
    la_x = jnp.dot(la, expand, precision=HI, preferred_element_type=F32)
    end_x = jnp.where(d == 0, la_x[L - 1:L], la_x[0:1])
    xdt = xs_ref[...] * dt_x
    w_state = (xdt * jnp.exp(end_x - la_x)).astype(BF)
    xdt_b = xdt.astype(BF)
    e_la = jnp.exp(la_x)
    dec = jnp.exp(end_x)
    for g in range(SSD_GROUPS):
        bm = bm_ref[:, g * SSD_STATE:(g + 1) * SSD_STATE].astype(BF)
        cm = cm_ref[:, g * SSD_STATE:(g + 1) * SSD_STATE].astype(BF)
        cb = lax.dot_general(cm, bm, (((1,), (1,)), ((), ())), preferred_element_type=F32)
        lanes = slice(g * SSD_GW, (g + 1) * SSD_GW)
        s_old = state[g]
        y_in = jnp.dot(cm, s_old.astype(BF), preferred_element_type=F32) * e_la[:, lanes]
        ys = []
        for j in range(SSD_HEADS // SSD_GROUPS):
            h = g * (SSD_HEADS // SSD_GROUPS) + j
            seg = la[:, h:h + 1] - la_t[h:h + 1, :]
            m = (cb * jnp.exp(jnp.where(mask, seg, -jnp.inf))).astype(BF)
            ys.append(jnp.dot(m, xdt_b[:, h * SSD_HEAD_DIM:(h + 1) * SSD_HEAD_DIM], preferred_element_type=F32))
        y_ref[0, :, lanes] = y_in + jnp.concatenate(ys, axis=1)
        state[g] = dec[:, lanes] * s_old + jnp.dot(bm.T, w_state[:, lanes], preferred_element_type=F32)


def ssd_scan(xs, bm, cm, dt_raw, dt_bias, a, n_seq, ctx_chunks):
    n = xs.shape[0]
    n_chunks = n // n_seq // SSD_CHUNK
    row = _scan_row(n_chunks, ctx_chunks)
    pad = lambda p: jnp.zeros((2, 1, LANES), F32).at[:, 0, :SSD_HEADS].set(p)
    tok = lambda width: pl.BlockSpec((SSD_CHUNK, width), lambda b, d, c: (row(b, d, c), 0))
    per_dir = pl.BlockSpec((1, 1, LANES), lambda b, d, c: (d, 0, 0))
    return pl.pallas_call(
        _ssd_body, grid=(n_seq, 2, n_chunks),
        in_specs=[tok(SSD_WIDTH), tok(SSD_BC), tok(SSD_BC),
                  pl.BlockSpec((SSD_CHUNK, LANES), lambda b, d, c: (row(b, d, c), d)), per_dir, per_dir],
        out_specs=pl.BlockSpec((1, SSD_CHUNK, SSD_WIDTH), lambda b, d, c: (d, row(b, d, c), 0)),
        out_shape=jax.ShapeDtypeStruct((2, n, SSD_WIDTH), F32),
        scratch_shapes=[pltpu.VMEM((SSD_GROUPS, SSD_STATE, SSD_GW), F32)],
        compiler_params=pltpu.CompilerParams(dimension_semantics=("arbitrary", "arbitrary", "arbitrary")),
        name="ssd_scan",
    )(xs, bm, cm, dt_raw, pad(dt_bias), pad(a))


def _gla_body(q_ref, f_ref, v_ref, lb_ref, o_ref, state):
    d = pl.program_id(1)
    L = GLA_CHUNK

    @pl.when(pl.program_id(2) == 0)
    def _():
        state[...] = jnp.zeros_like(state)

    mask = _causal_mask(L, d)
    lb = lb_ref[0]
    f = lb + (1.0 - lb) * jax.nn.sigmoid(f_ref[0])
    gc = jnp.dot(mask.astype(F32), jnp.log(f), precision=HI, preferred_element_type=F32)
    g_end = jnp.where(d == 0, gc[L - 1:L], gc[0:1])
    k = 1.0 - f
    qg = (q_ref[...] * jnp.exp(gc)).astype(BF)
    kg = (k * jnp.exp(-gc)).astype(BF)
    k_end = (k * jnp.exp(g_end - gc)).astype(BF)
    dec = jnp.exp(g_end)
    vb = v_ref[...].astype(BF)
    for h in range(HG_HEADS):
        lanes = slice(h * HG_KDIM, (h + 1) * HG_KDIM)
        att = lax.dot_general(qg[:, lanes], kg[:, lanes], (((1,), (1,)), ((), ())), preferred_element_type=F32)
        att = jnp.where(mask, att, 0.0).astype(BF)
        s_old = state[h]
        o = jnp.dot(att, vb[:, lanes], preferred_element_type=F32)
        o_ref[0, :, lanes] = o + lax.dot_general(qg[:, lanes], s_old.astype(BF), (((1,), (1,)), ((), ())),
                                                 preferred_element_type=F32)
        state[h] = s_old * dec[:, lanes] + jnp.dot(vb[:, lanes].T, k_end[:, lanes], preferred_element_type=F32)


def gla_scan(q, f_raw, v, lb, n_seq, ctx_chunks):
    n = q.shape[0]
    n_chunks = n // n_seq // GLA_CHUNK
    row = _scan_row(n_chunks, ctx_chunks)
    tok = pl.BlockSpec((GLA_CHUNK, HG_WIDTH), lambda b, d, c: (row(b, d, c), 0))
    dtok = pl.BlockSpec((1, GLA_CHUNK, HG_WIDTH), lambda b, d, c: (d, row(b, d, c), 0))
    return pl.pallas_call(
        _gla_body, grid=(n_seq, 2, n_chunks),
        in_specs=[tok, dtok, tok, pl.BlockSpec((1, 1, HG_WIDTH), lambda b, d, c: (d, 0, 0))],
        out_specs=dtok,
        out_shape=jax.ShapeDtypeStruct((2, n, HG_WIDTH), F32),
        scratch_shapes=[pltpu.VMEM((HG_HEADS, HG_VDIM, HG_KDIM), F32)],
        compiler_params=pltpu.CompilerParams(dimension_semantics=("arbitrary", "arbitrary", "arbitrary")),
        name="gla_scan",
    )(q, f_raw, v, lb)


def _group_rmsnorm(y, groups):
    width = y.shape[1] // groups
    parts = []
    for g in range(groups):
        seg = y[:, g * width:(g + 1) * width]
        parts.append(seg * lax.rsqrt(jnp.mean(seg * seg, axis=-1, keepdims=True) + NORM_EPS))
    return jnp.concatenate(parts, axis=1)


def _even_out_body(x_ref, gate_ref, ya_ref, z_ref, xs_ref, y0_ref, y1_ref, dskip_ref, nw_ref, w_ref, o_ref):
    yb = (y0_ref[0] + y1_ref[0] + dskip_ref[...] * xs_ref[...]) * _silu(z_ref[...])
    yb = _group_rmsnorm(yb, SSD_GROUPS) * nw_ref[...]
    mix = jnp.concatenate([ya_ref[...], yb], axis=1).astype(BF)
    o_ref[...] = x_ref[...] + gate_ref[0, 0] * jnp.dot(mix, w_ref[...], preferred_element_type=F32)


def even_out(x, mod_gate, ya, z, xs, ys, d_skip, norm_w, w_out, tiles_per_seq, ctx_tiles):
    n = x.shape[0]
    ydir = lambda d: pl.BlockSpec((1, TOKEN_TILE, SSD_WIDTH), lambda i: (d, i, 0))
    return pl.pallas_call(
        _even_out_body, grid=(n // TOKEN_TILE,),
        in_specs=[_tok_spec(D_MODEL), _mod_spec(tiles_per_seq, ctx_tiles), _tok_spec(SC_WIDTH), _tok_spec(SSD_WIDTH),
                  _tok_spec(SSD_WIDTH), ydir(0), ydir(1), _row_spec(SSD_WIDTH), _row_spec(SSD_WIDTH), _RESIDENT],
        out_specs=_tok_spec(D_MODEL),
        out_shape=jax.ShapeDtypeStruct((n, D_MODEL), F32),
        compiler_params=pltpu.CompilerParams(vmem_limit_bytes=MIX_VMEM_BYTES),
        name="even_out",
    )(x, mod_gate, ya, z, xs, ys, ys, jnp.repeat(d_skip, SSD_HEAD_DIM).reshape(1, SSD_WIDTH),
      norm_w.reshape(1, SSD_WIDTH), w_out.astype(BF))


def _odd_in_body(x_ref, shift_ref, scale_ref, nw_ref, w_ref, q_ref, f_ref, v_ref, g_ref):
    h = _norm_mod(x_ref[...], nw_ref[...], shift_ref[0, 0], scale_ref[0, 0]).astype(BF)
    proj = lambda j: jnp.dot(h, w_ref[:, j * HG_WIDTH:(j + 1) * HG_WIDTH], preferred_element_type=F32)
    q_ref[...] = _silu(proj(0)) * (HG_KDIM ** -0.5)
    f_ref[0] = proj(1)
    f_ref[1] = proj(2)
    v_ref[...] = proj(3)
    g_ref[...] = proj(4)


def odd_in(x, mod_shift, mod_scale, norm_w, w_in, tiles_per_seq, ctx_tiles):
    n = x.shape[0]
    mod = _mod_spec(tiles_per_seq, ctx_tiles)
    tok = _tok_spec(HG_WIDTH)
    sds = jax.ShapeDtypeStruct((n, HG_WIDTH), F32)
    return pl.pallas_call(
        _odd_in_body, grid=(n // TOKEN_TILE,),
        in_specs=[_tok_spec(D_MODEL), mod, mod, _row_spec(D_MODEL), _RESIDENT],
        out_specs=[tok, pl.BlockSpec((2, TOKEN_TILE, HG_WIDTH), lambda i: (0, i, 0)), tok, tok],
        out_shape=[sds, jax.ShapeDtypeStruct((2, n, HG_WIDTH), F32), sds, sds],
        compiler_params=pltpu.CompilerParams(vmem_limit_bytes=MIX_VMEM_BYTES),
        name="odd_in",
    )(x, mod_shift, mod_scale, norm_w.reshape(1, D_MODEL), w_in.astype(BF))


def _odd_out_body(x_ref, gate_ref, o0_ref, o1_ref, g_ref, nw_ref, w_ref, out_ref):
    o = _group_rmsnorm(o0_ref[0] + o1_ref[0], HG_HEADS) * nw_ref[...] * _silu(g_ref[...])
    out_ref[...] = x_ref[...] + gate_ref[0, 0] * jnp.dot(o.astype(BF), w_ref[...], preferred_element_type=F32)


def odd_out(x, mod_gate, os_, g, norm_w, w_out, tiles_per_seq, ctx_tiles):
    n = x.shape[0]
    odir = lambda d: pl.BlockSpec((1, TOKEN_TILE, HG_WIDTH), lambda i: (d, i, 0))
    return pl.pallas_call(
        _odd_out_body, grid=(n // TOKEN_TILE,),
        in_specs=[_tok_spec(D_MODEL), _mod_spec(tiles_per_seq, ctx_tiles), odir(0), odir(1), _tok_spec(HG_WIDTH),
                  _row_spec(HG_WIDTH), _RESIDENT],
        out_specs=_tok_spec(D_MODEL),
        out_shape=jax.ShapeDtypeStruct((n, D_MODEL), F32),
        compiler_params=pltpu.CompilerParams(vmem_limit_bytes=MIX_VMEM_BYTES),
        name="odd_out",
    )(x, mod_gate, os_, os_, g, norm_w.reshape(1, HG_WIDTH), w_out.astype(BF))


def _top16(vals, ids, n_ids):
    top_s, top_i = [], []
    for r in range(PEER_TOPK):
        m = jnp.max(vals, axis=0, keepdims=True)
        pos = jnp.min(jnp.where(vals == m, ids, float(n_ids)), axis=0, keepdims=True)
        if r + 1 < PEER_TOPK:
            vals = jnp.where(ids == pos, -jnp.inf, vals)
        top_s.append(m)
        top_i.append(pos)
    return jnp.concatenate(top_s, axis=0), jnp.concatenate(top_i, axis=0).astype(jnp.int32)


def _pick_row(table, row):
    r = lax.broadcasted_iota(jnp.int32, table.shape, 0)
    return jnp.sum(jnp.where(r == row, table, 0), axis=0, keepdims=True)


def _route_body(x_ref, shift_ref, scale_ref, nw_ref, wq_ref, keys_ref, h_ref, idx_ref, gate_ref,
                q_scr, idx_scr, gate_scr):
    h = _norm_mod(x_ref[...], nw_ref[...], shift_ref[0, 0], scale_ref[0, 0])
    h_ref[...] = h
    q_scr[...] = jnp.dot(h.astype(BF), wq_ref[...], preferred_element_type=F32)

    def route(hd, tok0):
        tops = []
        for a in range(2):
            col = pl.multiple_of(hd * PEER_QDIM + a * PEER_HALF, PEER_HALF)
            qs = q_scr[pl.ds(tok0, LANES), pl.ds(col, PEER_HALF)].astype(BF)
            sc = lax.dot_general(keys_ref[hd * 2 + a], qs, (((1,), (1,)), ((), ())),
                                 preferred_element_type=F32)
            tops.append(_top16(sc, lax.broadcasted_iota(jnp.int32, sc.shape, 0).astype(F32), PEER_NKEYS))
        (s0, i0), (s1, i1) = tops
        sub = lax.broadcasted_iota(jnp.int32, (8, LANES), 0).astype(F32)
        chunks = [(s0[0:8] + s1[0:1], sub * PEER_TOPK), (s0[8:16] + s1[0:1], (sub + 8) * PEER_TOPK)]
        chunks += [(s0[0:8] + s1[b:b + 1], sub * PEER_TOPK + b) for b in range(1, 8)]
        chunks.append((s0[0:1] + s1[8:16], sub + 8))
        best, pos = _top16(jnp.concatenate([c[0] for c in chunks], axis=0),
                           jnp.concatenate([c[1] for c in chunks], axis=0), PEER_TOPK * PEER_TOPK)
        ids = jnp.concatenate(
            [_pick_row(i0, pos[r:r + 1] >> 4) * PEER_NKEYS + _pick_row(i1, pos[r:r + 1] & (PEER_TOPK - 1))
             for r in range(PEER_TOPK)], axis=0)
        e = jnp.exp(best - best[0:1])
        r0 = pl.multiple_of(hd * PEER_TOPK, PEER_TOPK)
        idx_scr[pl.ds(r0, PEER_TOPK), pl.ds(tok0, LANES)] = ids * ROW_WORDS
        gate_scr[pl.ds(r0, PEER_TOPK), pl.ds(tok0, LANES)] = e / jnp.sum(e, axis=0, keepdims=True)

    def head_pair(hp, carry):
        for hh in range(2):
            for lg in range(TOKEN_TILE // LANES):
                route(hp * 2 + hh, lg * LANES)
        return carry

    lax.fori_loop(0, PEER_HEADS // 2, head_pair, 0)
    idx_ref[...] = idx_scr[...].T
    gate_ref[...] = gate_scr[...].T


def peer_route(x, mod_shift, mod_scale, norm_w, wq, keys, tiles_per_seq, ctx_tiles):
    n = x.shape[0]
    mod = _mod_spec(tiles_per_seq, ctx_tiles)
    sel = _tok_spec(PEER_SEL)
    return pl.pallas_call(
        _route_body, grid=(n // TOKEN_TILE,),
        in_specs=[_tok_spec(D_MODEL), mod, mod, _row_spec(D_MODEL),
                  pl.BlockSpec((D_MODEL, PEER_QW), lambda i: (0, 0)),
                  pl.BlockSpec((PEER_HEADS * 2, PEER_NKEYS, PEER_HALF), lambda i: (0, 0, 0))],
        out_specs=[_tok_spec(D_MODEL), sel, sel],
        out_shape=[jax.ShapeDtypeStruct((n, D_MODEL), F32),
                   jax.ShapeDtypeStruct((n, PEER_SEL), jnp.int32),
                   jax.ShapeDtypeStruct((n, PEER_SEL), F32)],
        scratch_shapes=[pltpu.VMEM((TOKEN_TILE, PEER_QW), F32),
                        pltpu.VMEM((PEER_SEL, TOKEN_TILE), jnp.int32),
                        pltpu.VMEM((PEER_SEL, TOKEN_TILE), F32)],
        compiler_params=pltpu.CompilerParams(vmem_limit_bytes=MIX_VMEM_BYTES),
        name="peer_route",
    )(x, mod_shift, mod_scale, norm_w.reshape(1, D_MODEL), wq.astype(BF),
      keys.reshape(PEER_HEADS * 2, PEER_NKEYS, PEER_HALF).astype(BF))


def pack_table(tab):
    E = tab.shape[0]
    bits = lax.bitcast_convert_type(tab.astype(BF), jnp.uint16).astype(jnp.uint32)
    words = bits[:, :D_MODEL // 2] | (bits[:, D_MODEL // 2:] << 16)
    return lax.bitcast_convert_type(words, jnp.int32).reshape(E * ROW_WORDS, LANES)


def _gather_group(idx_ref, tab_ref, tile_refs, tokens):
    rows = [idx_ref.at[t] for t in tokens]
    for k in range(PEER_SEL):
        for row, tile_ref in zip(rows, tile_refs):
            e = pl.multiple_of(row[k], ROW_WORDS)
            tile_ref[pl.ds(k, ROW_WORDS, stride=TILE_STRIDE), :] = tab_ref[pl.ds(e, ROW_WORDS), :]


def _unpack_rows(tile_ref):
    lo, hi = [], []
    for j in range(ROW_WORDS):
        w = tile_ref[j * TILE_STRIDE:j * TILE_STRIDE + PEER_SEL, :]
        lo.append(pltpu.bitcast(w << 16, F32))
        hi.append(pltpu.bitcast(w & jnp.int32(-65536), F32))
    return jnp.concatenate(lo + hi, axis=1).astype(BF)


def _for_each_token(idx_ref, tab_ref, tiles, compute):
    set_a, set_b = tiles[:PEER_GROUP], tiles[PEER_GROUP:]
    _gather_group(idx_ref, tab_ref, set_a, list(range(PEER_GROUP)))

    def step(s, carry):
        base = s * 2 * PEER_GROUP
        _gather_group(idx_ref, tab_ref, set_b, [base + PEER_GROUP + i for i in range(PEER_GROUP)])
        for i in range(PEER_GROUP):
            compute(set_a[i], base + i)
        for i in range(PEER_GROUP):
            compute(set_b[i], base + PEER_GROUP + i)
        _gather_group(idx_ref, tab_ref, set_a,
                      [jnp.minimum(base + 2 * PEER_GROUP + i, TOKEN_TILE - 1) for i in range(PEER_GROUP)])
        return carry

    lax.fori_loop(0, TOKEN_TILE // (2 * PEER_GROUP), step, 0)


def _peer_u_body(idx_ref, h_ref, gate_ref, tab_ref, act_ref, *tiles):
    def compute(tile, t):
        hb = jnp.broadcast_to(h_ref[pl.ds(t, 1), :], (8, D_MODEL)).astype(BF)
        s = lax.dot_general(hb, _unpack_rows(tile), (((1,), (1,)), ((), ())), preferred_element_type=F32)
        act_ref[pl.ds(t, 1), :] = s[0:1]

    _for_each_token(idx_ref, tab_ref, tiles, compute)
    d = act_ref[...]
    act_ref[...] = 0.5 * d * (1.0 + lax.erf(d * (2.0 ** -0.5))) * gate_ref[...]


def _peer_v_body(idx_ref, act_ref, x_ref, gate_ref, tab_ref, out_ref, *tiles):
    def compute(tile, t):
        ab = jnp.broadcast_to(act_ref[pl.ds(t, 1), :], (8, PEER_SEL)).astype(BF)
        o = jnp.dot(ab, _unpack_rows(tile), preferred_element_type=F32)
        out_ref[pl.ds(t, 1), :] = o[0:1]

    _for_each_token(idx_ref, tab_ref, tiles, compute)
    out_ref[...] = x_ref[...] + gate_ref[0, 0] * out_ref[...]


def peer_experts(x, h, idx_words, gate, mod_gate, u_pack, v_pack, tiles_per_seq, ctx_tiles):
    n = h.shape[0]
    grid = (n // TOKEN_TILE,)
    tiles = [pltpu.VMEM((ROW_WORDS * TILE_STRIDE, LANES), jnp.int32)] * (2 * PEER_GROUP)
    params = pltpu.CompilerParams(vmem_limit_bytes=PEER_VMEM_BYTES)
    idx_spec = pl.BlockSpec((TOKEN_TILE, PEER_SEL), lambda i: (i, 0), memory_space=pltpu.SMEM)
    sel_spec = _tok_spec(PEER_SEL)
    act = pl.pallas_call(
        _peer_u_body, grid=grid,
        in_specs=[idx_spec, _tok_spec(D_MODEL), sel_spec, _RESIDENT],
        out_specs=sel_spec,
        out_shape=jax.ShapeDtypeStruct((n, PEER_SEL), F32),
        scratch_shapes=tiles, compiler_params=params, name="peer_u",
    )(idx_words, h, gate, u_pack)
    return pl.pallas_call(
        _peer_v_body, grid=grid,
        in_specs=[idx_spec, sel_spec, _tok_spec(D_MODEL), _mod_spec(tiles_per_seq, ctx_tiles), _RESIDENT],
        out_specs=_tok_spec(D_MODEL),
        out_shape=jax.ShapeDtypeStruct((n, D_MODEL), F32),
        scratch_shapes=tiles, compiler_params=params, name="peer_v",
    )(idx_words, act, x, mod_gate, v_pack)


def peer_ffn(x, mods, norm_w, wq, keys, u_tab, v_tab, tiles_per_seq, ctx_tiles):
    h, idx_words, gate = peer_route(x, mods[3], mods[4], norm_w, wq, keys, tiles_per_seq, ctx_tiles)
    return peer_experts(x, h, idx_words, gate, mods[5], pack_table(u_tab), pack_table(v_tab), tiles_per_seq, ctx_tiles)


def _final_norm_body(x_ref, w_ref, o_ref):
    xf = x_ref[...]
    o_ref[...] = xf * lax.rsqrt(jnp.mean(xf * xf, axis=-1, keepdims=True) + NORM_EPS) * w_ref[...]


def final_rmsnorm(x, w):
    n = x.shape[0]
    return pl.pallas_call(
        _final_norm_body, grid=(n // TOKEN_TILE,),
        in_specs=[_tok_spec(D_MODEL), _row_spec(D_MODEL)],
        out_specs=_tok_spec(D_MODEL),
        out_shape=jax.ShapeDtypeStruct((n, D_MODEL), F32),
        name="final_rmsnorm",
    )(x, w.reshape(1, D_MODEL))


def even_layer(stream, mods, n_seq, tiles_per_seq, ctx_tiles, norm_w, w_in, w_out, sc_conv_w, ssd_conv_w,
               ssd_conv_b, dt_bias, a_log, d_skip, ssd_norm_w):
    ya, z, xs, bm, cm, dt_raw = even_in(stream, mods[0], mods[1], norm_w, w_in, sc_conv_w, ssd_conv_w, ssd_conv_b,
                                        tiles_per_seq, ctx_tiles)
    ys = ssd_scan(xs, bm, cm, dt_raw, dt_bias, -jnp.exp(a_log), n_seq, ctx_tiles * TOKEN_TILE // SSD_CHUNK)
    return even_out(stream, mods[2], ya, z, xs, ys, d_skip, ssd_norm_w, w_out, tiles_per_seq, ctx_tiles)


def odd_layer(stream, mods, n_seq, tiles_per_seq, ctx_tiles, norm_w, w_in, w_out, lower_bound, hg_norm_w):
    q, f_raw, v, g = odd_in(stream, mods[0], mods[1], norm_w, w_in, tiles_per_seq, ctx_tiles)
    os_ = gla_scan(q, f_raw, v, lower_bound.reshape(2, 1, HG_WIDTH), n_seq, ctx_tiles * TOKEN_TILE // GLA_CHUNK)
    return odd_out(stream, mods[2], os_, g, hg_norm_w, w_out, tiles_per_seq, ctx_tiles)


def kernel(x, c, ctx, c_ctx, ada_w, ada_b, norm_mix_w, norm_ffn_w, norm_f_w, ev_w_in, ev_w_out,
           sc_conv_w, ssd_conv_w, ssd_conv_b, ssd_dt_bias, ssd_a_log, ssd_d, ssd_norm_w,
           od_w_in, od_w_out, hg_lb_logits, hg_norm_w, peer_wq, peer_keys, peer_u, peer_v):
    Bsz, seq, D = x.shape
    ctx_len = ctx.shape[1]
    assert D == D_MODEL and ctx_len == TOKEN_TILE and seq % TOKEN_TILE == 0 and TOKEN_TILE % GRID_W == 0
    lb_sm = jax.nn.softmax(hg_lb_logits.astype(F32), axis=0)
    lower_bounds = jnp.cumsum(lb_sm, axis=0) - lb_sm[0]
    cond = jnp.concatenate([c, c_ctx[None, :], jnp.zeros((7 - Bsz % 8, D), F32)], axis=0)

    ctx_tiles = ctx_len // TOKEN_TILE
    tiles_per_seq = ctx_tiles + seq // TOKEN_TILE
    stream = jnp.concatenate([ctx, x], axis=1).reshape(Bsz * (ctx_len + seq), D)
    for layer in range(DEPTH):
        last = layer == DEPTH - 1
        j = layer // 2
        m = ada_modulation(cond, ada_w[layer], ada_b[layer]).reshape(-1, 6, D)
        mods = [jnp.stack([jnp.broadcast_to(m[Bsz, k], (Bsz, D)), m[:Bsz, k]], axis=1)[:, :, None, :] for k in range(6)]
        if layer % 2 == 0:
            stream = even_layer(stream, mods, Bsz, tiles_per_seq, ctx_tiles, norm_mix_w[layer], ev_w_in[j], ev_w_out[j],
                                sc_conv_w[j], ssd_conv_w[j], ssd_conv_b[j], ssd_dt_bias[j], ssd_a_log[j], ssd_d[j],
                                ssd_norm_w[j])
        else:
            stream = odd_layer(stream, mods, Bsz, tiles_per_seq, ctx_tiles, norm_mix_w[layer], od_w_in[j], od_w_out[j],
                               lower_bounds[layer], hg_norm_w[j])
        if last:
            stream = stream.reshape(Bsz, ctx_len + seq, D)[:, ctx_len:].reshape(Bsz * seq, D)
            tiles_per_seq, ctx_tiles = seq // TOKEN_TILE, 0
        stream = peer_ffn(stream, mods, norm_ffn_w[layer], peer_wq[layer], peer_keys[layer], peer_u[layer],
                          peer_v[layer], tiles_per_seq, ctx_tiles)
    return final_rmsnorm(stream, norm_f_w).reshape(Bsz, seq, D)
```

```python
import functools

import jax
import jax.numpy as jnp
from jax import lax
from jax.experimental import pallas as pl
from jax.experimental.pallas import tpu as pltpu

D_MODEL = 1024
DEPTH = 2
GRID_W = 64
CTX_LEN = 256
NORM_EPS = 1e-6
CONV_W = 3
SC_WIDTH = 1024
SSD_HEADS = 16
SSD_HEAD_DIM = 64
SSD_WIDTH = SSD_HEADS * SSD_HEAD_DIM
SSD_GROUPS = 4
SSD_STATE = 128
SSD_BC = SSD_GROUPS * SSD_STATE
SSD_CONV_DIM = SSD_WIDTH + 2 * SSD_BC
EVEN_MIX = SC_WIDTH + SSD_WIDTH
HG_HEADS = 8
HG_KDIM = 128
HG_VDIM = 128
HG_WIDTH = HG_HEADS * HG_KDIM
PEER_HEADS = 8
PEER_NKEYS = 128
PEER_QDIM = 256
PEER_HALF = PEER_QDIM // 2
PEER_TOPK = 16
PEER_SEL = PEER_HEADS * PEER_TOPK
PEER_QW = PEER_HEADS * PEER_QDIM

LANES = 128
TOKEN_TILE = 256
SSD_CHUNK = 128
GLA_CHUNK = 64
ROW_WORDS = D_MODEL // (2 * LANES)
TILE_STRIDE = PEER_SEL + 8
PEER_GROUP = 8
MIB = 1024 * 1024
PEER_VMEM_BYTES = 48 * MIB
MIX_VMEM_BYTES = 48 * MIB
HI = lax.Precision.HIGHEST
BF = jnp.bfloat16
F32 = jnp.float32


def _mod_spec(tiles_per_seq, ctx_tiles):
    return pl.BlockSpec((1, 1, 1, D_MODEL),
                        lambda i: (i // tiles_per_seq, jnp.where(i % tiles_per_seq < ctx_tiles, 0, 1), 0, 0))


def _tok_spec(width):
    return pl.BlockSpec((TOKEN_TILE, width), lambda i: (i, 0))


def _row_spec(width):
    return pl.BlockSpec((1, width), lambda i: (0, 0))


_RESIDENT = pl.BlockSpec(memory_space=pltpu.VMEM)
_SEL_MAJOR_SPEC = pl.BlockSpec((PEER_SEL, TOKEN_TILE), lambda i: (0, i))


def _norm_mod(x, nw, shift, scale):
    y = x * lax.rsqrt(jnp.mean(x * x, axis=-1, keepdims=True) + NORM_EPS) * nw
    return y * (1.0 + scale) + shift


def _silu(x):
    return x * jax.nn.sigmoid(x)


def _ada_body(c_ref, w_ref, b_ref, o_ref):
    cond = _silu(c_ref[...]).astype(BF)
    o_ref[...] = jnp.dot(cond, w_ref[...].astype(BF), preferred_element_type=F32) + b_ref[...]


def ada_modulation(cond, w, b):
    rows = cond.shape[0]
    return pl.pallas_call(
        _ada_body, grid=(6,),
        in_specs=[pl.BlockSpec((rows, D_MODEL), lambda j: (0, 0)),
                  pl.BlockSpec((D_MODEL, D_MODEL), lambda j: (0, j)),
                  pl.BlockSpec((1, D_MODEL), lambda j: (0, j))],
        out_specs=pl.BlockSpec((rows, D_MODEL), lambda j: (0, j)),
        out_shape=jax.ShapeDtypeStruct((rows, 6 * D_MODEL), F32),
        name="ada_modulation",
    )(cond, w, b.reshape(1, 6 * D_MODEL))


EVEN_W_COLS = 3 * SC_WIDTH + SSD_WIDTH + SSD_CONV_DIM + 2 * LANES


def _even_in_body(x_ref, shift_ref, scale_ref, nw_ref, w_ref, cw_sc_ref, cw_ssd_ref, cb_ssd_ref,
                  ya_ref, z_ref, xs_ref, bm_ref, cm_ref, dt_ref, *, tiles_per_seq, ctx_tiles):
    T = TOKEN_TILE
    h = _norm_mod(x_ref[...], nw_ref[...], shift_ref[0, 0], scale_ref[0, 0]).astype(BF)
    is_ctx = (pl.program_id(0) % tiles_per_seq) < ctx_tiles
    row_len = jnp.where(is_ctx, T, GRID_W)
    pos = lax.broadcasted_iota(jnp.int32, (T, 1), 0) & (row_len - 1)
    first, last = pos == 0, pos == row_len - 1

    def conv(u, cw_ref):
        prev = jnp.where(first, 0.0, pltpu.roll(u, 1, 0))
        nxt = jnp.where(last, 0.0, pltpu.roll(u, T - 1, 0))
        return prev * cw_ref[0:1, :] + u * cw_ref[1:2, :] + nxt * cw_ref[2:3, :]

    def proj(lo, width):
        return jnp.dot(h, w_ref[:, lo:lo + width], preferred_element_type=F32)

    sc_x, sc_b, sc_c = proj(0, SC_WIDTH), proj(SC_WIDTH, SC_WIDTH), proj(2 * SC_WIDTH, SC_WIDTH)
    ya_ref[...] = sc_b * conv(sc_c * sc_x, cw_sc_ref)
    z_ref[...] = proj(3 * SC_WIDTH, SSD_WIDTH)
    xbc = _silu(conv(proj(3 * SC_WIDTH + SSD_WIDTH, SSD_CONV_DIM), cw_ssd_ref) + cb_ssd_ref[...])
    xs_ref[...] = xbc[:, :SSD_WIDTH]
    bm_ref[...] = xbc[:, SSD_WIDTH:SSD_WIDTH + SSD_BC]
    cm_ref[...] = xbc[:, SSD_WIDTH + SSD_BC:]
    dt_ref[...] = proj(3 * SC_WIDTH + SSD_WIDTH + SSD_CONV_DIM, 2 * LANES)


def even_in(x, mod_shift, mod_scale, norm_w, w_in, sc_conv_w, ssd_conv_w, ssd_conv_b, tiles_per_seq, ctx_tiles):
    n = x.shape[0]
    main = 3 * SC_WIDTH + SSD_WIDTH + SSD_CONV_DIM
    w_dt = jnp.zeros((D_MODEL, 2, LANES), F32).at[:, :, :SSD_HEADS].set(w_in[:, main:].reshape(D_MODEL, 2, SSD_HEADS))
    w = jnp.concatenate([w_in[:, :main], w_dt.reshape(D_MODEL, 2 * LANES)], axis=1).astype(BF)
    mod = _mod_spec(tiles_per_seq, ctx_tiles)
    widths = (SC_WIDTH, SSD_WIDTH, SSD_WIDTH, SSD_BC, SSD_BC, 2 * LANES)
    return pl.pallas_call(
        functools.partial(_even_in_body, tiles_per_seq=tiles_per_seq, ctx_tiles=ctx_tiles),
        grid=(n // TOKEN_TILE,),
        in_specs=[_tok_spec(D_MODEL), mod, mod, _row_spec(D_MODEL), _RESIDENT,
                  pl.BlockSpec((CONV_W, SC_WIDTH), lambda i: (0, 0)),
                  pl.BlockSpec((CONV_W, SSD_CONV_DIM), lambda i: (0, 0)), _row_spec(SSD_CONV_DIM)],
        out_specs=[_tok_spec(wd) for wd in widths],
        out_shape=[jax.ShapeDtypeStruct((n, wd), F32) for wd in widths],
        compiler_params=pltpu.CompilerParams(vmem_limit_bytes=MIX_VMEM_BYTES),
        name="even_in",
    )(x, mod_shift, mod_scale, norm_w.reshape(1, D_MODEL), w, sc_conv_w, ssd_conv_w,
      ssd_conv_b.reshape(1, SSD_CONV_DIM))


def _scan_row(n_chunks, ctx_chunks):
    def row(b, d, c):
        back = jnp.where(c < ctx_chunks, ctx_chunks - 1 - c, n_chunks - 1 + ctx_chunks - c)
        return b * n_chunks + jnp.where(d == 0, c, back)
    return row


def _causal_mask(L, d):
    r = lax.broadcasted_iota(jnp.int32, (L, L), 0)
    c = lax.broadcasted_iota(jnp.int32, (L, L), 1)
    return jnp.where(d == 0, c - r, r - c) <= 0


SSD_GW = (SSD_HEADS // SSD_GROUPS) * SSD_HEAD_DIM


def _ssd_body(xs_ref, bm_ref, cm_ref, dt_ref, bias_ref, a_ref, y_ref, state):
    d = pl.program_id(1)
    L = SSD_CHUNK

    @pl.when(pl.program_id(2) == 0)
    def _():
        state[...] = jnp.zeros_like(state)

    mask = _causal_mask(L, d)
    v = dt_ref[...] + bias_ref[0]
    dt = jnp.maximum(v, 0.0) + jnp.log1p(jnp.exp(-jnp.abs(v)))
    la = jnp.dot(mask.astype(F32), dt * a_ref[0], precision=HI, preferred_element_type=F32)
    la_t = la.T
    head_of_lane = lax.broadcasted_iota(jnp.int32, (LANES, SSD_WIDTH), 1) // SSD_HEAD_DIM
    expand = (lax.broadcasted_iota(jnp.int32, (LANES, SSD_WIDTH), 0) == head_of_lane).astype(F32)
    dt_x = jnp.dot(dt, expand, precision=HI, preferred_element_type=F32)
    la_x = jnp.dot(la, expand, precision=HI, preferred_element_type=F32)
    end_x = jnp.where(d == 0, la_x[L - 1:L], la_x[0:1])
    xdt = xs_ref[...] * dt_x
    w_state = (xdt * jnp.exp(end_x - la_x)).astype(BF)
    xdt_b = xdt.astype(BF)
    e_la = jnp.exp(la_x)
    dec = jnp.exp(end_x)
    for g in range(SSD_GROUPS):
        bm = bm_ref[:, g * SSD_STATE:(g + 1) * SSD_STATE].astype(BF)
        cm = cm_ref[:, g * SSD_STATE:(g + 1) * SSD_STATE].astype(BF)
        cb = lax.dot_general(cm, bm, (((1,), (1,)), ((), ())), preferred_element_type=F32)
        lanes = slice(g * SSD_GW, (g + 1) * SSD_GW)
        s_old = state[g]
        y_in = jnp.dot(cm, s_old.astype(BF), preferred_element_type=F32) * e_la[:, lanes]
        ys = []
        for j in range(SSD_HEADS // SSD_GROUPS):
            h = g * (SSD_HEADS // SSD_GROUPS) + j
            seg = la[:, h:h + 1] - la_t[h:h + 1, :]
            m = (cb * jnp.exp(jnp.where(mask, seg, -jnp.inf))).astype(BF)
            ys.append(jnp.dot(m, xdt_b[:, h * SSD_HEAD_DIM:(h + 1) * SSD_HEAD_DIM], preferred_element_type=F32))
        y_ref[0, :, lanes] = y_in + jnp.concatenate(ys, axis=1)
        state[g] = dec[:, lanes] * s_old + jnp.dot(bm.T, w_state[:, lanes], preferred_element_type=F32)


def ssd_scan(xs, bm, cm, dt_raw, dt_bias, a, n_seq, ctx_chunks):
    n = xs.shape[0]
    n_chunks = n // n_seq // SSD_CHUNK
    row = _scan_row(n_chunks, ctx_chunks)
    pad = lambda p: jnp.zeros((2, 1, LANES), F32).at[:, 0, :SSD_HEADS].set(p)
    tok = lambda width: pl.BlockSpec((SSD_CHUNK, width), lambda b, d, c: (row(b, d, c), 0))
    per_dir = pl.BlockSpec((1, 1, LANES), lambda b, d, c: (d, 0, 0))
    return pl.pallas_call(
        _ssd_body, grid=(n_seq, 2, n_chunks),
        in_specs=[tok(SSD_WIDTH), tok(SSD_BC), tok(SSD_BC),
                  pl.BlockSpec((SSD_CHUNK, LANES), lambda b, d, c: (row(b, d, c), d)), per_dir, per_dir],
        out_specs=pl.BlockSpec((1, SSD_CHUNK, SSD_WIDTH), lambda b, d, c: (d, row(b, d, c), 0)),
        out_shape=jax.ShapeDtypeStruct((2, n, SSD_WIDTH), F32),
        scratch_shapes=[pltpu.VMEM((SSD_GROUPS, SSD_STATE, SSD_GW), F32)],
        compiler_params=pltpu.CompilerParams(dimension_semantics=("arbitrary", "arbitrary", "arbitrary")),
        name="ssd_scan",
    )(xs, bm, cm, dt_raw, pad(dt_bias), pad(a))


def _gla_body(q_ref, f_ref, v_ref, lb_ref, o_ref, state):
    d = pl.program_id(1)
    L = GLA_CHUNK

    @pl.when(pl.program_id(2) == 0)
    def _():
        state[...] = jnp.zeros_like(state)

    mask = _causal_mask(L, d)
    lb = lb_ref[0]
    f = lb + (1.0 - lb) * jax.nn.sigmoid(f_ref[0])
    gc = jnp.dot(mask.astype(F32), jnp.log(f), precision=HI, preferred_element_type=F32)
    g_end = jnp.where(d == 0, gc[L - 1:L], gc[0:1])
    k = 1.0 - f
    qg = (q_ref[...] * jnp.exp(gc)).astype(BF)
    kg = (k * jnp.exp(-gc)).astype(BF)
    k_end = (k * jnp.exp(g_end - gc)).astype(BF)
    dec = jnp.exp(g_end)
    vb = v_ref[...].astype(BF)
    for h in range(HG_HEADS):
        lanes = slice(h * HG_KDIM, (h + 1) * HG_KDIM)
        att = lax.dot_general(qg[:, lanes], kg[:, lanes], (((1,), (1,)), ((), ())), preferred_element_type=F32)
        att = jnp.where(mask, att, 0.0).astype(BF)
        s_old = state[h]
        o = jnp.dot(att, vb[:, lanes], preferred_element_type=F32)
        o_ref[0, :, lanes] = o + lax.dot_general(qg[:, lanes], s_old.astype(BF), (((1,), (1,)), ((), ())),
                                                 preferred_element_type=F32)
        state[h] = s_old * dec[:, lanes] + jnp.dot(vb[:, lanes].T, k_end[:, lanes], preferred_element_type=F32)


def gla_scan(q, f_raw, v, lb, n_seq, ctx_chunks):
    n = q.shape[0]
    n_chunks = n // n_seq // GLA_CHUNK
    row = _scan_row(n_chunks, ctx_chunks)
    tok = pl.BlockSpec((GLA_CHUNK, HG_WIDTH), lambda b, d, c: (row(b, d, c), 0))
    dtok = pl.BlockSpec((1, GLA_CHUNK, HG_WIDTH), lambda b, d, c: (d, row(b, d, c), 0))
    return pl.pallas_call(
        _gla_body, grid=(n_seq, 2, n_chunks),
        in_specs=[tok, dtok, tok, pl.BlockSpec((1, 1, HG_WIDTH), lambda b, d, c: (d, 0, 0))],
        out_specs=dtok,
        out_shape=jax.ShapeDtypeStruct((2, n, HG_WIDTH), F32),
        scratch_shapes=[pltpu.VMEM((HG_HEADS, HG_VDIM, HG_KDIM), F32)],
        compiler_params=pltpu.CompilerParams(dimension_semantics=("arbitrary", "arbitrary", "arbitrary")),
        name="gla_scan",
    )(q, f_raw, v, lb)


def _group_rmsnorm(y, groups):
    width = y.shape[1] // groups
    parts = []
    for g in range(groups):
        seg = y[:, g * width:(g + 1) * width]
        parts.append(seg * lax.rsqrt(jnp.mean(seg * seg, axis=-1, keepdims=True) + NORM_EPS))
    return jnp.concatenate(parts, axis=1)


def _even_out_body(x_ref, gate_ref, ya_ref, z_ref, xs_ref, y0_ref, y1_ref, dskip_ref, nw_ref, w_ref, o_ref):
    yb = (y0_ref[0] + y1_ref[0] + dskip_ref[...] * xs_ref[...]) * _silu(z_ref[...])
    yb = _group_rmsnorm(yb, SSD_GROUPS) * nw_ref[...]
    mix = jnp.concatenate([ya_ref[...], yb], axis=1).astype(BF)
    o_ref[...] = x_ref[...] + gate_ref[0, 0] * jnp.dot(mix, w_ref[...], preferred_element_type=F32)


def even_out(x, mod_gate, ya, z, xs, ys, d_skip, norm_w, w_out, tiles_per_seq, ctx_tiles):
    n = x.shape[0]
    ydir = lambda d: pl.BlockSpec((1, TOKEN_TILE, SSD_WIDTH), lambda i: (d, i, 0))
    return pl.pallas_call(
        _even_out_body, grid=(n // TOKEN_TILE,),
        in_specs=[_tok_spec(D_MODEL), _mod_spec(tiles_per_seq, ctx_tiles), _tok_spec(SC_WIDTH), _tok_spec(SSD_WIDTH),
                  _tok_spec(SSD_WIDTH), ydir(0), ydir(1), _row_spec(SSD_WIDTH), _row_spec(SSD_WIDTH), _RESIDENT],
        out_specs=_tok_spec(D_MODEL),
        out_shape=jax.ShapeDtypeStruct((n, D_MODEL), F32),
        compiler_params=pltpu.CompilerParams(vmem_limit_bytes=MIX_VMEM_BYTES),
        name="even_out",
    )(x, mod_gate, ya, z, xs, ys, ys, jnp.repeat(d_skip, SSD_HEAD_DIM).reshape(1, SSD_WIDTH),
      norm_w.reshape(1, SSD_WIDTH), w_out.astype(BF))


def _odd_in_body(x_ref, shift_ref, scale_ref, nw_ref, w_ref, q_ref, f_ref, v_ref, g_ref):
    h = _norm_mod(x_ref[...], nw_ref[...], shift_ref[0, 0], scale_ref[0, 0]).astype(BF)
    proj = lambda j: jnp.dot(h, w_ref[:, j * HG_WIDTH:(j + 1) * HG_WIDTH], preferred_element_type=F32)
    q_ref[...] = _silu(proj(0)) * (HG_KDIM ** -0.5)
    f_ref[0] = proj(1)
    f_ref[1] = proj(2)
    v_ref[...] = proj(3)
    g_ref[...] = proj(4)


def odd_in(x, mod_shift, mod_scale, norm_w, w_in, tiles_per_seq, ctx_tiles):
    n = x.shape[0]
    mod = _mod_spec(tiles_per_seq, ctx_tiles)
    tok = _tok_spec(HG_WIDTH)
    sds = jax.ShapeDtypeStruct((n, HG_WIDTH), F32)
    return pl.pallas_call(
        _odd_in_body, grid=(n // TOKEN_TILE,),
        in_specs=[_tok_spec(D_MODEL), mod, mod, _row_spec(D_MODEL), _RESIDENT],
        out_specs=[tok, pl.BlockSpec((2, TOKEN_TILE, HG_WIDTH), lambda i: (0, i, 0)), tok, tok],
        out_shape=[sds, jax.ShapeDtypeStruct((2, n, HG_WIDTH), F32), sds, sds],
        compiler_params=pltpu.CompilerParams(vmem_limit_bytes=MIX_VMEM_BYTES),
        name="odd_in",
    )(x, mod_shift, mod_scale, norm_w.reshape(1, D_MODEL), w_in.astype(BF))


def _odd_out_body(x_ref, gate_ref, o0_ref, o1_ref, g_ref, nw_ref, w_ref, out_ref):
    o = _group_rmsnorm(o0_ref[0] + o1_ref[0], HG_HEADS) * nw_ref[...] * _silu(g_ref[...])
    out_ref[...] = x_ref[...] + gate_ref[0, 0] * jnp.dot(o.astype(BF), w_ref[...], preferred_element_type=F32)


def odd_out(x, mod_gate, os_, g, norm_w, w_out, tiles_per_seq, ctx_tiles):
    n = x.shape[0]
    odir = lambda d: pl.BlockSpec((1, TOKEN_TILE, HG_WIDTH), lambda i: (d, i, 0))
    return pl.pallas_call(
        _odd_out_body, grid=(n // TOKEN_TILE,),
        in_specs=[_tok_spec(D_MODEL), _mod_spec(tiles_per_seq, ctx_tiles), odir(0), odir(1), _tok_spec(HG_WIDTH),
                  _row_spec(HG_WIDTH), _RESIDENT],
        out_specs=_tok_spec(D_MODEL),
        out_shape=jax.ShapeDtypeStruct((n, D_MODEL), F32),
        compiler_params=pltpu.CompilerParams(vmem_limit_bytes=MIX_VMEM_BYTES),
        name="odd_out",
    )(x, mod_gate, os_, os_, g, norm_w.reshape(1, HG_WIDTH), w_out.astype(BF))


def _top16(vals, ids, n_ids):
    top_s, top_i = [], []
    for r in range(PEER_TOPK):
        m = jnp.max(vals, axis=0, keepdims=True)
        pos = jnp.min(jnp.where(vals == m, ids, float(n_ids)), axis=0, keepdims=True)
        if r + 1 < PEER_TOPK:
            vals = jnp.where(ids == pos, -jnp.inf, vals)
        top_s.append(m)
        top_i.append(pos)
    return jnp.concatenate(top_s, axis=0), jnp.concatenate(top_i, axis=0).astype(jnp.int32)


def _pick_row(table, row):
    r = lax.broadcasted_iota(jnp.int32, table.shape, 0)
    return jnp.sum(jnp.where(r == row, table, 0), axis=0, keepdims=True)


def _route_body(x_ref, shift_ref, scale_ref, nw_ref, wq_ref, keys_ref, h_ref, idx_ref, idxt_ref, gate_ref,
                q_scr, idx_scr, gate_scr):
    h = _norm_mod(x_ref[...], nw_ref[...], shift_ref[0, 0], scale_ref[0, 0])
    h_ref[...] = h
    q_scr[...] = jnp.dot(h.astype(BF), wq_ref[...], preferred_element_type=F32)

    def route(hd, tok0):
        tops = []
        for a in range(2):
            col = pl.multiple_of(hd * PEER_QDIM + a * PEER_HALF, PEER_HALF)
            qs = q_scr[pl.ds(tok0, LANES), pl.ds(col, PEER_HALF)].astype(BF)
            sc = lax.dot_general(keys_ref[hd * 2 + a], qs, (((1,), (1,)), ((), ())),
                                 preferred_element_type=F32)
            tops.append(_top16(sc, lax.broadcasted_iota(jnp.int32, sc.shape, 0).astype(F32), PEER_NKEYS))
        (s0, i0), (s1, i1) = tops
        sub = lax.broadcasted_iota(jnp.int32, (8, LANES), 0).astype(F32)
        chunks = [(s0[0:8] + s1[0:1], sub * PEER_TOPK), (s0[8:16] + s1[0:1], (sub + 8) * PEER_TOPK)]
        chunks += [(s0[0:8] + s1[b:b + 1], sub * PEER_TOPK + b) for b in range(1, 8)]
        chunks.append((s0[0:1] + s1[8:16], sub + 8))
        best, pos = _top16(jnp.concatenate([c[0] for c in chunks], axis=0),
                           jnp.concatenate([c[1] for c in chunks], axis=0), PEER_TOPK * PEER_TOPK)
        ids = jnp.concatenate(
            [_pick_row(i0, pos[r:r + 1] >> 4) * PEER_NKEYS + _pick_row(i1, pos[r:r + 1] & (PEER_TOPK - 1))
             for r in range(PEER_TOPK)], axis=0)
        e = jnp.exp(best - best[0:1])
        r0 = pl.multiple_of(hd * PEER_TOPK, PEER_TOPK)
        idx_scr[pl.ds(r0, PEER_TOPK), pl.ds(tok0, LANES)] = ids * ROW_WORDS
        gate_scr[pl.ds(r0, PEER_TOPK), pl.ds(tok0, LANES)] = e / jnp.sum(e, axis=0, keepdims=True)

    def head_pair(hp, carry):
        for hh in range(2):
            for lg in range(TOKEN_TILE // LANES):
                route(hp * 2 + hh, lg * LANES)
        return carry

    lax.fori_loop(0, PEER_HEADS // 2, head_pair, 0)
    idx_ref[...] = idx_scr[...].T
    idxt_ref[...] = idx_scr[...]
    gate_ref[...] = gate_scr[...].T


def peer_route(x, mod_shift, mod_scale, norm_w, wq, keys, tiles_per_seq, ctx_tiles):
    n = x.shape[0]
    mod = _mod_spec(tiles_per_seq, ctx_tiles)
    sel = _tok_spec(PEER_SEL)
    return pl.pallas_call(
        _route_body, grid=(n // TOKEN_TILE,),
        in_specs=[_tok_spec(D_MODEL), mod, mod, _row_spec(D_MODEL),
                  pl.BlockSpec((D_MODEL, PEER_QW), lambda i: (0, 0)),
                  pl.BlockSpec((PEER_HEADS * 2, PEER_NKEYS, PEER_HALF), lambda i: (0, 0, 0))],
        out_specs=[_tok_spec(D_MODEL), sel, _SEL_MAJOR_SPEC, sel],
        out_shape=[jax.ShapeDtypeStruct((n, D_MODEL), F32),
                   jax.ShapeDtypeStruct((n, PEER_SEL), jnp.int32),
                   jax.ShapeDtypeStruct((PEER_SEL, n), jnp.int32),
                   jax.ShapeDtypeStruct((n, PEER_SEL), F32)],
        scratch_shapes=[pltpu.VMEM((TOKEN_TILE, PEER_QW), F32),
                        pltpu.VMEM((PEER_SEL, TOKEN_TILE), jnp.int32),
                        pltpu.VMEM((PEER_SEL, TOKEN_TILE), F32)],
        compiler_params=pltpu.CompilerParams(vmem_limit_bytes=MIX_VMEM_BYTES),
        name="peer_route",
    )(x, mod_shift, mod_scale, norm_w.reshape(1, D_MODEL), wq.astype(BF),
      keys.reshape(PEER_HEADS * 2, PEER_NKEYS, PEER_HALF).astype(BF))


def pack_table(tab):
    E = tab.shape[0]
    bits = lax.bitcast_convert_type(tab.astype(BF), jnp.uint16).astype(jnp.uint32)
    words = bits[:, :D_MODEL // 2] | (bits[:, D_MODEL // 2:] << 16)
    return lax.bitcast_convert_type(words, jnp.int32).reshape(E * ROW_WORDS, LANES)


def _gather_group(idx_ref, tab_ref, tile_refs, tokens):
    if idx_ref.shape[0] == TOKEN_TILE:
        rows = [idx_ref.at[t] for t in tokens]
        read_id = lambda i, k: rows[i][k]
    else:
        cols = []
        for t in tokens:
            lane0 = pl.multiple_of((t // LANES) * LANES, LANES)
            cols.append(pltpu.roll(idx_ref[:, pl.ds(lane0, LANES)], (LANES - t % LANES) % LANES, 1))

        def read_id(i, k):
            v = cols[i][(k // 8) * 8:(k // 8) * 8 + 8, :]
            return (pltpu.roll(v, 8 - k % 8, 0) if k % 8 else v)[0, 0]

    for k in range(PEER_SEL):
        for i, tile_ref in enumerate(tile_refs):
            e = pl.multiple_of(read_id(i, k), ROW_WORDS)
            tile_ref[pl.ds(k, ROW_WORDS, stride=TILE_STRIDE), :] = tab_ref[pl.ds(e, ROW_WORDS), :]


def _unpack_rows(tile_ref):
    lo, hi = [], []
    for j in range(ROW_WORDS):
        w = tile_ref[j * TILE_STRIDE:j * TILE_STRIDE + PEER_SEL, :]
        lo.append(pltpu.bitcast(w << 16, F32))
        hi.append(pltpu.bitcast(w & jnp.int32(-65536), F32))
    return jnp.concatenate(lo + hi, axis=1).astype(BF)


def _for_each_token(idx_ref, tab_ref, tiles, compute):
    set_a, set_b = tiles[:PEER_GROUP], tiles[PEER_GROUP:]
    _gather_group(idx_ref, tab_ref, set_a, list(range(PEER_GROUP)))

    def step(s, carry):
        base = s * 2 * PEER_GROUP
        _gather_group(idx_ref, tab_ref, set_b, [base + PEER_GROUP + i for i in range(PEER_GROUP)])
        for i in range(PEER_GROUP):
            compute(set_a[i], base + i)
        for i in range(PEER_GROUP):
            compute(set_b[i], base + PEER_GROUP + i)
        _gather_group(idx_ref, tab_ref, set_a,
                      [jnp.minimum(base + 2 * PEER_GROUP + i, TOKEN_TILE - 1) for i in range(PEER_GROUP)])
        return carry

    lax.fori_loop(0, TOKEN_TILE // (2 * PEER_GROUP), step, 0)


def _peer_u_body(idx_ref, h_ref, gate_ref, tab_ref, act_ref, *tiles):
    def compute(tile, t):
        hb = jnp.broadcast_to(h_ref[pl.ds(t, 1), :], (8, D_MODEL)).astype(BF)
        s = lax.dot_general(hb, _unpack_rows(tile), (((1,), (1,)), ((), ())), preferred_element_type=F32)
        act_ref[pl.ds(t, 1), :] = s[0:1]

    _for_each_token(idx_ref, tab_ref, tiles, compute)
    d = act_ref[...]
    act_ref[...] = 0.5 * d * (1.0 + lax.erf(d * (2.0 ** -0.5))) * gate_ref[...]


def _peer_v_body(idx_ref, act_ref, x_ref, gate_ref, tab_ref, out_ref, *tiles):
    def compute(tile, t):
        ab = jnp.broadcast_to(act_ref[pl.ds(t, 1), :], (8, PEER_SEL)).astype(BF)
        o = jnp.dot(ab, _unpack_rows(tile), preferred_element_type=F32)
        out_ref[pl.ds(t, 1), :] = o[0:1]

    _for_each_token(idx_ref, tab_ref, tiles, compute)
    out_ref[...] = x_ref[...] + gate_ref[0, 0] * out_ref[...]


def peer_experts(x, h, idx_words, idxt_words, gate, mod_gate, u_pack, v_pack, tiles_per_seq, ctx_tiles):
    n = h.shape[0]
    grid = (n // TOKEN_TILE,)
    tiles = [pltpu.VMEM((ROW_WORDS * TILE_STRIDE, LANES), jnp.int32)] * (2 * PEER_GROUP)
    params = pltpu.CompilerParams(vmem_limit_bytes=PEER_VMEM_BYTES)
    idx_spec = pl.BlockSpec((TOKEN_TILE, PEER_SEL), lambda i: (i, 0), memory_space=pltpu.SMEM)
    sel_spec = _tok_spec(PEER_SEL)
    act = pl.pallas_call(
        _peer_u_body, grid=grid,
        in_specs=[idx_spec, _tok_spec(D_MODEL), sel_spec, _RESIDENT],
        out_specs=sel_spec,
        out_shape=jax.ShapeDtypeStruct((n, PEER_SEL), F32),
        scratch_shapes=tiles, compiler_params=params, name="peer_u",
    )(idx_words, h, gate, u_pack)
    return pl.pallas_call(
        _peer_v_body, grid=grid,
        in_specs=[_SEL_MAJOR_SPEC, sel_spec, _tok_spec(D_MODEL), _mod_spec(tiles_per_seq, ctx_tiles), _RESIDENT],
        out_specs=_tok_spec(D_MODEL),
        out_shape=jax.ShapeDtypeStruct((n, D_MODEL), F32),
        scratch_shapes=tiles, compiler_params=params, name="peer_v",
    )(idxt_words, act, x, mod_gate, v_pack)


def peer_ffn(x, mods, norm_w, wq, keys, u_tab, v_tab, tiles_per_seq, ctx_tiles):
    h, idx_words, idxt_words, gate = peer_route(x, mods[3], mods[4], norm_w, wq, keys, tiles_per_seq, ctx_tiles)
    return peer_experts(x, h, idx_words, idxt_words, gate, mods[5], pack_table(u_tab), pack_table(v_tab),
                        tiles_per_seq, ctx_tiles)


def _final_norm_body(x_ref, w_ref, o_ref):
    xf = x_ref[...]
    o_ref[...] = xf * lax.rsqrt(jnp.mean(xf * xf, axis=-1, keepdims=True) + NORM_EPS) * w_ref[...]


def final_rmsnorm(x, w):
    n = x.shape[0]
    return pl.pallas_call(
        _final_norm_body, grid=(n // TOKEN_TILE,),
        in_specs=[_tok_spec(D_MODEL), _row_spec(D_MODEL)],
        out_specs=_tok_spec(D_MODEL),
        out_shape=jax.ShapeDtypeStruct((n, D_MODEL), F32),
        name="final_rmsnorm",
    )(x, w.reshape(1, D_MODEL))


def even_layer(stream, mods, n_seq, tiles_per_seq, ctx_tiles, norm_w, w_in, w_out, sc_conv_w, ssd_conv_w,
               ssd_conv_b, dt_bias, a_log, d_skip, ssd_norm_w):
    ya, z, xs, bm, cm, dt_raw = even_in(stream, mods[0], mods[1], norm_w, w_in, sc_conv_w, ssd_conv_w, ssd_conv_b,
                                        tiles_per_seq, ctx_tiles)
    ys = ssd_scan(xs, bm, cm, dt_raw, dt_bias, -jnp.exp(a_log), n_seq, ctx_tiles * TOKEN_TILE // SSD_CHUNK)
    return even_out(stream, mods[2], ya, z, xs, ys, d_skip, ssd_norm_w, w_out, tiles_per_seq, ctx_tiles)


def odd_layer(stream, mods, n_seq, tiles_per_seq, ctx_tiles, norm_w, w_in, w_out, lower_bound, hg_norm_w):
    q, f_raw, v, g = odd_in(stream, mods[0], mods[1], norm_w, w_in, tiles_per_seq, ctx_tiles)
    os_ = gla_scan(q, f_raw, v, lower_bound.reshape(2, 1, HG_WIDTH), n_seq, ctx_tiles * TOKEN_TILE // GLA_CHUNK)
    return odd_out(stream, mods[2], os_, g, hg_norm_w, w_out, tiles_per_seq, ctx_tiles)


def kernel(x, c, ctx, c_ctx, ada_w, ada_b, norm_mix_w, norm_ffn_w, norm_f_w, ev_w_in, ev_w_out,
           sc_conv_w, ssd_conv_w, ssd_conv_b, ssd_dt_bias, ssd_a_log, ssd_d, ssd_norm_w,
           od_w_in, od_w_out, hg_lb_logits, hg_norm_w, peer_wq, peer_keys, peer_u, peer_v):
    Bsz, seq, D = x.shape
    ctx_len = ctx.shape[1]
    assert D == D_MODEL and ctx_len == TOKEN_TILE and seq % TOKEN_TILE == 0 and TOKEN_TILE % GRID_W == 0
    lb_sm = jax.nn.softmax(hg_lb_logits.astype(F32), axis=0)
    lower_bounds = jnp.cumsum(lb_sm, axis=0) - lb_sm[0]
    cond = jnp.concatenate([c, c_ctx[None, :], jnp.zeros((7 - Bsz % 8, D), F32)], axis=0)

    ctx_tiles = ctx_len // TOKEN_TILE
    tiles_per_seq = ctx_tiles + seq // TOKEN_TILE
    stream = jnp.concatenate([ctx, x], axis=1).reshape(Bsz * (ctx_len + seq), D)
    for layer in range(DEPTH):
        last = layer == DEPTH - 1
        j = layer // 2
        m = ada_modulation(cond, ada_w[layer], ada_b[layer]).reshape(-1, 6, D)
        mods = [jnp.stack([jnp.broadcast_to(m[Bsz, k], (Bsz, D)), m[:Bsz, k]], axis=1)[:, :, None, :] for k in range(6)]
        if layer % 2 == 0:
            stream = even_layer(stream, mods, Bsz, tiles_per_seq, ctx_tiles, norm_mix_w[layer], ev_w_in[j], ev_w_out[j],
                                sc_conv_w[j], ssd_conv_w[j], ssd_conv_b[j], ssd_dt_bias[j], ssd_a_log[j], ssd_d[j],
                                ssd_norm_w[j])
        else:
            stream = odd_layer(stream, mods, Bsz, tiles_per_seq, ctx_tiles, norm_mix_w[layer], od_w_in[j], od_w_out[j],
                               lower_bounds[layer], hg_norm_w[j])
        if last:
            stream = stream.reshape(Bsz, ctx_len + seq, D)[:, ctx_len:].reshape(Bsz * seq, D)
            tiles_per_seq, ctx_tiles = seq // TOKEN_TILE, 0
        stream = peer_ffn(stream, mods, norm_ffn_w[layer], peer_wq[layer], peer_keys[layer], peer_u[layer],
                          peer_v[layer], tiles_per_seq, ctx_tiles)
    return final_rmsnorm(stream, norm_f_w).reshape(Bsz, seq, D)
```

```python
import functools

import jax
import jax.numpy as jnp
from jax import lax
from jax.experimental import pallas as pl
from jax.experimental.pallas import tpu as pltpu

D_MODEL = 1024
DEPTH = 2
GRID_W = 64
CTX_LEN = 256
NORM_EPS = 1e-6
CONV_W = 3
SC_WIDTH = 1024
SSD_HEADS = 16
SSD_HEAD_DIM = 64
SSD_WIDTH = SSD_HEADS * SSD_HEAD_DIM
SSD_GROUPS = 4
SSD_STATE = 128
SSD_BC = SSD_GROUPS * SSD_STATE
SSD_CONV_DIM = SSD_WIDTH + 2 * SSD_BC
EVEN_MIX = SC_WIDTH + SSD_WIDTH
HG_HEADS = 8
HG_KDIM = 128
HG_VDIM = 128
HG_WIDTH = HG_HEADS * HG_KDIM
PEER_HEADS = 8
PEER_NKEYS = 128
PEER_QDIM = 256
PEER_HALF = PEER_QDIM // 2
PEER_TOPK = 16
PEER_SEL = PEER_HEADS * PEER_TOPK
PEER_QW = PEER_HEADS * PEER_QDIM

LANES = 128
TOKEN_TILE = 256
SSD_CHUNK = 128
GLA_CHUNK = 64
ROW_WORDS = D_MODEL // (2 * LANES)
TILE_STRIDE = PEER_SEL + 8
PEER_U_GROUP = 16
PEER_V_GROUP = 8
MIB = 1024 * 1024
PEER_VMEM_BYTES = 48 * MIB
MIX_VMEM_BYTES = 48 * MIB
HI = lax.Precision.HIGHEST
BF = jnp.bfloat16
F32 = jnp.float32


def _mod_spec(tiles_per_seq, ctx_tiles):
    return pl.BlockSpec((1, 1, 1, D_MODEL),
                        lambda i: (i // tiles_per_seq, jnp.where(i % tiles_per_seq < ctx_tiles, 0, 1), 0, 0))


def _tok_spec(width):
    return pl.BlockSpec((TOKEN_TILE, width), lambda i: (i, 0))


def _row_spec(width):
    return pl.BlockSpec((1, width), lambda i: (0, 0))


_RESIDENT = pl.BlockSpec(memory_space=pltpu.VMEM)
_SEL_MAJOR_SPEC = pl.BlockSpec((PEER_SEL, TOKEN_TILE), lambda i: (0, i))


def _norm_mod(x, nw, shift, scale):
    y = x * lax.rsqrt(jnp.mean(x * x, axis=-1, keepdims=True) + NORM_EPS) * nw
    return y * (1.0 + scale) + shift


def _silu(x):
    return x * jax.nn.sigmoid(x)


def _select_dot(a, b, select_lhs):
    sel, x = (a, b) if select_lhs else (b, a)
    sel = sel.astype(F32).astype(BF)
    hi = x.astype(BF)
    rest = x - hi.astype(F32)
    mid = rest.astype(BF)
    parts = (hi, mid, (rest - mid.astype(F32)).astype(BF))
    if select_lhs:
        return sum(jnp.dot(sel, p, preferred_element_type=F32) for p in parts)
    return sum(jnp.dot(p, sel, preferred_element_type=F32) for p in parts)


def _ada_body(c_ref, w_ref, b_ref, o_ref):
    cond = _silu(c_ref[...]).astype(BF)
    o_ref[...] = jnp.dot(cond, w_ref[...].astype(BF), preferred_element_type=F32) + b_ref[...]


def ada_modulation(cond, w, b):
    rows = cond.shape[0]
    return pl.pallas_call(
        _ada_body, grid=(6,),
        in_specs=[pl.BlockSpec((rows, D_MODEL), lambda j: (0, 0)),
                  pl.BlockSpec((D_MODEL, D_MODEL), lambda j: (0, j)),
                  pl.BlockSpec((1, D_MODEL), lambda j: (0, j))],
        out_specs=pl.BlockSpec((rows, D_MODEL), lambda j: (0, j)),
        out_shape=jax.ShapeDtypeStruct((rows, 6 * D_MODEL), F32),
        name="ada_modulation",
    )(cond, w, b.reshape(1, 6 * D_MODEL))


EVEN_W_COLS = 3 * SC_WIDTH + SSD_WIDTH + SSD_CONV_DIM + 2 * LANES


def _even_in_body(x_ref, shift_ref, scale_ref, nw_ref, w_ref, cw_sc_ref, cw_ssd_ref, cb_ssd_ref,
                  ya_ref, z_ref, xs_ref, bm_ref, cm_ref, dt_ref, *, tiles_per_seq, ctx_tiles):
    T = TOKEN_TILE
    h = _norm_mod(x_ref[...], nw_ref[...], shift_ref[0, 0], scale_ref[0, 0]).astype(BF)
    is_ctx = (pl.program_id(0) % tiles_per_seq) < ctx_tiles
    row_len = jnp.where(is_ctx, T, GRID_W)
    pos = lax.broadcasted_iota(jnp.int32, (T, 1), 0) & (row_len - 1)
    first, last = pos == 0, pos == row_len - 1

    def conv(u, cw_ref):
        prev = jnp.where(first, 0.0, pltpu.roll(u, 1, 0))
        nxt = jnp.where(last, 0.0, pltpu.roll(u, T - 1, 0))
        return prev * cw_ref[0:1, :] + u * cw_ref[1:2, :] + nxt * cw_ref[2:3, :]

    def proj(lo, width):
        return jnp.dot(h, w_ref[:, lo:lo + width], preferred_element_type=F32)

    sc_x, sc_b, sc_c = proj(0, SC_WIDTH), proj(SC_WIDTH, SC_WIDTH), proj(2 * SC_WIDTH, SC_WIDTH)
    ya_ref[...] = sc_b * conv(sc_c * sc_x, cw_sc_ref)
    z_ref[...] = proj(3 * SC_WIDTH, SSD_WIDTH)
    xbc = _silu(conv(proj(3 * SC_WIDTH + SSD_WIDTH, SSD_CONV_DIM), cw_ssd_ref) + cb_ssd_ref[...])
    xs_ref[...] = xbc[:, :SSD_WIDTH]
    bm_ref[...] = xbc[:, SSD_WIDTH:SSD_WIDTH + SSD_BC]
    cm_ref[...] = xbc[:, SSD_WIDTH + SSD_BC:]
    dt_ref[...] = proj(3 * SC_WIDTH + SSD_WIDTH + SSD_CONV_DIM, 2 * LANES)


def even_in(x, mod_shift, mod_scale, norm_w, w_in, sc_conv_w, ssd_conv_w, ssd_conv_b, tiles_per_seq, ctx_tiles):
    n = x.shape[0]
    main = 3 * SC_WIDTH + SSD_WIDTH + SSD_CONV_DIM
    w_dt = jnp.zeros((D_MODEL, 2, LANES), F32).at[:, :, :SSD_HEADS].set(w_in[:, main:].reshape(D_MODEL, 2, SSD_HEADS))
    w = jnp.concatenate([w_in[:, :main], w_dt.reshape(D_MODEL, 2 * LANES)], axis=1).astype(BF)
    mod = _mod_spec(tiles_per_seq, ctx_tiles)
    widths = (SC_WIDTH, SSD_WIDTH, SSD_WIDTH, SSD_BC, SSD_BC, 2 * LANES)
    return pl.pallas_call(
        functools.partial(_even_in_body, tiles_per_seq=tiles_per_seq, ctx_tiles=ctx_tiles),
        grid=(n // TOKEN_TILE,),
        in_specs=[_tok_spec(D_MODEL), mod, mod, _row_spec(D_MODEL), _RESIDENT,
                  pl.BlockSpec((CONV_W, SC_WIDTH), lambda i: (0, 0)),
                  pl.BlockSpec((CONV_W, SSD_CONV_DIM), lambda i: (0, 0)), _row_spec(SSD_CONV_DIM)],
        out_specs=[_tok_spec(wd) for wd in widths],
        out_shape=[jax.ShapeDtypeStruct((n, wd), F32) for wd in widths],
        compiler_params=pltpu.CompilerParams(vmem_limit_bytes=MIX_VMEM_BYTES),
        name="even_in",
    )(x, mod_shift, mod_scale, norm_w.reshape(1, D_MODEL), w, sc_conv_w, ssd_conv_w,
      ssd_conv_b.reshape(1, SSD_CONV_DIM))


def _scan_row(n_chunks, ctx_chunks):
    def row(b, d, c):
        back = jnp.where(c < ctx_chunks, ctx_chunks - 1 - c, n_chunks - 1 + ctx_chunks - c)
        return b * n_chunks + jnp.where(d == 0, c, back)
    return row


def _causal_mask(L, d):
    r = lax.broadcasted_iota(jnp.int32, (L, L), 0)
    c = lax.broadcasted_iota(jnp.int32, (L, L), 1)
    return jnp.where(d == 0, c - r, r - c) <= 0


SSD_GW = (SSD_HEADS // SSD_GROUPS) * SSD_HEAD_DIM


def _ssd_body(xs_ref, bm_ref, cm_ref, dt_ref, bias_ref, a_ref, y_ref, state):
    d = pl.program_id(1)
    L = SSD_CHUNK

    @pl.when(pl.program_id(2) == 0)
    def _():
        state[...] = jnp.zeros_like(state)

    mask = _causal_mask(L, d)
    v = dt_ref[...] + bias_ref[0]
    dt = jnp.maximum(v, 0.0) + jnp.log1p(jnp.exp(-jnp.abs(v)))
    la = _select_dot(mask, dt * a_ref[0], True)
    la_t = la.T
    head_of_lane = lax.broadcasted_iota(jnp.int32, (LANES, SSD_WIDTH), 1) // SSD_HEAD_DIM
    expand = lax.broadcasted_iota(jnp.int32, (LANES, SSD_WIDTH), 0) == head_of_lane
    dt_x = _select_dot(dt, expand, False)
    la_x = _select_dot(la, expand, False)
    end_x = jnp.where(d == 0, la_x[L - 1:L], la_x[0:1])
    xdt = xs_ref[...] * dt_x
    w_state = (xdt * jnp.exp(end_x - la_x)).astype(BF)
    xdt_b = xdt.astype(BF)
    e_la = jnp.exp(la_x)
    dec = jnp.exp(end_x)
    for g in range(SSD_GROUPS):
        bm = bm_ref[:, g * SSD_STATE:(g + 1) * SSD_STATE].astype(BF)
        cm = cm_ref[:, g * SSD_STATE:(g + 1) * SSD_STATE].astype(BF)
        cb = lax.dot_general(cm, bm, (((1,), (1,)), ((), ())), preferred_element_type=F32)
        lanes = slice(g * SSD_GW, (g + 1) * SSD_GW)
        s_old = state[g]
        y_in = jnp.dot(cm, s_old.astype(BF), preferred_element_type=F32) * e_la[:, lanes]
        ys = []
        for j in range(SSD_HEADS // SSD_GROUPS):
            h = g * (SSD_HEADS // SSD_GROUPS) + j
            seg = la[:, h:h + 1] - la_t[h:h + 1, :]
            m = (cb * jnp.exp(jnp.where(mask, seg, -jnp.inf))).astype(BF)
            ys.append(jnp.dot(m, xdt_b[:, h * SSD_HEAD_DIM:(h + 1) * SSD_HEAD_DIM], preferred_element_type=F32))
        y_ref[0, :, lanes] = y_in + jnp.concatenate(ys, axis=1)
        state[g] = dec[:, lanes] * s_old + jnp.dot(bm.T, w_state[:, lanes], preferred_element_type=F32)


def ssd_scan(xs, bm, cm, dt_raw, dt_bias, a, n_seq, ctx_chunks):
    n = xs.shape[0]
    n_chunks = n // n_seq // SSD_CHUNK
    row = _scan_row(n_chunks, ctx_chunks)
    pad = lambda p: jnp.zeros((2, 1, LANES), F32).at[:, 0, :SSD_HEADS].set(p)
    tok = lambda width: pl.BlockSpec((SSD_CHUNK, width), lambda b, d, c: (row(b, d, c), 0))
    per_dir = pl.BlockSpec((1, 1, LANES), lambda b, d, c: (d, 0, 0))
    return pl.pallas_call(
        _ssd_body, grid=(n_seq, 2, n_chunks),
        in_specs=[tok(SSD_WIDTH), tok(SSD_BC), tok(SSD_BC),
                  pl.BlockSpec((SSD_CHUNK, LANES), lambda b, d, c: (row(b, d, c), d)), per_dir, per_dir],
        out_specs=pl.BlockSpec((1, SSD_CHUNK, SSD_WIDTH), lambda b, d, c: (d, row(b, d, c), 0)),
        out_shape=jax.ShapeDtypeStruct((2, n, SSD_WIDTH), F32),
        scratch_shapes=[pltpu.VMEM((SSD_GROUPS, SSD_STATE, SSD_GW), F32)],
        compiler_params=pltpu.CompilerParams(dimension_semantics=("arbitrary", "arbitrary", "arbitrary")),
        name="ssd_scan",
    )(xs, bm, cm, dt_raw, pad(dt_bias), pad(a))


def _gla_body(q_ref, f_ref, v_ref, lb_ref, o_ref, state):
    d = pl.program_id(1)
    L = GLA_CHUNK

    @pl.when(pl.program_id(2) == 0)
    def _():
        state[...] = jnp.zeros_like(state)

    mask = _causal_mask(L, d)
    lb = lb_ref[0]
    f = lb + (1.0 - lb) * jax.nn.sigmoid(f_ref[0])
    gc = _select_dot(mask, jnp.log(f), True)
    g_end = jnp.where(d == 0, gc[L - 1:L], gc[0:1])
    k = 1.0 - f
    qg = (q_ref[...] * jnp.exp(gc)).astype(BF)
    kg = (k * jnp.exp(-gc)).astype(BF)
    k_end = (k * jnp.exp(g_end - gc)).astype(BF)
    dec = jnp.exp(g_end)
    vb = v_ref[...].astype(BF)
    for h in range(HG_HEADS):
        lanes = slice(h * HG_KDIM, (h + 1) * HG_KDIM)
        att = lax.dot_general(qg[:, lanes], kg[:, lanes], (((1,), (1,)), ((), ())), preferred_element_type=F32)
        att = jnp.where(mask, att, 0.0).astype(BF)
        s_old = state[h]
        o = jnp.dot(att, vb[:, lanes], preferred_element_type=F32)
        o_ref[0, :, lanes] = o + lax.dot_general(qg[:, lanes], s_old.astype(BF), (((1,), (1,)), ((), ())),
                                                 preferred_element_type=F32)
        state[h] = s_old * dec[:, lanes] + jnp.dot(vb[:, lanes].T, k_end[:, lanes], preferred_element_type=F32)


def gla_scan(q, f_raw, v, lb, n_seq, ctx_chunks):
    n = q.shape[0]
    n_chunks = n // n_seq // GLA_CHUNK
    row = _scan_row(n_chunks, ctx_chunks)
    tok = pl.BlockSpec((GLA_CHUNK, HG_WIDTH), lambda b, d, c: (row(b, d, c), 0))
    dtok = pl.BlockSpec((1, GLA_CHUNK, HG_WIDTH), lambda b, d, c: (d, row(b, d, c), 0))
    return pl.pallas_call(
        _gla_body, grid=(n_seq, 2, n_chunks),
        in_specs=[tok, dtok, tok, pl.BlockSpec((1, 1, HG_WIDTH), lambda b, d, c: (d, 0, 0))],
        out_specs=dtok,
        out_shape=jax.ShapeDtypeStruct((2, n, HG_WIDTH), F32),
        scratch_shapes=[pltpu.VMEM((HG_HEADS, HG_VDIM, HG_KDIM), F32)],
        compiler_params=pltpu.CompilerParams(dimension_semantics=("arbitrary", "arbitrary", "arbitrary")),
        name="gla_scan",
    )(q, f_raw, v, lb)


def _group_rmsnorm(y, groups):
    width = y.shape[1] // groups
    parts = []
    for g in range(groups):
        seg = y[:, g * width:(g + 1) * width]
        parts.append(seg * lax.rsqrt(jnp.mean(seg * seg, axis=-1, keepdims=True) + NORM_EPS))
    return jnp.concatenate(parts, axis=1)


def _even_out_body(x_ref, gate_ref, ya_ref, z_ref, xs_ref, y0_ref, y1_ref, dskip_ref, nw_ref, w_ref, o_ref):
    yb = (y0_ref[0] + y1_ref[0] + dskip_ref[...] * xs_ref[...]) * _silu(z_ref[...])
    yb = _group_rmsnorm(yb, SSD_GROUPS) * nw_ref[...]
    mix = jnp.concatenate([ya_ref[...], yb], axis=1).astype(BF)
    o_ref[...] = x_ref[...] + gate_ref[0, 0] * jnp.dot(mix, w_ref[...], preferred_element_type=F32)


def even_out(x, mod_gate, ya, z, xs, ys, d_skip, norm_w, w_out, tiles_per_seq, ctx_tiles):
    n = x.shape[0]
    ydir = lambda d: pl.BlockSpec((1, TOKEN_TILE, SSD_WIDTH), lambda i: (d, i, 0))
    return pl.pallas_call(
        _even_out_body, grid=(n // TOKEN_TILE,),
        in_specs=[_tok_spec(D_MODEL), _mod_spec(tiles_per_seq, ctx_tiles), _tok_spec(SC_WIDTH), _tok_spec(SSD_WIDTH),
                  _tok_spec(SSD_WIDTH), ydir(0), ydir(1), _row_spec(SSD_WIDTH), _row_spec(SSD_WIDTH), _RESIDENT],
        out_specs=_tok_spec(D_MODEL),
        out_shape=jax.ShapeDtypeStruct((n, D_MODEL), F32),
        compiler_params=pltpu.CompilerParams(vmem_limit_bytes=MIX_VMEM_BYTES),
        name="even_out",
    )(x, mod_gate, ya, z, xs, ys, ys, jnp.repeat(d_skip, SSD_HEAD_DIM).reshape(1, SSD_WIDTH),
      norm_w.reshape(1, SSD_WIDTH), w_out.astype(BF))


def _odd_in_body(x_ref, shift_ref, scale_ref, nw_ref, w_ref, q_ref, f_ref, v_ref, g_ref):
    h = _norm_mod(x_ref[...], nw_ref[...], shift_ref[0, 0], scale_ref[0, 0]).astype(BF)
    proj = lambda j: jnp.dot(h, w_ref[:, j * HG_WIDTH:(j + 1) * HG_WIDTH], preferred_element_type=F32)
    q_ref[...] = _silu(proj(0)) * (HG_KDIM ** -0.5)
    f_ref[0] = proj(1)
    f_ref[1] = proj(2)
    v_ref[...] = proj(3)
    g_ref[...] = proj(4)


def odd_in(x, mod_shift, mod_scale, norm_w, w_in, tiles_per_seq, ctx_tiles):
    n = x.shape[0]
    mod = _mod_spec(tiles_per_seq, ctx_tiles)
    tok = _tok_spec(HG_WIDTH)
    sds = jax.ShapeDtypeStruct((n, HG_WIDTH), F32)
    return pl.pallas_call(
        _odd_in_body, grid=(n // TOKEN_TILE,),
        in_specs=[_tok_spec(D_MODEL), mod, mod, _row_spec(D_MODEL), _RESIDENT],
        out_specs=[tok, pl.BlockSpec((2, TOKEN_TILE, HG_WIDTH), lambda i: (0, i, 0)), tok, tok],
        out_shape=[sds, jax.ShapeDtypeStruct((2, n, HG_WIDTH), F32), sds, sds],
        compiler_params=pltpu.CompilerParams(vmem_limit_bytes=MIX_VMEM_BYTES),
        name="odd_in",
    )(x, mod_shift, mod_scale, norm_w.reshape(1, D_MODEL), w_in.astype(BF))


def _odd_out_body(x_ref, gate_ref, o0_ref, o1_ref, g_ref, nw_ref, w_ref, out_ref):
    o = _group_rmsnorm(o0_ref[0] + o1_ref[0], HG_HEADS) * nw_ref[...] * _silu(g_ref[...])
    out_ref[...] = x_ref[...] + gate_ref[0, 0] * jnp.dot(o.astype(BF), w_ref[...], preferred_element_type=F32)


def odd_out(x, mod_gate, os_, g, norm_w, w_out, tiles_per_seq, ctx_tiles):
    n = x.shape[0]
    odir = lambda d: pl.BlockSpec((1, TOKEN_TILE, HG_WIDTH), lambda i: (d, i, 0))
    return pl.pallas_call(
        _odd_out_body, grid=(n // TOKEN_TILE,),
        in_specs=[_tok_spec(D_MODEL), _mod_spec(tiles_per_seq, ctx_tiles), odir(0), odir(1), _tok_spec(HG_WIDTH),
                  _row_spec(HG_WIDTH), _RESIDENT],
        out_specs=_tok_spec(D_MODEL),
        out_shape=jax.ShapeDtypeStruct((n, D_MODEL), F32),
        compiler_params=pltpu.CompilerParams(vmem_limit_bytes=MIX_VMEM_BYTES),
        name="odd_out",
    )(x, mod_gate, os_, os_, g, norm_w.reshape(1, HG_WIDTH), w_out.astype(BF))


def _top16(vals, ids, n_ids):
    top_s, top_i = [], []
    for r in range(PEER_TOPK):
        m = jnp.max(vals, axis=0, keepdims=True)
        pos = jnp.min(jnp.where(vals == m, ids, float(n_ids)), axis=0, keepdims=True)
        if r + 1 < PEER_TOPK:
            vals = jnp.where(ids == pos, -jnp.inf, vals)
        top_s.append(m)
        top_i.append(pos)
    return jnp.concatenate(top_s, axis=0), jnp.concatenate(top_i, axis=0).astype(jnp.int32)


def _pick_row(table, row):
    r = lax.broadcasted_iota(jnp.int32, table.shape, 0)
    return jnp.sum(jnp.where(r == row, table, 0), axis=0, keepdims=True)


def _route_body(x_ref, shift_ref, scale_ref, nw_ref, wq_ref, keys_ref, h_ref, idx_ref, idxt_ref, gate_ref,
                q_scr, idx_scr, gate_scr):
    h = _norm_mod(x_ref[...], nw_ref[...], shift_ref[0, 0], scale_ref[0, 0])
    h_ref[...] = h
    q_scr[...] = jnp.dot(h.astype(BF), wq_ref[...], preferred_element_type=F32)

    def route(hd, tok0):
        tops = []
        for a in range(2):
            col = pl.multiple_of(hd * PEER_QDIM + a * PEER_HALF, PEER_HALF)
            qs = q_scr[pl.ds(tok0, LANES), pl.ds(col, PEER_HALF)].astype(BF)
            sc = lax.dot_general(keys_ref[hd * 2 + a], qs, (((1,), (1,)), ((), ())),
                                 preferred_element_type=F32)
            tops.append(_top16(sc, lax.broadcasted_iota(jnp.int32, sc.shape, 0).astype(F32), PEER_NKEYS))
        (s0, i0), (s1, i1) = tops
        sub = lax.broadcasted_iota(jnp.int32, (8, LANES), 0).astype(F32)
        chunks = [(s0[0:8] + s1[0:1], sub * PEER_TOPK), (s0[8:16] + s1[0:1], (sub + 8) * PEER_TOPK)]
        chunks += [(s0[0:8] + s1[b:b + 1], sub * PEER_TOPK + b) for b in range(1, 8)]
        chunks.append((s0[0:1] + s1[8:16], sub + 8))
        best, pos = _top16(jnp.concatenate([c[0] for c in chunks], axis=0),
                           jnp.concatenate([c[1] for c in chunks], axis=0), PEER_TOPK * PEER_TOPK)
        ids = jnp.concatenate(
            [_pick_row(i0, pos[r:r + 1] >> 4) * PEER_NKEYS + _pick_row(i1, pos[r:r + 1] & (PEER_TOPK - 1))
             for r in range(PEER_TOPK)], axis=0)
        e = jnp.exp(best - best[0:1])
        r0 = pl.multiple_of(hd * PEER_TOPK, PEER_TOPK)
        idx_scr[pl.ds(r0, PEER_TOPK), pl.ds(tok0, LANES)] = ids * ROW_WORDS
        gate_scr[pl.ds(r0, PEER_TOPK), pl.ds(tok0, LANES)] = e / jnp.sum(e, axis=0, keepdims=True)

    def head_pair(hp, carry):
        for hh in range(2):
            for lg in range(TOKEN_TILE // LANES):
                route(hp * 2 + hh, lg * LANES)
        return carry

    lax.fori_loop(0, PEER_HEADS // 2, head_pair, 0)
    idx_ref[...] = idx_scr[...].T
    idxt_ref[...] = idx_scr[...]
    gate_ref[...] = gate_scr[...].T


PACK_ROWS = 512


def _pack_body(tab_ref, out_ref):
    bits = pltpu.bitcast(tab_ref[...].astype(BF).astype(F32), jnp.int32)
    words = lax.shift_right_logical(bits[:, :D_MODEL // 2], 16) | bits[:, D_MODEL // 2:]
    for s in range(ROW_WORDS):
        out_ref[pl.ds(s, PACK_ROWS, stride=ROW_WORDS), :] = words[:, s * LANES:(s + 1) * LANES]


def pack_table(tab):
    E = tab.shape[0]
    return pl.pallas_call(
        _pack_body, grid=(E // PACK_ROWS,),
        in_specs=[pl.BlockSpec((PACK_ROWS, D_MODEL), lambda i: (i, 0))],
        out_specs=pl.BlockSpec((PACK_ROWS * ROW_WORDS, LANES), lambda i: (i, 0)),
        out_shape=jax.ShapeDtypeStruct((E * ROW_WORDS, LANES), jnp.int32),
        name="pack_table",
    )(tab)


def _gather_group(idx_ref, tab_ref, tile_refs, tokens):
    if idx_ref.shape[0] == TOKEN_TILE:
        rows = [idx_ref.at[t] for t in tokens]
        read_id = lambda i, k: rows[i][k]
    else:
        cols = []
        for t in tokens:
            lane0 = pl.multiple_of((t // LANES) * LANES, LANES)
            cols.append(pltpu.roll(idx_ref[:, pl.ds(lane0, LANES)], (LANES - t % LANES) % LANES, 1))

        def read_id(i, k):
            v = cols[i][(k // 8) * 8:(k // 8) * 8 + 8, :]
            return (pltpu.roll(v, 8 - k % 8, 0) if k % 8 else v)[0, 0]

    for k in range(PEER_SEL):
        for i, tile_ref in enumerate(tile_refs):
            e = pl.multiple_of(read_id(i, k), ROW_WORDS)
            tile_ref[pl.ds(k, ROW_WORDS, stride=TILE_STRIDE), :] = tab_ref[pl.ds(e, ROW_WORDS), :]


def _unpack_rows(tile_ref):
    lo, hi = [], []
    for j in range(ROW_WORDS):
        w = tile_ref[j * TILE_STRIDE:j * TILE_STRIDE + PEER_SEL, :]
        lo.append(pltpu.bitcast(w << 16, F32))
        hi.append(pltpu.bitcast(w & jnp.int32(-65536), F32))
    return jnp.concatenate(lo + hi, axis=1).astype(BF)


def _for_each_token(idx_ref, tab_ref, tiles, compute):
    group = len(tiles) // 2
    set_a, set_b = tiles[:group], tiles[group:]
    _gather_group(idx_ref, tab_ref, set_a, list(range(group)))

    def step(s, carry):
        base = s * 2 * group
        _gather_group(idx_ref, tab_ref, set_b, [base + group + i for i in range(group)])
        for i in range(group):
            compute(set_a[i], base + i)
        for i in range(group):
            compute(set_b[i], base + group + i)
        _gather_group(idx_ref, tab_ref, set_a,
                      [jnp.minimum(base + 2 * group + i, TOKEN_TILE - 1) for i in range(group)])
        return carry

    lax.fori_loop(0, TOKEN_TILE // (2 * group), step, 0)


def _peer_u_body(idx_ref, h_ref, gate_ref, tab_ref, act_ref, *tiles):
    def compute(tile, t):
        hb = jnp.broadcast_to(h_ref[pl.ds(t, 1), :], (8, D_MODEL)).astype(BF)
        s = lax.dot_general(hb, _unpack_rows(tile), (((1,), (1,)), ((), ())), preferred_element_type=F32)
        act_ref[pl.ds(t, 1), :] = s[0:1]

    _for_each_token(idx_ref, tab_ref, tiles, compute)
    d = act_ref[...]
    act_ref[...] = 0.5 * d * (1.0 + lax.erf(d * (2.0 ** -0.5))) * gate_ref[...]


def _peer_v_body(idx_ref, act_ref, x_ref, gate_ref, tab_ref, fw_ref, out_ref, *tiles, final_norm):
    def compute(tile, t):
        ab = jnp.broadcast_to(act_ref[pl.ds(t, 1), :], (8, PEER_SEL)).astype(BF)
        o = jnp.dot(ab, _unpack_rows(tile), preferred_element_type=F32)
        out_ref[pl.ds(t, 1), :] = o[0:1]

    _for_each_token(idx_ref, tab_ref, tiles, compute)
    y = x_ref[...] + gate_ref[0, 0] * out_ref[...]
    if final_norm:
        y = y * lax.rsqrt(jnp.mean(y * y, axis=-1, keepdims=True) + NORM_EPS) * fw_ref[...]
    out_ref[...] = y


def peer_ffn(x, x_spec, n, mods, mod_spec, norm_w, wq, keys, u_tab, v_tab, final_norm_w=None):
    grid = (n // TOKEN_TILE,)
    sel_spec = _tok_spec(PEER_SEL)
    h, idx_words, idxt_words, gate = pl.pallas_call(
        _route_body, grid=grid,
        in_specs=[x_spec, mod_spec, mod_spec, _row_spec(D_MODEL),
                  pl.BlockSpec((D_MODEL, PEER_QW), lambda i: (0, 0)),
                  pl.BlockSpec((PEER_HEADS * 2, PEER_NKEYS, PEER_HALF), lambda i: (0, 0, 0))],
        out_specs=[_tok_spec(D_MODEL), sel_spec, _SEL_MAJOR_SPEC, sel_spec],
        out_shape=[jax.ShapeDtypeStruct((n, D_MODEL), F32),
                   jax.ShapeDtypeStruct((n, PEER_SEL), jnp.int32),
                   jax.ShapeDtypeStruct((PEER_SEL, n), jnp.int32),
                   jax.ShapeDtypeStruct((n, PEER_SEL), F32)],
        scratch_shapes=[pltpu.VMEM((TOKEN_TILE, PEER_QW), F32),
                        pltpu.VMEM((PEER_SEL, TOKEN_TILE), jnp.int32),
                        pltpu.VMEM((PEER_SEL, TOKEN_TILE), F32)],
        compiler_params=pltpu.CompilerParams(vmem_limit_bytes=MIX_VMEM_BYTES),
        name="peer_route",
    )(x, mods[3], mods[4], norm_w.reshape(1, D_MODEL), wq.astype(BF),
      keys.reshape(PEER_HEADS * 2, PEER_NKEYS, PEER_HALF).astype(BF))

    tiles = lambda group: [pltpu.VMEM((ROW_WORDS * TILE_STRIDE, LANES), jnp.int32)] * (2 * group)
    params = pltpu.CompilerParams(vmem_limit_bytes=PEER_VMEM_BYTES)
    act = pl.pallas_call(
        _peer_u_body, grid=grid,
        in_specs=[pl.BlockSpec((TOKEN_TILE, PEER_SEL), lambda i: (i, 0), memory_space=pltpu.SMEM),
                  _tok_spec(D_MODEL), sel_spec, _RESIDENT],
        out_specs=sel_spec,
        out_shape=jax.ShapeDtypeStruct((n, PEER_SEL), F32),
        scratch_shapes=tiles(PEER_U_GROUP), compiler_params=params, name="peer_u",
    )(idx_words, h, gate, pack_table(u_tab))
    fw = jnp.ones((D_MODEL,), F32) if final_norm_w is None else final_norm_w
    return pl.pallas_call(
        functools.partial(_peer_v_body, final_norm=final_norm_w is not None), grid=grid,
        in_specs=[_SEL_MAJOR_SPEC, sel_spec, x_spec, mod_spec, _RESIDENT, _row_spec(D_MODEL)],
        out_specs=_tok_spec(D_MODEL),
        out_shape=jax.ShapeDtypeStruct((n, D_MODEL), F32),
        scratch_shapes=tiles(PEER_V_GROUP), compiler_params=params, name="peer_v",
    )(idxt_words, act, x, mods[5], pack_table(v_tab), fw.reshape(1, D_MODEL))


def even_layer(stream, mods, n_seq, tiles_per_seq, ctx_tiles, norm_w, w_in, w_out, sc_conv_w, ssd_conv_w,
               ssd_conv_b, dt_bias, a_log, d_skip, ssd_norm_w):
    ya, z, xs, bm, cm, dt_raw = even_in(stream, mods[0], mods[1], norm_w, w_in, sc_conv_w, ssd_conv_w, ssd_conv_b,
                                        tiles_per_seq, ctx_tiles)
    ys = ssd_scan(xs, bm, cm, dt_raw, dt_bias, -jnp.exp(a_log), n_seq, ctx_tiles * TOKEN_TILE // SSD_CHUNK)
    return even_out(stream, mods[2], ya, z, xs, ys, d_skip, ssd_norm_w, w_out, tiles_per_seq, ctx_tiles)


def odd_layer(stream, mods, n_seq, tiles_per_seq, ctx_tiles, norm_w, w_in, w_out, lower_bound, hg_norm_w):
    q, f_raw, v, g = odd_in(stream, mods[0], mods[1], norm_w, w_in, tiles_per_seq, ctx_tiles)
    os_ = gla_scan(q, f_raw, v, lower_bound.reshape(2, 1, HG_WIDTH), n_seq, ctx_tiles * TOKEN_TILE // GLA_CHUNK)
    return odd_out(stream, mods[2], os_, g, hg_norm_w, w_out, tiles_per_seq, ctx_tiles)


def kernel(x, c, ctx, c_ctx, ada_w, ada_b, norm_mix_w, norm_ffn_w, norm_f_w, ev_w_in, ev_w_out,
           sc_conv_w, ssd_conv_w, ssd_conv_b, ssd_dt_bias, ssd_a_log, ssd_d, ssd_norm_w,
           od_w_in, od_w_out, hg_lb_logits, hg_norm_w, peer_wq, peer_keys, peer_u, peer_v):
    Bsz, seq, D = x.shape
    ctx_len = ctx.shape[1]
    assert D == D_MODEL and ctx_len == TOKEN_TILE and seq % TOKEN_TILE == 0 and TOKEN_TILE % GRID_W == 0
    lb_sm = jax.nn.softmax(hg_lb_logits.astype(F32), axis=0)
    lower_bounds = jnp.cumsum(lb_sm, axis=0) - lb_sm[0]
    cond = jnp.concatenate([c, c_ctx[None, :], jnp.zeros((7 - Bsz % 8, D), F32)], axis=0)

    ctx_tiles = ctx_len // TOKEN_TILE
    tiles_per_seq = ctx_tiles + seq // TOKEN_TILE
    stream = jnp.concatenate([ctx, x], axis=1).reshape(Bsz * (ctx_len + seq), D)
    for layer in range(DEPTH):
        last = layer == DEPTH - 1
        j = layer // 2
        m = ada_modulation(cond, ada_w[layer], ada_b[layer]).reshape(-1, 6, D)
        mods = [jnp.stack([jnp.broadcast_to(m[Bsz, k], (Bsz, D)), m[:Bsz, k]], axis=1)[:, :, None, :] for k in range(6)]
        if layer % 2 == 0:
            stream = even_layer(stream, mods, Bsz, tiles_per_seq, ctx_tiles, norm_mix_w[layer], ev_w_in[j], ev_w_out[j],
                                sc_conv_w[j], ssd_conv_w[j], ssd_conv_b[j], ssd_dt_bias[j], ssd_a_log[j], ssd_d[j],
                                ssd_norm_w[j])
        else:
            stream = odd_layer(stream, mods, Bsz, tiles_per_seq, ctx_tiles, norm_mix_w[layer], od_w_in[j], od_w_out[j],
                               lower_bounds[layer], hg_norm_w[j])
        if not last:
            stream = peer_ffn(stream, _tok_spec(D), stream.shape[0], mods, _mod_spec(tiles_per_seq, ctx_tiles),
                              norm_ffn_w[layer], peer_wq[layer], peer_keys[layer], peer_u[layer], peer_v[layer])
        else:
            lat_tiles = seq // TOKEN_TILE
            lat_spec = pl.BlockSpec((TOKEN_TILE, D),
                                    lambda i: ((i // lat_tiles) * tiles_per_seq + ctx_tiles + i % lat_tiles, 0))
            out = peer_ffn(stream, lat_spec, Bsz * seq, mods, _mod_spec(lat_tiles, 0), norm_ffn_w[layer],
                           peer_wq[layer], peer_keys[layer], peer_u[layer], peer_v[layer], final_norm_w=norm_f_w)
    return out.reshape(Bsz, seq, D)
```

```python
import functools

import jax
import jax.numpy as jnp
from jax import lax
from jax.experimental import pallas as pl
from jax.experimental.pallas import tpu as pltpu

D_MODEL = 1024
DEPTH = 2
GRID_W = 64
CTX_LEN = 256
NORM_EPS = 1e-6
CONV_W = 3
SC_WIDTH = 1024
SSD_HEADS = 16
SSD_HEAD_DIM = 64
SSD_WIDTH = SSD_HEADS * SSD_HEAD_DIM
SSD_GROUPS = 4
SSD_STATE = 128
SSD_BC = SSD_GROUPS * SSD_STATE
SSD_CONV_DIM = SSD_WIDTH + 2 * SSD_BC
EVEN_MIX = SC_WIDTH + SSD_WIDTH
HG_HEADS = 8
HG_KDIM = 128
HG_VDIM = 128
HG_WIDTH = HG_HEADS * HG_KDIM
PEER_HEADS = 8
PEER_NKEYS = 128
PEER_QDIM = 256
PEER_HALF = PEER_QDIM // 2
PEER_TOPK = 16
PEER_SEL = PEER_HEADS * PEER_TOPK
PEER_QW = PEER_HEADS * PEER_QDIM

LANES = 128
TOKEN_TILE = 256
SSD_CHUNK = 128
GLA_CHUNK = 64
ROW_WORDS = D_MODEL // (2 * LANES)
TILE_STRIDE = PEER_SEL + 8
PEER_U_GROUP = 16
PEER_V_GROUP = 8
MIB = 1024 * 1024
PEER_VMEM_BYTES = 48 * MIB
MIX_VMEM_BYTES = 48 * MIB
HI = lax.Precision.HIGHEST
BF = jnp.bfloat16
F32 = jnp.float32


def _mod_spec(tiles_per_seq, ctx_tiles):
    return pl.BlockSpec((1, 1, 1, D_MODEL),
                        lambda i: (i // tiles_per_seq, jnp.where(i % tiles_per_seq < ctx_tiles, 0, 1), 0, 0))


def _tok_spec(width):
    return pl.BlockSpec((TOKEN_TILE, width), lambda i: (i, 0))


def _row_spec(width):
    return pl.BlockSpec((1, width), lambda i: (0, 0))


_RESIDENT = pl.BlockSpec(memory_space=pltpu.VMEM)
_SEL_MAJOR_SPEC = pl.BlockSpec((PEER_SEL, TOKEN_TILE), lambda i: (0, i))


def _norm_mod(x, nw, shift, scale):
    y = x * lax.rsqrt(jnp.mean(x * x, axis=-1, keepdims=True) + NORM_EPS) * nw
    return y * (1.0 + scale) + shift


def _silu(x):
    return x * jax.nn.sigmoid(x)


def _select_dot(a, b, select_lhs):
    sel, x = (a, b) if select_lhs else (b, a)
    sel = sel.astype(F32).astype(BF)
    hi = x.astype(BF)
    rest = x - hi.astype(F32)
    mid = rest.astype(BF)
    parts = (hi, mid, (rest - mid.astype(F32)).astype(BF))
    if select_lhs:
        return sum(jnp.dot(sel, p, preferred_element_type=F32) for p in parts)
    return sum(jnp.dot(p, sel, preferred_element_type=F32) for p in parts)


def _ada_body(c_ref, w_ref, b_ref, o_ref):
    cond = _silu(c_ref[...]).astype(BF)
    o_ref[...] = jnp.dot(cond, w_ref[...].astype(BF), preferred_element_type=F32) + b_ref[...]


def ada_modulation(cond, w, b):
    rows = cond.shape[0]
    return pl.pallas_call(
        _ada_body, grid=(6,),
        in_specs=[pl.BlockSpec((rows, D_MODEL), lambda j: (0, 0)),
                  pl.BlockSpec((D_MODEL, D_MODEL), lambda j: (0, j)),
                  pl.BlockSpec((1, D_MODEL), lambda j: (0, j))],
        out_specs=pl.BlockSpec((rows, D_MODEL), lambda j: (0, j)),
        out_shape=jax.ShapeDtypeStruct((rows, 6 * D_MODEL), F32),
        name="ada_modulation",
    )(cond, w, b.reshape(1, 6 * D_MODEL))


EVEN_W_COLS = 3 * SC_WIDTH + SSD_WIDTH + SSD_CONV_DIM + 2 * LANES


def _even_in_body(x_ref, shift_ref, scale_ref, nw_ref, w_ref, cw_sc_ref, cw_ssd_ref, cb_ssd_ref,
                  ya_ref, z_ref, xs_ref, bm_ref, cm_ref, dt_ref, *, tiles_per_seq, ctx_tiles):
    T = TOKEN_TILE
    h = _norm_mod(x_ref[...], nw_ref[...], shift_ref[0, 0], scale_ref[0, 0]).astype(BF)
    is_ctx = (pl.program_id(0) % tiles_per_seq) < ctx_tiles
    row_len = jnp.where(is_ctx, T, GRID_W)
    pos = lax.broadcasted_iota(jnp.int32, (T, 1), 0) & (row_len - 1)
    first, last = pos == 0, pos == row_len - 1

    def conv(u, cw_ref):
        prev = jnp.where(first, 0.0, pltpu.roll(u, 1, 0))
        nxt = jnp.where(last, 0.0, pltpu.roll(u, T - 1, 0))
        return prev * cw_ref[0:1, :] + u * cw_ref[1:2, :] + nxt * cw_ref[2:3, :]

    def proj(lo, width):
        return jnp.dot(h, w_ref[:, lo:lo + width], preferred_element_type=F32)

    sc_x, sc_b, sc_c = proj(0, SC_WIDTH), proj(SC_WIDTH, SC_WIDTH), proj(2 * SC_WIDTH, SC_WIDTH)
    ya_ref[...] = sc_b * conv(sc_c * sc_x, cw_sc_ref)
    z_ref[...] = proj(3 * SC_WIDTH, SSD_WIDTH)
    xbc = _silu(conv(proj(3 * SC_WIDTH + SSD_WIDTH, SSD_CONV_DIM), cw_ssd_ref) + cb_ssd_ref[...])
    xs_ref[...] = xbc[:, :SSD_WIDTH]
    bm_ref[...] = xbc[:, SSD_WIDTH:SSD_WIDTH + SSD_BC]
    cm_ref[...] = xbc[:, SSD_WIDTH + SSD_BC:]
    dt_ref[...] = proj(3 * SC_WIDTH + SSD_WIDTH + SSD_CONV_DIM, 2 * LANES)


def even_in(x, mod_shift, mod_scale, norm_w, w_in, sc_conv_w, ssd_conv_w, ssd_conv_b, tiles_per_seq, ctx_tiles):
    n = x.shape[0]
    main = 3 * SC_WIDTH + SSD_WIDTH + SSD_CONV_DIM
    w_dt = jnp.zeros((D_MODEL, 2, LANES), F32).at[:, :, :SSD_HEADS].set(w_in[:, main:].reshape(D_MODEL, 2, SSD_HEADS))
    w = jnp.concatenate([w_in[:, :main], w_dt.reshape(D_MODEL, 2 * LANES)], axis=1).astype(BF)
    mod = _mod_spec(tiles_per_seq, ctx_tiles)
    widths = (SC_WIDTH, SSD_WIDTH, SSD_WIDTH, SSD_BC, SSD_BC, 2 * LANES)
    return pl.pallas_call(
        functools.partial(_even_in_body, tiles_per_seq=tiles_per_seq, ctx_tiles=ctx_tiles),
        grid=(n // TOKEN_TILE,),
        in_specs=[_tok_spec(D_MODEL), mod, mod, _row_spec(D_MODEL), _RESIDENT,
                  pl.BlockSpec((CONV_W, SC_WIDTH), lambda i: (0, 0)),
                  pl.BlockSpec((CONV_W, SSD_CONV_DIM), lambda i: (0, 0)), _row_spec(SSD_CONV_DIM)],
        out_specs=[_tok_spec(wd) for wd in widths],
        out_shape=[jax.ShapeDtypeStruct((n, wd), F32) for wd in widths],
        compiler_params=pltpu.CompilerParams(vmem_limit_bytes=MIX_VMEM_BYTES),
        name="even_in",
    )(x, mod_shift, mod_scale, norm_w.reshape(1, D_MODEL), w, sc_conv_w, ssd_conv_w,
      ssd_conv_b.reshape(1, SSD_CONV_DIM))


def _scan_row(n_chunks, ctx_chunks):
    def row(b, d, c):
        if d == 0:
            return b * n_chunks + c
        return b * n_chunks + jnp.where(c < ctx_chunks, ctx_chunks - 1 - c, n_chunks - 1 + ctx_chunks - c)
    return row


def _causal_mask(L, d):
    r = lax.broadcasted_iota(jnp.int32, (L, L), 0)
    c = lax.broadcasted_iota(jnp.int32, (L, L), 1)
    return c <= r if d == 0 else c >= r


SSD_GW = (SSD_HEADS // SSD_GROUPS) * SSD_HEAD_DIM


def _alternate(*phased):
    live = list(phased)
    while live:
        live = [g for g in live if next(g, StopIteration) is not StopIteration]


def _ssd_chunk(d, xs_ref, bm_ref, cm_ref, dt_ref, bias, a, y_ref, state):
    L = SSD_CHUNK
    mask = _causal_mask(L, d)
    v = dt_ref[...] + bias
    dt = jnp.maximum(v, 0.0) + jnp.log1p(jnp.exp(-jnp.abs(v)))
    la = _select_dot(mask, dt * a, True)
    la_t = la.T
    head_of_lane = lax.broadcasted_iota(jnp.int32, (LANES, SSD_WIDTH), 1) // SSD_HEAD_DIM
    expand = lax.broadcasted_iota(jnp.int32, (LANES, SSD_WIDTH), 0) == head_of_lane
    dt_x = _select_dot(dt, expand, False)
    la_x = _select_dot(la, expand, False)
    end_x = la_x[L - 1:L] if d == 0 else la_x[0:1]
    xdt = xs_ref[...] * dt_x
    w_state = (xdt * jnp.exp(end_x - la_x)).astype(BF)
    xdt_b = xdt.astype(BF)
    e_la = jnp.exp(la_x)
    dec = jnp.exp(end_x)
    yield
    for g in range(SSD_GROUPS):
        bm = bm_ref[:, g * SSD_STATE:(g + 1) * SSD_STATE].astype(BF)
        cm = cm_ref[:, g * SSD_STATE:(g + 1) * SSD_STATE].astype(BF)
        cb = lax.dot_general(cm, bm, (((1,), (1,)), ((), ())), preferred_element_type=F32)
        lanes = slice(g * SSD_GW, (g + 1) * SSD_GW)
        s_old = state[g]
        y_in = jnp.dot(cm, s_old.astype(BF), preferred_element_type=F32) * e_la[:, lanes]
        ys = []
        for j in range(SSD_HEADS // SSD_GROUPS):
            h = g * (SSD_HEADS // SSD_GROUPS) + j
            seg = la[:, h:h + 1] - la_t[h:h + 1, :]
            m = (cb * jnp.exp(jnp.where(mask, seg, -jnp.inf))).astype(BF)
            ys.append(jnp.dot(m, xdt_b[:, h * SSD_HEAD_DIM:(h + 1) * SSD_HEAD_DIM], preferred_element_type=F32))
        y_ref[:, lanes] = y_in + jnp.concatenate(ys, axis=1)
        state[g] = dec[:, lanes] * s_old + jnp.dot(bm.T, w_state[:, lanes], preferred_element_type=F32)
        yield


def _ssd_body(xs0_ref, bm0_ref, cm0_ref, dt0_ref, xs1_ref, bm1_ref, cm1_ref, dt1_ref, bias_ref, a_ref,
              y0_ref, y1_ref, state):
    @pl.when(pl.program_id(1) == 0)
    def _():
        state[...] = jnp.zeros_like(state)

    _alternate(_ssd_chunk(0, xs0_ref, bm0_ref, cm0_ref, dt0_ref, bias_ref[0], a_ref[0], y0_ref, state.at[0]),
               _ssd_chunk(1, xs1_ref, bm1_ref, cm1_ref, dt1_ref, bias_ref[1], a_ref[1], y1_ref, state.at[1]))


def ssd_scan(xs, bm, cm, dt_raw, dt_bias, a, n_seq, ctx_chunks):
    n = xs.shape[0]
    n_chunks = n // n_seq // SSD_CHUNK
    row = _scan_row(n_chunks, ctx_chunks)
    pad = lambda p: jnp.zeros((2, 1, LANES), F32).at[:, 0, :SSD_HEADS].set(p)
    tok = lambda d, width: pl.BlockSpec((SSD_CHUNK, width), lambda b, c: (row(b, d, c), 0))
    ins = lambda d: [tok(d, SSD_WIDTH), tok(d, SSD_BC), tok(d, SSD_BC),
                     pl.BlockSpec((SSD_CHUNK, LANES), lambda b, c: (row(b, d, c), d))]
    per_dir = pl.BlockSpec((2, 1, LANES), lambda b, c: (0, 0, 0))
    sds = jax.ShapeDtypeStruct((n, SSD_WIDTH), F32)
    return pl.pallas_call(
        _ssd_body, grid=(n_seq, n_chunks),
        in_specs=ins(0) + ins(1) + [per_dir, per_dir],
        out_specs=[tok(0, SSD_WIDTH), tok(1, SSD_WIDTH)],
        out_shape=[sds, sds],
        scratch_shapes=[pltpu.VMEM((2, SSD_GROUPS, SSD_STATE, SSD_GW), F32)],
        compiler_params=pltpu.CompilerParams(dimension_semantics=("arbitrary", "arbitrary")),
        name="ssd_scan",
    )(xs, bm, cm, dt_raw, xs, bm, cm, dt_raw, pad(dt_bias), pad(a))


def _gla_chunk(d, q_ref, f_ref, v_ref, lb, o_ref, state):
    L = GLA_CHUNK
    mask = _causal_mask(L, d)
    f = lb + (1.0 - lb) * jax.nn.sigmoid(f_ref[0])
    gc = _select_dot(mask, jnp.log(f), True)
    g_end = gc[L - 1:L] if d == 0 else gc[0:1]
    k = 1.0 - f
    qg = (q_ref[...] * jnp.exp(gc)).astype(BF)
    kg = (k * jnp.exp(-gc)).astype(BF)
    k_end = (k * jnp.exp(g_end - gc)).astype(BF)
    dec = jnp.exp(g_end)
    vb = v_ref[...].astype(BF)
    yield
    for h in range(HG_HEADS):
        lanes = slice(h * HG_KDIM, (h + 1) * HG_KDIM)
        att = lax.dot_general(qg[:, lanes], kg[:, lanes], (((1,), (1,)), ((), ())), preferred_element_type=F32)
        att = jnp.where(mask, att, 0.0).astype(BF)
        s_old = state[h]
        o = jnp.dot(att, vb[:, lanes], preferred_element_type=F32)
        o_ref[:, lanes] = o + lax.dot_general(qg[:, lanes], s_old.astype(BF), (((1,), (1,)), ((), ())),
                                              preferred_element_type=F32)
        state[h] = s_old * dec[:, lanes] + jnp.dot(vb[:, lanes].T, k_end[:, lanes], preferred_element_type=F32)
        yield


def _gla_body(q0_ref, f0_ref, v0_ref, q1_ref, f1_ref, v1_ref, lb_ref, o0_ref, o1_ref, state):
    @pl.when(pl.program_id(1) == 0)
    def _():
        state[...] = jnp.zeros_like(state)

    _alternate(_gla_chunk(0, q0_ref, f0_ref, v0_ref, lb_ref[0], o0_ref, state.at[0]),
               _gla_chunk(1, q1_ref, f1_ref, v1_ref, lb_ref[1], o1_ref, state.at[1]))


def gla_scan(q, f_raw, v, lb, n_seq, ctx_chunks):
    n = q.shape[0]
    n_chunks = n // n_seq // GLA_CHUNK
    row = _scan_row(n_chunks, ctx_chunks)
    tok = lambda d: pl.BlockSpec((GLA_CHUNK, HG_WIDTH), lambda b, c: (row(b, d, c), 0))
    ins = lambda d: [tok(d), pl.BlockSpec((1, GLA_CHUNK, HG_WIDTH), lambda b, c: (d, row(b, d, c), 0)), tok(d)]
    sds = jax.ShapeDtypeStruct((n, HG_WIDTH), F32)
    return pl.pallas_call(
        _gla_body, grid=(n_seq, n_chunks),
        in_specs=ins(0) + ins(1) + [pl.BlockSpec((2, 1, HG_WIDTH), lambda b, c: (0, 0, 0))],
        out_specs=[tok(0), tok(1)],
        out_shape=[sds, sds],
        scratch_shapes=[pltpu.VMEM((2, HG_HEADS, HG_VDIM, HG_KDIM), F32)],
        compiler_params=pltpu.CompilerParams(dimension_semantics=("arbitrary", "arbitrary")),
        name="gla_scan",
    )(q, f_raw, v, q, f_raw, v, lb)


def _group_rmsnorm(y, groups):
    width = y.shape[1] // groups
    parts = []
    for g in range(groups):
        seg = y[:, g * width:(g + 1) * width]
        parts.append(seg * lax.rsqrt(jnp.mean(seg * seg, axis=-1, keepdims=True) + NORM_EPS))
    return jnp.concatenate(parts, axis=1)


def _even_out_body(x_ref, gate_ref, ya_ref, z_ref, xs_ref, y0_ref, y1_ref, dskip_ref, nw_ref, w_ref, o_ref):
    yb = (y0_ref[...] + y1_ref[...] + dskip_ref[...] * xs_ref[...]) * _silu(z_ref[...])
    yb = _group_rmsnorm(yb, SSD_GROUPS) * nw_ref[...]
    mix = jnp.concatenate([ya_ref[...], yb], axis=1).astype(BF)
    o_ref[...] = x_ref[...] + gate_ref[0, 0] * jnp.dot(mix, w_ref[...], preferred_element_type=F32)


def even_out(x, mod_gate, ya, z, xs, y0, y1, d_skip, norm_w, w_out, tiles_per_seq, ctx_tiles):
    n = x.shape[0]
    return pl.pallas_call(
        _even_out_body, grid=(n // TOKEN_TILE,),
        in_specs=[_tok_spec(D_MODEL), _mod_spec(tiles_per_seq, ctx_tiles), _tok_spec(SC_WIDTH), _tok_spec(SSD_WIDTH),
                  _tok_spec(SSD_WIDTH), _tok_spec(SSD_WIDTH), _tok_spec(SSD_WIDTH), _row_spec(SSD_WIDTH),
                  _row_spec(SSD_WIDTH), _RESIDENT],
        out_specs=_tok_spec(D_MODEL),
        out_shape=jax.ShapeDtypeStruct((n, D_MODEL), F32),
        compiler_params=pltpu.CompilerParams(vmem_limit_bytes=MIX_VMEM_BYTES),
        name="even_out",
    )(x, mod_gate, ya, z, xs, y0, y1, jnp.repeat(d_skip, SSD_HEAD_DIM).reshape(1, SSD_WIDTH),
      norm_w.reshape(1, SSD_WIDTH), w_out.astype(BF))


def _odd_in_body(x_ref, shift_ref, scale_ref, nw_ref, w_ref, q_ref, f_ref, v_ref, g_ref):
    h = _norm_mod(x_ref[...], nw_ref[...], shift_ref[0, 0], scale_ref[0, 0]).astype(BF)
    proj = lambda j: jnp.dot(h, w_ref[:, j * HG_WIDTH:(j + 1) * HG_WIDTH], preferred_element_type=F32)
    q_ref[...] = _silu(proj(0)) * (HG_KDIM ** -0.5)
    f_ref[0] = proj(1)
    f_ref[1] = proj(2)
    v_ref[...] = proj(3)
    g_ref[...] = proj(4)


def odd_in(x, mod_shift, mod_scale, norm_w, w_in, tiles_per_seq, ctx_tiles):
    n = x.shape[0]
    mod = _mod_spec(tiles_per_seq, ctx_tiles)
    tok = _tok_spec(HG_WIDTH)
    sds = jax.ShapeDtypeStruct((n, HG_WIDTH), F32)
    return pl.pallas_call(
        _odd_in_body, grid=(n // TOKEN_TILE,),
        in_specs=[_tok_spec(D_MODEL), mod, mod, _row_spec(D_MODEL), _RESIDENT],
        out_specs=[tok, pl.BlockSpec((2, TOKEN_TILE, HG_WIDTH), lambda i: (0, i, 0)), tok, tok],
        out_shape=[sds, jax.ShapeDtypeStruct((2, n, HG_WIDTH), F32), sds, sds],
        compiler_params=pltpu.CompilerParams(vmem_limit_bytes=MIX_VMEM_BYTES),
        name="odd_in",
    )(x, mod_shift, mod_scale, norm_w.reshape(1, D_MODEL), w_in.astype(BF))


def _odd_out_body(x_ref, gate_ref, o0_ref, o1_ref, g_ref, nw_ref, w_ref, out_ref):
    o = _group_rmsnorm(o0_ref[...] + o1_ref[...], HG_HEADS) * nw_ref[...] * _silu(g_ref[...])
    out_ref[...] = x_ref[...] + gate_ref[0, 0] * jnp.dot(o.astype(BF), w_ref[...], preferred_element_type=F32)


def odd_out(x, mod_gate, o0, o1, g, norm_w, w_out, tiles_per_seq, ctx_tiles):
    n = x.shape[0]
    return pl.pallas_call(
        _odd_out_body, grid=(n // TOKEN_TILE,),
        in_specs=[_tok_spec(D_MODEL), _mod_spec(tiles_per_seq, ctx_tiles), _tok_spec(HG_WIDTH), _tok_spec(HG_WIDTH),
                  _tok_spec(HG_WIDTH), _row_spec(HG_WIDTH), _RESIDENT],
        out_specs=_tok_spec(D_MODEL),
        out_shape=jax.ShapeDtypeStruct((n, D_MODEL), F32),
        compiler_params=pltpu.CompilerParams(vmem_limit_bytes=MIX_VMEM_BYTES),
        name="odd_out",
    )(x, mod_gate, o0, o1, g, norm_w.reshape(1, HG_WIDTH), w_out.astype(BF))


def _top16(vals, ids, n_ids):
    top_s, top_i = [], []
    for r in range(PEER_TOPK):
        m = jnp.max(vals, axis=0, keepdims=True)
        pos = jnp.min(jnp.where(vals == m, ids, float(n_ids)), axis=0, keepdims=True)
        if r + 1 < PEER_TOPK:
            vals = jnp.where(ids == pos, -jnp.inf, vals)
        top_s.append(m)
        top_i.append(pos)
    return jnp.concatenate(top_s, axis=0), jnp.concatenate(top_i, axis=0).astype(jnp.int32)


def _pick_row(table, row):
    r = lax.broadcasted_iota(jnp.int32, table.shape, 0)
    return jnp.sum(jnp.where(r == row, table, 0), axis=0, keepdims=True)


def _route_body(x_ref, shift_ref, scale_ref, nw_ref, wq_ref, keys_ref, h_ref, idx_ref, idxt_ref, gate_ref,
                q_scr, idx_scr, gate_scr):
    h = _norm_mod(x_ref[...], nw_ref[...], shift_ref[0, 0], scale_ref[0, 0])
    h_ref[...] = h
    q_scr[...] = jnp.dot(h.astype(BF), wq_ref[...], preferred_element_type=F32)

    def route(hd, tok0):
        tops = []
        for a in range(2):
            col = pl.multiple_of(hd * PEER_QDIM + a * PEER_HALF, PEER_HALF)
            qs = q_scr[pl.ds(tok0, LANES), pl.ds(col, PEER_HALF)].astype(BF)
            sc = lax.dot_general(keys_ref[hd * 2 + a], qs, (((1,), (1,)), ((), ())),
                                 preferred_element_type=F32)
            tops.append(_top16(sc, lax.broadcasted_iota(jnp.int32, sc.shape, 0).astype(F32), PEER_NKEYS))
        (s0, i0), (s1, i1) = tops
        sub = lax.broadcasted_iota(jnp.int32, (8, LANES), 0).astype(F32)
        chunks = [(s0[0:8] + s1[0:1], sub * PEER_TOPK), (s0[8:16] + s1[0:1], (sub + 8) * PEER_TOPK)]
        chunks += [(s0[0:8] + s1[b:b + 1], sub * PEER_TOPK + b) for b in range(1, 8)]
        chunks.append((s0[0:1] + s1[8:16], sub + 8))
        best, pos = _top16(jnp.concatenate([c[0] for c in chunks], axis=0),
                           jnp.concatenate([c[1] for c in chunks], axis=0), PEER_TOPK * PEER_TOPK)
        ids = jnp.concatenate(
            [_pick_row(i0, pos[r:r + 1] >> 4) * PEER_NKEYS + _pick_row(i1, pos[r:r + 1] & (PEER_TOPK - 1))
             for r in range(PEER_TOPK)], axis=0)
        e = jnp.exp(best - best[0:1])
        r0 = pl.multiple_of(hd * PEER_TOPK, PEER_TOPK)
        idx_scr[pl.ds(r0, PEER_TOPK), pl.ds(tok0, LANES)] = ids * ROW_WORDS
        gate_scr[pl.ds(r0, PEER_TOPK), pl.ds(tok0, LANES)] = e / jnp.sum(e, axis=0, keepdims=True)

    def head_pair(hp, carry):
        for hh in range(2):
            for lg in range(TOKEN_TILE // LANES):
                route(hp * 2 + hh, lg * LANES)
        return carry

    lax.fori_loop(0, PEER_HEADS // 2, head_pair, 0)
    idx_ref[...] = idx_scr[...].T
    idxt_ref[...] = idx_scr[...]
    gate_ref[...] = gate_scr[...].T


PACK_ROWS = 512


def _pack_body(tab_ref, out_ref):
    bits = pltpu.bitcast(tab_ref[0].astype(BF).astype(F32), jnp.int32)
    words = lax.shift_right_logical(bits[:, :D_MODEL // 2], 16) | bits[:, D_MODEL // 2:]
    for s in range(ROW_WORDS):
        out_ref[pl.ds(s, PACK_ROWS, stride=ROW_WORDS), :] = words[:, s * LANES:(s + 1) * LANES]


def pack_table(tabs, layer):
    E = tabs.shape[1]
    return pl.pallas_call(
        _pack_body, grid=(E // PACK_ROWS,),
        in_specs=[pl.BlockSpec((1, PACK_ROWS, D_MODEL), lambda i: (layer, i, 0))],
        out_specs=pl.BlockSpec((PACK_ROWS * ROW_WORDS, LANES), lambda i: (i, 0)),
        out_shape=jax.ShapeDtypeStruct((E * ROW_WORDS, LANES), jnp.int32),
        name="pack_table",
    )(tabs)


def _gather_group(idx_ref, tab_ref, tile_refs, tokens):
    if idx_ref.shape[0] == TOKEN_TILE:
        rows = [idx_ref.at[t] for t in tokens]
        read_id = lambda i, k: rows[i][k]
    else:
        cols = []
        for t in tokens:
            lane0 = pl.multiple_of((t // LANES) * LANES, LANES)
            cols.append(pltpu.roll(idx_ref[:, pl.ds(lane0, LANES)], (LANES - t % LANES) % LANES, 1))

        def read_id(i, k):
            v = cols[i][(k // 8) * 8:(k // 8) * 8 + 8, :]
            return (pltpu.roll(v, 8 - k % 8, 0) if k % 8 else v)[0, 0]

    for k in range(PEER_SEL):
        for i, tile_ref in enumerate(tile_refs):
            e = pl.multiple_of(read_id(i, k), ROW_WORDS)
            tile_ref[pl.ds(k, ROW_WORDS, stride=TILE_STRIDE), :] = tab_ref[pl.ds(e, ROW_WORDS), :]


def _unpack_rows(tile_ref):
    lo, hi = [], []
    for j in range(ROW_WORDS):
        w = tile_ref[j * TILE_STRIDE:j * TILE_STRIDE + PEER_SEL, :]
        lo.append(pltpu.bitcast(w << 16, F32))
        hi.append(pltpu.bitcast(w & jnp.int32(-65536), F32))
    return jnp.concatenate(lo + hi, axis=1).astype(BF)


def _for_each_token(idx_ref, tab_ref, tiles, compute):
    group = len(tiles) // 2
    set_a, set_b = tiles[:group], tiles[group:]
    _gather_group(idx_ref, tab_ref, set_a, list(range(group)))

    def step(s, carry):
        base = s * 2 * group
        _gather_group(idx_ref, tab_ref, set_b, [base + group + i for i in range(group)])
        for i in range(group):
            compute(set_a[i], base + i)
        for i in range(group):
            compute(set_b[i], base + group + i)
        _gather_group(idx_ref, tab_ref, set_a,
                      [jnp.minimum(base + 2 * group + i, TOKEN_TILE - 1) for i in range(group)])
        return carry

    lax.fori_loop(0, TOKEN_TILE // (2 * group), step, 0)


def _peer_u_body(idx_ref, h_ref, gate_ref, tab_ref, act_ref, *tiles):
    def compute(tile, t):
        hb = jnp.broadcast_to(h_ref[pl.ds(t, 1), :], (8, D_MODEL)).astype(BF)
        s = lax.dot_general(hb, _unpack_rows(tile), (((1,), (1,)), ((), ())), preferred_element_type=F32)
        act_ref[pl.ds(t, 1), :] = s[0:1]

    _for_each_token(idx_ref, tab_ref, tiles, compute)
    d = act_ref[...]
    act_ref[...] = 0.5 * d * (1.0 + lax.erf(d * (2.0 ** -0.5))) * gate_ref[...]


def _peer_v_body(idx_ref, act_ref, x_ref, gate_ref, tab_ref, fw_ref, out_ref, *tiles, final_norm):
    def compute(tile, t):
        ab = jnp.broadcast_to(act_ref[pl.ds(t, 1), :], (8, PEER_SEL)).astype(BF)
        o = jnp.dot(ab, _unpack_rows(tile), preferred_element_type=F32)
        out_ref[pl.ds(t, 1), :] = o[0:1]

    _for_each_token(idx_ref, tab_ref, tiles, compute)
    y = x_ref[...] + gate_ref[0, 0] * out_ref[...]
    if final_norm:
        y = y * lax.rsqrt(jnp.mean(y * y, axis=-1, keepdims=True) + NORM_EPS) * fw_ref[...]
    out_ref[...] = y


def peer_ffn(x, x_spec, n, mods, mod_spec, norm_w, wq, keys, u_pack, v_pack, final_norm_w=None):
    grid = (n // TOKEN_TILE,)
    sel_spec = _tok_spec(PEER_SEL)
    h, idx_words, idxt_words, gate = pl.pallas_call(
        _route_body, grid=grid,
        in_specs=[x_spec, mod_spec, mod_spec, _row_spec(D_MODEL),
                  pl.BlockSpec((D_MODEL, PEER_QW), lambda i: (0, 0)),
                  pl.BlockSpec((PEER_HEADS * 2, PEER_NKEYS, PEER_HALF), lambda i: (0, 0, 0))],
        out_specs=[_tok_spec(D_MODEL), sel_spec, _SEL_MAJOR_SPEC, sel_spec],
        out_shape=[jax.ShapeDtypeStruct((n, D_MODEL), F32),
                   jax.ShapeDtypeStruct((n, PEER_SEL), jnp.int32),
                   jax.ShapeDtypeStruct((PEER_SEL, n), jnp.int32),
                   jax.ShapeDtypeStruct((n, PEER_SEL), F32)],
        scratch_shapes=[pltpu.VMEM((TOKEN_TILE, PEER_QW), F32),
                        pltpu.VMEM((PEER_SEL, TOKEN_TILE), jnp.int32),
                        pltpu.VMEM((PEER_SEL, TOKEN_TILE), F32)],
        compiler_params=pltpu.CompilerParams(vmem_limit_bytes=MIX_VMEM_BYTES),
        name="peer_route",
    )(x, mods[3], mods[4], norm_w.reshape(1, D_MODEL), wq.astype(BF),
      keys.reshape(PEER_HEADS * 2, PEER_NKEYS, PEER_HALF).astype(BF))

    tiles = lambda group: [pltpu.VMEM((ROW_WORDS * TILE_STRIDE, LANES), jnp.int32)] * (2 * group)
    params = pltpu.CompilerParams(vmem_limit_bytes=PEER_VMEM_BYTES)
    act = pl.pallas_call(
        _peer_u_body, grid=grid,
        in_specs=[pl.BlockSpec((TOKEN_TILE, PEER_SEL), lambda i: (i, 0), memory_space=pltpu.SMEM),
                  _tok_spec(D_MODEL), sel_spec, _RESIDENT],
        out_specs=sel_spec,
        out_shape=jax.ShapeDtypeStruct((n, PEER_SEL), F32),
        scratch_shapes=tiles(PEER_U_GROUP), compiler_params=params, name="peer_u",
    )(idx_words, h, gate, u_pack)
    fw = jnp.ones((D_MODEL,), F32) if final_norm_w is None else final_norm_w
    return pl.pallas_call(
        functools.partial(_peer_v_body, final_norm=final_norm_w is not None), grid=grid,
        in_specs=[_SEL_MAJOR_SPEC, sel_spec, x_spec, mod_spec, _RESIDENT, _row_spec(D_MODEL)],
        out_specs=_tok_spec(D_MODEL),
        out_shape=jax.ShapeDtypeStruct((n, D_MODEL), F32),
        scratch_shapes=tiles(PEER_V_GROUP), compiler_params=params, name="peer_v",
    )(idxt_words, act, x, mods[5], v_pack, fw.reshape(1, D_MODEL))


def even_layer(stream, mods, n_seq, tiles_per_seq, ctx_tiles, norm_w, w_in, w_out, sc_conv_w, ssd_conv_w,
               ssd_conv_b, dt_bias, a_log, d_skip, ssd_norm_w):
    ya, z, xs, bm, cm, dt_raw = even_in(stream, mods[0], mods[1], norm_w, w_in, sc_conv_w, ssd_conv_w, ssd_conv_b,
                                        tiles_per_seq, ctx_tiles)
    y0, y1 = ssd_scan(xs, bm, cm, dt_raw, dt_bias, -jnp.exp(a_log), n_seq, ctx_tiles * TOKEN_TILE // SSD_CHUNK)
    return even_out(stream, mods[2], ya, z, xs, y0, y1, d_skip, ssd_norm_w, w_out, tiles_per_seq, ctx_tiles)


def odd_layer(stream, mods, n_seq, tiles_per_seq, ctx_tiles, norm_w, w_in, w_out, lower_bound, hg_norm_w):
    q, f_raw, v, g = odd_in(stream, mods[0], mods[1], norm_w, w_in, tiles_per_seq, ctx_tiles)
    o0, o1 = gla_scan(q, f_raw, v, lower_bound.reshape(2, 1, HG_WIDTH), n_seq, ctx_tiles * TOKEN_TILE // GLA_CHUNK)
    return odd_out(stream, mods[2], o0, o1, g, hg_norm_w, w_out, tiles_per_seq, ctx_tiles)


def kernel(x, c, ctx, c_ctx, ada_w, ada_b, norm_mix_w, norm_ffn_w, norm_f_w, ev_w_in, ev_w_out,
           sc_conv_w, ssd_conv_w, ssd_conv_b, ssd_dt_bias, ssd_a_log, ssd_d, ssd_norm_w,
           od_w_in, od_w_out, hg_lb_logits, hg_norm_w, peer_wq, peer_keys, peer_u, peer_v):
    Bsz, seq, D = x.shape
    ctx_len = ctx.shape[1]
    assert D == D_MODEL and ctx_len == TOKEN_TILE and seq % TOKEN_TILE == 0 and TOKEN_TILE % GRID_W == 0
    lb_sm = jax.nn.softmax(hg_lb_logits.astype(F32), axis=0)
    lower_bounds = jnp.cumsum(lb_sm, axis=0) - lb_sm[0]
    cond = jnp.concatenate([c, c_ctx[None, :], jnp.zeros((7 - Bsz % 8, D), F32)], axis=0)

    ctx_tiles = ctx_len // TOKEN_TILE
    tiles_per_seq = ctx_tiles + seq // TOKEN_TILE
    stream = jnp.concatenate([ctx, x], axis=1).reshape(Bsz * (ctx_len + seq), D)
    for layer in range(DEPTH):
        last = layer == DEPTH - 1
        j = layer // 2
        m = ada_modulation(cond, ada_w[layer], ada_b[layer]).reshape(-1, 6, D)
        mods = [jnp.stack([jnp.broadcast_to(m[Bsz, k], (Bsz, D)), m[:Bsz, k]], axis=1)[:, :, None, :] for k in range(6)]
        if layer % 2 == 0:
            stream = even_layer(stream, mods, Bsz, tiles_per_seq, ctx_tiles, norm_mix_w[layer], ev_w_in[j], ev_w_out[j],
                                sc_conv_w[j], ssd_conv_w[j], ssd_conv_b[j], ssd_dt_bias[j], ssd_a_log[j], ssd_d[j],
                                ssd_norm_w[j])
        else:
            stream = odd_layer(stream, mods, Bsz, tiles_per_seq, ctx_tiles, norm_mix_w[layer], od_w_in[j], od_w_out[j],
                               lower_bounds[layer], hg_norm_w[j])
        u_pack, v_pack = pack_table(peer_u, layer), pack_table(peer_v, layer)
        if not last:
            stream = peer_ffn(stream, _tok_spec(D), stream.shape[0], mods, _mod_spec(tiles_per_seq, ctx_tiles),
                              norm_ffn_w[layer], peer_wq[layer], peer_keys[layer], u_pack, v_pack)
        else:
            lat_tiles = seq // TOKEN_TILE
            lat_spec = pl.BlockSpec((TOKEN_TILE, D),
                                    lambda i: ((i // lat_tiles) * tiles_per_seq + ctx_tiles + i % lat_tiles, 0))
            out = peer_ffn(stream, lat_spec, Bsz * seq, mods, _mod_spec(lat_tiles, 0), norm_ffn_w[layer],
                           peer_wq[layer], peer_keys[layer], u_pack, v_pack, final_norm_w=norm_f_w)
    return out.reshape(Bsz, seq, D)
```

```python
import functools

import jax
import jax.numpy as jnp
from jax import lax
from jax.experimental import pallas as pl
from jax.experimental.pallas import tpu as pltpu

D_MODEL = 1024
DEPTH = 2
GRID_W = 64
CTX_LEN = 256
NORM_EPS = 1e-6
CONV_W = 3
SC_WIDTH = 1024
SSD_HEADS = 16
SSD_HEAD_DIM = 64
SSD_WIDTH = SSD_HEADS * SSD_HEAD_DIM
SSD_GROUPS = 4
SSD_STATE = 128
SSD_BC = SSD_GROUPS * SSD_STATE
SSD_CONV_DIM = SSD_WIDTH + 2 * SSD_BC
EVEN_MIX = SC_WIDTH + SSD_WIDTH
HG_HEADS = 8
HG_KDIM = 128
HG_VDIM = 128
HG_WIDTH = HG_HEADS * HG_KDIM
PEER_HEADS = 8
PEER_NKEYS = 128
PEER_QDIM = 256
PEER_HALF = PEER_QDIM // 2
PEER_TOPK = 16
PEER_SEL = PEER_HEADS * PEER_TOPK
PEER_QW = PEER_HEADS * PEER_QDIM

LANES = 128
TOKEN_TILE = 256
SSD_CHUNK = 128
GLA_CHUNK = 64
ROW_WORDS = D_MODEL // (2 * LANES)
TILE_STRIDE = PEER_SEL + 8
PEER_U_GROUP = 16
PEER_V_GROUP = 8
ROUTE_HEADS_PER_STEP = 4
MIB = 1024 * 1024
PEER_VMEM_BYTES = 48 * MIB
MIX_VMEM_BYTES = 48 * MIB
HI = lax.Precision.HIGHEST
BF = jnp.bfloat16
F32 = jnp.float32


def _mod_spec(tiles_per_seq, ctx_tiles):
    return pl.BlockSpec((1, 1, 1, D_MODEL),
                        lambda i: (i // tiles_per_seq, jnp.where(i % tiles_per_seq < ctx_tiles, 0, 1), 0, 0))


def _tok_spec(width):
    return pl.BlockSpec((TOKEN_TILE, width), lambda i: (i, 0))


def _row_spec(width):
    return pl.BlockSpec((1, width), lambda i: (0, 0))


_RESIDENT = pl.BlockSpec(memory_space=pltpu.VMEM)
_SEL_MAJOR_SPEC = pl.BlockSpec((PEER_SEL, TOKEN_TILE), lambda i: (0, i))


def _norm_mod(x, nw, shift, scale):
    y = x * lax.rsqrt(jnp.mean(x * x, axis=-1, keepdims=True) + NORM_EPS) * nw
    return y * (1.0 + scale) + shift


def _silu(x):
    return x * jax.nn.sigmoid(x)


def _select_dot(a, b, select_lhs):
    sel, x = (a, b) if select_lhs else (b, a)
    sel = sel.astype(F32).astype(BF)
    hi = x.astype(BF)
    rest = x - hi.astype(F32)
    mid = rest.astype(BF)
    parts = (hi, mid, (rest - mid.astype(F32)).astype(BF))
    if select_lhs:
        return sum(jnp.dot(sel, p, preferred_element_type=F32) for p in parts)
    return sum(jnp.dot(p, sel, preferred_element_type=F32) for p in parts)


def _ada_body(c_ref, w_ref, b_ref, o_ref):
    cond = _silu(c_ref[...]).astype(BF)
    o_ref[...] = jnp.dot(cond, w_ref[...].astype(BF), preferred_element_type=F32) + b_ref[...]


def ada_modulation(cond, w, b):
    rows = cond.shape[0]
    return pl.pallas_call(
        _ada_body, grid=(6,),
        in_specs=[pl.BlockSpec((rows, D_MODEL), lambda j: (0, 0)),
                  pl.BlockSpec((D_MODEL, D_MODEL), lambda j: (0, j)),
                  pl.BlockSpec((1, D_MODEL), lambda j: (0, j))],
        out_specs=pl.BlockSpec((rows, D_MODEL), lambda j: (0, j)),
        out_shape=jax.ShapeDtypeStruct((rows, 6 * D_MODEL), F32),
        name="ada_modulation",
    )(cond, w, b.reshape(1, 6 * D_MODEL))


EVEN_W_COLS = 3 * SC_WIDTH + SSD_WIDTH + SSD_CONV_DIM + 2 * LANES


def _even_in_body(x_ref, shift_ref, scale_ref, nw_ref, w_ref, cw_sc_ref, cw_ssd_ref, cb_ssd_ref,
                  ya_ref, z_ref, xs_ref, bm_ref, cm_ref, dt_ref, *, tiles_per_seq, ctx_tiles):
    T = TOKEN_TILE
    h = _norm_mod(x_ref[...], nw_ref[...], shift_ref[0, 0], scale_ref[0, 0]).astype(BF)
    is_ctx = (pl.program_id(0) % tiles_per_seq) < ctx_tiles
    row_len = jnp.where(is_ctx, T, GRID_W)
    pos = lax.broadcasted_iota(jnp.int32, (T, 1), 0) & (row_len - 1)
    first, last = pos == 0, pos == row_len - 1

    def conv(u, cw_ref):
        prev = jnp.where(first, 0.0, pltpu.roll(u, 1, 0))
        nxt = jnp.where(last, 0.0, pltpu.roll(u, T - 1, 0))
        return prev * cw_ref[0:1, :] + u * cw_ref[1:2, :] + nxt * cw_ref[2:3, :]

    def proj(lo, width):
        return jnp.dot(h, w_ref[:, lo:lo + width], preferred_element_type=F32)

    sc_x, sc_b, sc_c = proj(0, SC_WIDTH), proj(SC_WIDTH, SC_WIDTH), proj(2 * SC_WIDTH, SC_WIDTH)
    ya_ref[...] = sc_b * conv(sc_c * sc_x, cw_sc_ref)
    z_ref[...] = proj(3 * SC_WIDTH, SSD_WIDTH)
    xbc = _silu(conv(proj(3 * SC_WIDTH + SSD_WIDTH, SSD_CONV_DIM), cw_ssd_ref) + cb_ssd_ref[...])
    xs_ref[...] = xbc[:, :SSD_WIDTH]
    bm_ref[...] = xbc[:, SSD_WIDTH:SSD_WIDTH + SSD_BC]
    cm_ref[...] = xbc[:, SSD_WIDTH + SSD_BC:]
    dt_ref[...] = proj(3 * SC_WIDTH + SSD_WIDTH + SSD_CONV_DIM, 2 * LANES)


def even_in(x, mod_shift, mod_scale, norm_w, w_in, sc_conv_w, ssd_conv_w, ssd_conv_b, tiles_per_seq, ctx_tiles):
    n = x.shape[0]
    main = 3 * SC_WIDTH + SSD_WIDTH + SSD_CONV_DIM
    w_dt = jnp.zeros((D_MODEL, 2, LANES), F32).at[:, :, :SSD_HEADS].set(w_in[:, main:].reshape(D_MODEL, 2, SSD_HEADS))
    w = jnp.concatenate([w_in[:, :main], w_dt.reshape(D_MODEL, 2 * LANES)], axis=1).astype(BF)
    mod = _mod_spec(tiles_per_seq, ctx_tiles)
    widths = (SC_WIDTH, SSD_WIDTH, SSD_WIDTH, SSD_BC, SSD_BC, 2 * LANES)
    return pl.pallas_call(
        functools.partial(_even_in_body, tiles_per_seq=tiles_per_seq, ctx_tiles=ctx_tiles),
        grid=(n // TOKEN_TILE,),
        in_specs=[_tok_spec(D_MODEL), mod, mod, _row_spec(D_MODEL), _RESIDENT,
                  pl.BlockSpec((CONV_W, SC_WIDTH), lambda i: (0, 0)),
                  pl.BlockSpec((CONV_W, SSD_CONV_DIM), lambda i: (0, 0)), _row_spec(SSD_CONV_DIM)],
        out_specs=[_tok_spec(wd) for wd in widths],
        out_shape=[jax.ShapeDtypeStruct((n, wd), F32) for wd in widths],
        compiler_params=pltpu.CompilerParams(vmem_limit_bytes=MIX_VMEM_BYTES),
        name="even_in",
    )(x, mod_shift, mod_scale, norm_w.reshape(1, D_MODEL), w, sc_conv_w, ssd_conv_w,
      ssd_conv_b.reshape(1, SSD_CONV_DIM))


def _scan_row(n_chunks, ctx_chunks):
    def row(b, d, c):
        if d == 0:
            return b * n_chunks + c
        return b * n_chunks + jnp.where(c < ctx_chunks, ctx_chunks - 1 - c, n_chunks - 1 + ctx_chunks - c)
    return row


def _causal_mask(L, d):
    r = lax.broadcasted_iota(jnp.int32, (L, L), 0)
    c = lax.broadcasted_iota(jnp.int32, (L, L), 1)
    return c <= r if d == 0 else c >= r


SSD_GW = (SSD_HEADS // SSD_GROUPS) * SSD_HEAD_DIM


def _alternate(*phased):
    live = list(phased)
    while live:
        live = [g for g in live if next(g, StopIteration) is not StopIteration]


def _ssd_chunk(d, xs_ref, bm_ref, cm_ref, dt_ref, bias, a, y_ref, state):
    L = SSD_CHUNK
    mask = _causal_mask(L, d)
    v = dt_ref[...] + bias
    dt = jnp.maximum(v, 0.0) + jnp.log1p(jnp.exp(-jnp.abs(v)))
    la = _select_dot(mask, dt * a, True)
    la_t = la.T
    head_of_lane = lax.broadcasted_iota(jnp.int32, (LANES, SSD_WIDTH), 1) // SSD_HEAD_DIM
    expand = lax.broadcasted_iota(jnp.int32, (LANES, SSD_WIDTH), 0) == head_of_lane
    dt_x = _select_dot(dt, expand, False)
    la_x = _select_dot(la, expand, False)
    end_x = la_x[L - 1:L] if d == 0 else la_x[0:1]
    xdt = xs_ref[...] * dt_x
    w_state = (xdt * jnp.exp(end_x - la_x)).astype(BF)
    xdt_b = xdt.astype(BF)
    e_la = jnp.exp(la_x)
    dec = jnp.exp(end_x)
    yield
    for g in range(SSD_GROUPS):
        bm = bm_ref[:, g * SSD_STATE:(g + 1) * SSD_STATE].astype(BF)
        cm = cm_ref[:, g * SSD_STATE:(g + 1) * SSD_STATE].astype(BF)
        cb = lax.dot_general(cm, bm, (((1,), (1,)), ((), ())), preferred_element_type=F32)
        lanes = slice(g * SSD_GW, (g + 1) * SSD_GW)
        s_old = state[g]
        y_in = jnp.dot(cm, s_old.astype(BF), preferred_element_type=F32) * e_la[:, lanes]
        ys = []
        for j in range(SSD_HEADS // SSD_GROUPS):
            h = g * (SSD_HEADS // SSD_GROUPS) + j
            seg = la[:, h:h + 1] - la_t[h:h + 1, :]
            m = (cb * jnp.exp(jnp.where(mask, seg, -jnp.inf))).astype(BF)
            ys.append(jnp.dot(m, xdt_b[:, h * SSD_HEAD_DIM:(h + 1) * SSD_HEAD_DIM], preferred_element_type=F32))
        y_ref[:, lanes] = y_in + jnp.concatenate(ys, axis=1)
        state[g] = dec[:, lanes] * s_old + jnp.dot(bm.T, w_state[:, lanes], preferred_element_type=F32)
        yield


def _ssd_body(xs0_ref, bm0_ref, cm0_ref, dt0_ref, xs1_ref, bm1_ref, cm1_ref, dt1_ref, bias_ref, a_ref,
              y0_ref, y1_ref, state):
    @pl.when(pl.program_id(1) == 0)
    def _():
        state[...] = jnp.zeros_like(state)

    _alternate(_ssd_chunk(0, xs0_ref, bm0_ref, cm0_ref, dt0_ref, bias_ref[0], a_ref[0], y0_ref, state.at[0]),
               _ssd_chunk(1, xs1_ref, bm1_ref, cm1_ref, dt1_ref, bias_ref[1], a_ref[1], y1_ref, state.at[1]))


def ssd_scan(xs, bm, cm, dt_raw, dt_bias, a, n_seq, ctx_chunks):
    n = xs.shape[0]
    n_chunks = n // n_seq // SSD_CHUNK
    row = _scan_row(n_chunks, ctx_chunks)
    pad = lambda p: jnp.zeros((2, 1, LANES), F32).at[:, 0, :SSD_HEADS].set(p)
    tok = lambda d, width: pl.BlockSpec((SSD_CHUNK, width), lambda b, c: (row(b, d, c), 0))
    ins = lambda d: [tok(d, SSD_WIDTH), tok(d, SSD_BC), tok(d, SSD_BC),
                     pl.BlockSpec((SSD_CHUNK, LANES), lambda b, c: (row(b, d, c), d))]
    per_dir = pl.BlockSpec((2, 1, LANES), lambda b, c: (0, 0, 0))
    sds = jax.ShapeDtypeStruct((n, SSD_WIDTH), F32)
    return pl.pallas_call(
        _ssd_body, grid=(n_seq, n_chunks),
        in_specs=ins(0) + ins(1) + [per_dir, per_dir],
        out_specs=[tok(0, SSD_WIDTH), tok(1, SSD_WIDTH)],
        out_shape=[sds, sds],
        scratch_shapes=[pltpu.VMEM((2, SSD_GROUPS, SSD_STATE, SSD_GW), F32)],
        compiler_params=pltpu.CompilerParams(dimension_semantics=("arbitrary", "arbitrary")),
        name="ssd_scan",
    )(xs, bm, cm, dt_raw, xs, bm, cm, dt_raw, pad(dt_bias), pad(a))


def _gla_chunk(d, q_ref, f_ref, v_ref, lb, o_ref, state):
    L = GLA_CHUNK
    mask = _causal_mask(L, d)
    f = lb + (1.0 - lb) * jax.nn.sigmoid(f_ref[0])
    gc = _select_dot(mask, jnp.log(f), True)
    g_end = gc[L - 1:L] if d == 0 else gc[0:1]
    k = 1.0 - f
    qg = (q_ref[...] * jnp.exp(gc)).astype(BF)
    kg = (k * jnp.exp(-gc)).astype(BF)
    k_end = (k * jnp.exp(g_end - gc)).astype(BF)
    dec = jnp.exp(g_end)
    vb = v_ref[...].astype(BF)
    yield
    for h in range(HG_HEADS):
        lanes = slice(h * HG_KDIM, (h + 1) * HG_KDIM)
        att = lax.dot_general(qg[:, lanes], kg[:, lanes], (((1,), (1,)), ((), ())), preferred_element_type=F32)
        att = jnp.where(mask, att, 0.0).astype(BF)
        s_old = state[h]
        o = jnp.dot(att, vb[:, lanes], preferred_element_type=F32)
        o_ref[:, lanes] = o + lax.dot_general(qg[:, lanes], s_old.astype(BF), (((1,), (1,)), ((), ())),
                                              preferred_element_type=F32)
        state[h] = s_old * dec[:, lanes] + jnp.dot(vb[:, lanes].T, k_end[:, lanes], preferred_element_type=F32)
        yield


def _gla_body(q0_ref, f0_ref, v0_ref, q1_ref, f1_ref, v1_ref, lb_ref, o0_ref, o1_ref, state):
    @pl.when(pl.program_id(1) == 0)
    def _():
        state[...] = jnp.zeros_like(state)

    _alternate(_gla_chunk(0, q0_ref, f0_ref, v0_ref, lb_ref[0], o0_ref, state.at[0]),
               _gla_chunk(1, q1_ref, f1_ref, v1_ref, lb_ref[1], o1_ref, state.at[1]))


def gla_scan(q, f_raw, v, lb, n_seq, ctx_chunks):
    n = q.shape[0]
    n_chunks = n // n_seq // GLA_CHUNK
    row = _scan_row(n_chunks, ctx_chunks)
    tok = lambda d: pl.BlockSpec((GLA_CHUNK, HG_WIDTH), lambda b, c: (row(b, d, c), 0))
    ins = lambda d: [tok(d), pl.BlockSpec((1, GLA_CHUNK, HG_WIDTH), lambda b, c: (d, row(b, d, c), 0)), tok(d)]
    sds = jax.ShapeDtypeStruct((n, HG_WIDTH), F32)
    return pl.pallas_call(
        _gla_body, grid=(n_seq, n_chunks),
        in_specs=ins(0) + ins(1) + [pl.BlockSpec((2, 1, HG_WIDTH), lambda b, c: (0, 0, 0))],
        out_specs=[tok(0), tok(1)],
        out_shape=[sds, sds],
        scratch_shapes=[pltpu.VMEM((2, HG_HEADS, HG_VDIM, HG_KDIM), F32)],
        compiler_params=pltpu.CompilerParams(dimension_semantics=("arbitrary", "arbitrary")),
        name="gla_scan",
    )(q, f_raw, v, q, f_raw, v, lb)


def _group_rmsnorm(y, groups):
    width = y.shape[1] // groups
    parts = []
    for g in range(groups):
        seg = y[:, g * width:(g + 1) * width]
        parts.append(seg * lax.rsqrt(jnp.mean(seg * seg, axis=-1, keepdims=True) + NORM_EPS))
    return jnp.concatenate(parts, axis=1)


def _even_out_body(x_ref, gate_ref, ya_ref, z_ref, xs_ref, y0_ref, y1_ref, dskip_ref, nw_ref, w_ref, o_ref):
    yb = (y0_ref[...] + y1_ref[...] + dskip_ref[...] * xs_ref[...]) * _silu(z_ref[...])
    yb = _group_rmsnorm(yb, SSD_GROUPS) * nw_ref[...]
    mix = jnp.concatenate([ya_ref[...], yb], axis=1).astype(BF)
    o_ref[...] = x_ref[...] + gate_ref[0, 0] * jnp.dot(mix, w_ref[...], preferred_element_type=F32)


def even_out(x, mod_gate, ya, z, xs, y0, y1, d_skip, norm_w, w_out, tiles_per_seq, ctx_tiles):
    n = x.shape[0]
    return pl.pallas_call(
        _even_out_body, grid=(n // TOKEN_TILE,),
        in_specs=[_tok_spec(D_MODEL), _mod_spec(tiles_per_seq, ctx_tiles), _tok_spec(SC_WIDTH), _tok_spec(SSD_WIDTH),
                  _tok_spec(SSD_WIDTH), _tok_spec(SSD_WIDTH), _tok_spec(SSD_WIDTH), _row_spec(SSD_WIDTH),
                  _row_spec(SSD_WIDTH), _RESIDENT],
        out_specs=_tok_spec(D_MODEL),
        out_shape=jax.ShapeDtypeStruct((n, D_MODEL), F32),
        compiler_params=pltpu.CompilerParams(vmem_limit_bytes=MIX_VMEM_BYTES),
        name="even_out",
    )(x, mod_gate, ya, z, xs, y0, y1, jnp.repeat(d_skip, SSD_HEAD_DIM).reshape(1, SSD_WIDTH),
      norm_w.reshape(1, SSD_WIDTH), w_out.astype(BF))


def _odd_in_body(x_ref, shift_ref, scale_ref, nw_ref, w_ref, q_ref, f_ref, v_ref, g_ref):
    h = _norm_mod(x_ref[...], nw_ref[...], shift_ref[0, 0], scale_ref[0, 0]).astype(BF)
    proj = lambda j: jnp.dot(h, w_ref[:, j * HG_WIDTH:(j + 1) * HG_WIDTH], preferred_element_type=F32)
    q_ref[...] = _silu(proj(0)) * (HG_KDIM ** -0.5)
    f_ref[0] = proj(1)
    f_ref[1] = proj(2)
    v_ref[...] = proj(3)
    g_ref[...] = proj(4)


def odd_in(x, mod_shift, mod_scale, norm_w, w_in, tiles_per_seq, ctx_tiles):
    n = x.shape[0]
    mod = _mod_spec(tiles_per_seq, ctx_tiles)
    tok = _tok_spec(HG_WIDTH)
    sds = jax.ShapeDtypeStruct((n, HG_WIDTH), F32)
    return pl.pallas_call(
        _odd_in_body, grid=(n // TOKEN_TILE,),
        in_specs=[_tok_spec(D_MODEL), mod, mod, _row_spec(D_MODEL), _RESIDENT],
        out_specs=[tok, pl.BlockSpec((2, TOKEN_TILE, HG_WIDTH), lambda i: (0, i, 0)), tok, tok],
        out_shape=[sds, jax.ShapeDtypeStruct((2, n, HG_WIDTH), F32), sds, sds],
        compiler_params=pltpu.CompilerParams(vmem_limit_bytes=MIX_VMEM_BYTES),
        name="odd_in",
    )(x, mod_shift, mod_scale, norm_w.reshape(1, D_MODEL), w_in.astype(BF))


def _odd_out_body(x_ref, gate_ref, o0_ref, o1_ref, g_ref, nw_ref, w_ref, out_ref):
    o = _group_rmsnorm(o0_ref[...] + o1_ref[...], HG_HEADS) * nw_ref[...] * _silu(g_ref[...])
    out_ref[...] = x_ref[...] + gate_ref[0, 0] * jnp.dot(o.astype(BF), w_ref[...], preferred_element_type=F32)


def odd_out(x, mod_gate, o0, o1, g, norm_w, w_out, tiles_per_seq, ctx_tiles):
    n = x.shape[0]
    return pl.pallas_call(
        _odd_out_body, grid=(n // TOKEN_TILE,),
        in_specs=[_tok_spec(D_MODEL), _mod_spec(tiles_per_seq, ctx_tiles), _tok_spec(HG_WIDTH), _tok_spec(HG_WIDTH),
                  _tok_spec(HG_WIDTH), _row_spec(HG_WIDTH), _RESIDENT],
        out_specs=_tok_spec(D_MODEL),
        out_shape=jax.ShapeDtypeStruct((n, D_MODEL), F32),
        compiler_params=pltpu.CompilerParams(vmem_limit_bytes=MIX_VMEM_BYTES),
        name="odd_out",
    )(x, mod_gate, o0, o1, g, norm_w.reshape(1, HG_WIDTH), w_out.astype(BF))


def _top16(vals, ids, n_ids):
    top_s, top_i = [], []
    for r in range(PEER_TOPK):
        m = jnp.max(vals, axis=0, keepdims=True)
        pos = jnp.min(jnp.where(vals == m, ids, float(n_ids)), axis=0, keepdims=True)
        if r + 1 < PEER_TOPK:
            vals = jnp.where(ids == pos, -jnp.inf, vals)
        top_s.append(m)
        top_i.append(pos)
    return jnp.concatenate(top_s, axis=0), jnp.concatenate(top_i, axis=0).astype(jnp.int32)


def _pick_row(table, row):
    r = lax.broadcasted_iota(jnp.int32, table.shape, 0)
    return jnp.sum(jnp.where(r == row, table, 0), axis=0, keepdims=True)


def _route_body(x_ref, shift_ref, scale_ref, nw_ref, wq_ref, keys_ref, h_ref, idx_ref, idxt_ref, gate_ref,
                q_scr, idx_scr, gate_scr):
    h = _norm_mod(x_ref[...], nw_ref[...], shift_ref[0, 0], scale_ref[0, 0])
    h_ref[...] = h
    q_scr[...] = jnp.dot(h.astype(BF), wq_ref[...], preferred_element_type=F32)

    def route(hd, tok0):
        tops = []
        for a in range(2):
            col = pl.multiple_of(hd * PEER_QDIM + a * PEER_HALF, PEER_HALF)
            qs = q_scr[pl.ds(tok0, LANES), pl.ds(col, PEER_HALF)].astype(BF)
            sc = lax.dot_general(keys_ref[hd * 2 + a], qs, (((1,), (1,)), ((), ())),
                                 preferred_element_type=F32)
            tops.append(_top16(sc, lax.broadcasted_iota(jnp.int32, sc.shape, 0).astype(F32), PEER_NKEYS))
        (s0, i0), (s1, i1) = tops
        sub = lax.broadcasted_iota(jnp.int32, (8, LANES), 0).astype(F32)
        chunks = [(s0[0:8] + s1[0:1], sub * PEER_TOPK), (s0[8:16] + s1[0:1], (sub + 8) * PEER_TOPK)]
        chunks += [(s0[0:8] + s1[b:b + 1], sub * PEER_TOPK + b) for b in range(1, 8)]
        chunks.append((s0[0:1] + s1[8:16], sub + 8))
        best, pos = _top16(jnp.concatenate([c[0] for c in chunks], axis=0),
                           jnp.concatenate([c[1] for c in chunks], axis=0), PEER_TOPK * PEER_TOPK)
        ids = jnp.concatenate(
            [_pick_row(i0, pos[r:r + 1] >> 4) * PEER_NKEYS + _pick_row(i1, pos[r:r + 1] & (PEER_TOPK - 1))
             for r in range(PEER_TOPK)], axis=0)
        e = jnp.exp(best - best[0:1])
        r0 = pl.multiple_of(hd * PEER_TOPK, PEER_TOPK)
        idx_scr[pl.ds(r0, PEER_TOPK), pl.ds(tok0, LANES)] = ids * ROW_WORDS
        gate_scr[pl.ds(r0, PEER_TOPK), pl.ds(tok0, LANES)] = e / jnp.sum(e, axis=0, keepdims=True)

    def head_group(hg, carry):
        for hh in range(ROUTE_HEADS_PER_STEP):
            for lg in range(TOKEN_TILE // LANES):
                route(hg * ROUTE_HEADS_PER_STEP + hh, lg * LANES)
        return carry

    lax.fori_loop(0, PEER_HEADS // ROUTE_HEADS_PER_STEP, head_group, 0)
    idx_ref[...] = idx_scr[...].T
    idxt_ref[...] = idx_scr[...]
    gate_ref[...] = gate_scr[...].T


PACK_ROWS = 512


def _pack_body(tab_ref, out_ref):
    bits = pltpu.bitcast(tab_ref[0].astype(BF).astype(F32), jnp.int32)
    words = lax.shift_right_logical(bits[:, :D_MODEL // 2], 16) | bits[:, D_MODEL // 2:]
    for s in range(ROW_WORDS):
        out_ref[pl.ds(s, PACK_ROWS, stride=ROW_WORDS), :] = words[:, s * LANES:(s + 1) * LANES]


def pack_table(tabs, layer):
    E = tabs.shape[1]
    return pl.pallas_call(
        _pack_body, grid=(E // PACK_ROWS,),
        in_specs=[pl.BlockSpec((1, PACK_ROWS, D_MODEL), lambda i: (layer, i, 0))],
        out_specs=pl.BlockSpec((PACK_ROWS * ROW_WORDS, LANES), lambda i: (i, 0)),
        out_shape=jax.ShapeDtypeStruct((E * ROW_WORDS, LANES), jnp.int32),
        name="pack_table",
    )(tabs)


def _gather_group(idx_ref, tab_ref, tile_refs, tokens):
    if idx_ref.shape[0] == TOKEN_TILE:
        rows = [idx_ref.at[t] for t in tokens]
        read_id = lambda i, k: rows[i][k]
    else:
        cols = []
        for t in tokens:
            lane0 = pl.multiple_of((t // LANES) * LANES, LANES)
            cols.append(pltpu.roll(idx_ref[:, pl.ds(lane0, LANES)], (LANES - t % LANES) % LANES, 1))

        def read_id(i, k):
            v = cols[i][(k // 8) * 8:(k // 8) * 8 + 8, :]
            return (pltpu.roll(v, 8 - k % 8, 0) if k % 8 else v)[0, 0]

    for k in range(PEER_SEL):
        for i, tile_ref in enumerate(tile_refs):
            e = pl.multiple_of(read_id(i, k), ROW_WORDS)
            tile_ref[pl.ds(k, ROW_WORDS, stride=TILE_STRIDE), :] = tab_ref[pl.ds(e, ROW_WORDS), :]


def _unpack_rows(tile_ref):
    lo, hi = [], []
    for j in range(ROW_WORDS):
        w = tile_ref[j * TILE_STRIDE:j * TILE_STRIDE + PEER_SEL, :]
        lo.append(pltpu.bitcast(w << 16, F32))
        hi.append(pltpu.bitcast(w & jnp.int32(-65536), F32))
    return jnp.concatenate(lo + hi, axis=1).astype(BF)


def _for_each_token(idx_ref, tab_ref, tiles, compute):
    group = len(tiles) // 2
    set_a, set_b = tiles[:group], tiles[group:]
    _gather_group(idx_ref, tab_ref, set_a, list(range(group)))

    def step(s, carry):
        base = s * 2 * group
        _gather_group(idx_ref, tab_ref, set_b, [base + group + i for i in range(group)])
        for i in range(group):
            compute(set_a[i], base + i)
        for i in range(group):
            compute(set_b[i], base + group + i)
        _gather_group(idx_ref, tab_ref, set_a,
                      [jnp.minimum(base + 2 * group + i, TOKEN_TILE - 1) for i in range(group)])
        return carry

    lax.fori_loop(0, TOKEN_TILE // (2 * group), step, 0)


def _peer_u_body(idx_ref, h_ref, gate_ref, tab_ref, act_ref, *tiles):
    def compute(tile, t):
        hb = jnp.broadcast_to(h_ref[pl.ds(t, 1), :], (8, D_MODEL)).astype(BF)
        s = lax.dot_general(hb, _unpack_rows(tile), (((1,), (1,)), ((), ())), preferred_element_type=F32)
        act_ref[pl.ds(t, 1), :] = s[0:1]

    _for_each_token(idx_ref, tab_ref, tiles, compute)
    d = act_ref[...]
    act_ref[...] = 0.5 * d * (1.0 + lax.erf(d * (2.0 ** -0.5))) * gate_ref[...]


def _peer_v_body(idx_ref, act_ref, x_ref, gate_ref, tab_ref, fw_ref, out_ref, *tiles, final_norm):
    def compute(tile, t):
        ab = jnp.broadcast_to(act_ref[pl.ds(t, 1), :], (8, PEER_SEL)).astype(BF)
        o = jnp.dot(ab, _unpack_rows(tile), preferred_element_type=F32)
        out_ref[pl.ds(t, 1), :] = o[0:1]

    _for_each_token(idx_ref, tab_ref, tiles, compute)
    y = x_ref[...] + gate_ref[0, 0] * out_ref[...]
    if final_norm:
        y = y * lax.rsqrt(jnp.mean(y * y, axis=-1, keepdims=True) + NORM_EPS) * fw_ref[...]
    out_ref[...] = y


def peer_ffn(x, x_spec, n, mods, mod_spec, norm_w, wq, keys, u_pack, v_pack, final_norm_w=None):
    grid = (n // TOKEN_TILE,)
    sel_spec = _tok_spec(PEER_SEL)
    h, idx_words, idxt_words, gate = pl.pallas_call(
        _route_body, grid=grid,
        in_specs=[x_spec, mod_spec, mod_spec, _row_spec(D_MODEL),
                  pl.BlockSpec((D_MODEL, PEER_QW), lambda i: (0, 0)),
                  pl.BlockSpec((PEER_HEADS * 2, PEER_NKEYS, PEER_HALF), lambda i: (0, 0, 0))],
        out_specs=[_tok_spec(D_MODEL), sel_spec, _SEL_MAJOR_SPEC, sel_spec],
        out_shape=[jax.ShapeDtypeStruct((n, D_MODEL), F32),
                   jax.ShapeDtypeStruct((n, PEER_SEL), jnp.int32),
                   jax.ShapeDtypeStruct((PEER_SEL, n), jnp.int32),
                   jax.ShapeDtypeStruct((n, PEER_SEL), F32)],
        scratch_shapes=[pltpu.VMEM((TOKEN_TILE, PEER_QW), F32),
                        pltpu.VMEM((PEER_SEL, TOKEN_TILE), jnp.int32),
                        pltpu.VMEM((PEER_SEL, TOKEN_TILE), F32)],
        compiler_params=pltpu.CompilerParams(vmem_limit_bytes=MIX_VMEM_BYTES),
        name="peer_route",
    )(x, mods[3], mods[4], norm_w.reshape(1, D_MODEL), wq.astype(BF),
      keys.reshape(PEER_HEADS * 2, PEER_NKEYS, PEER_HALF).astype(BF))

    tiles = lambda group: [pltpu.VMEM((ROW_WORDS * TILE_STRIDE, LANES), jnp.int32)] * (2 * group)
    params = pltpu.CompilerParams(vmem_limit_bytes=PEER_VMEM_BYTES)
    act = pl.pallas_call(
        _peer_u_body, grid=grid,
        in_specs=[pl.BlockSpec((TOKEN_TILE, PEER_SEL), lambda i: (i, 0), memory_space=pltpu.SMEM),
                  _tok_spec(D_MODEL), sel_spec, _RESIDENT],
        out_specs=sel_spec,
        out_shape=jax.ShapeDtypeStruct((n, PEER_SEL), F32),
        scratch_shapes=tiles(PEER_U_GROUP), compiler_params=params, name="peer_u",
    )(idx_words, h, gate, u_pack)
    fw = jnp.ones((D_MODEL,), F32) if final_norm_w is None else final_norm_w
    return pl.pallas_call(
        functools.partial(_peer_v_body, final_norm=final_norm_w is not None), grid=grid,
        in_specs=[_SEL_MAJOR_SPEC, sel_spec, x_spec, mod_spec, _RESIDENT, _row_spec(D_MODEL)],
        out_specs=_tok_spec(D_MODEL),
        out_shape=jax.ShapeDtypeStruct((n, D_MODEL), F32),
        scratch_shapes=tiles(PEER_V_GROUP), compiler_params=params, name="peer_v",
    )(idxt_words, act, x, mods[5], v_pack, fw.reshape(1, D_MODEL))


def even_layer(stream, mods, n_seq, tiles_per_seq, ctx_tiles, norm_w, w_in, w_out, sc_conv_w, ssd_conv_w,
               ssd_conv_b, dt_bias, a_log, d_skip, ssd_norm_w):
    ya, z, xs, bm, cm, dt_raw = even_in(stream, mods[0], mods[1], norm_w, w_in, sc_conv_w, ssd_conv_w, ssd_conv_b,
                                        tiles_per_seq, ctx_tiles)
    y0, y1 = ssd_scan(xs, bm, cm, dt_raw, dt_bias, -jnp.exp(a_log), n_seq, ctx_tiles * TOKEN_TILE // SSD_CHUNK)
    return even_out(stream, mods[2], ya, z, xs, y0, y1, d_skip, ssd_norm_w, w_out, tiles_per_seq, ctx_tiles)


def odd_layer(stream, mods, n_seq, tiles_per_seq, ctx_tiles, norm_w, w_in, w_out, lower_bound, hg_norm_w):
    q, f_raw, v, g = odd_in(stream, mods[0], mods[1], norm_w, w_in, tiles_per_seq, ctx_tiles)
    o0, o1 = gla_scan(q, f_raw, v, lower_bound.reshape(2, 1, HG_WIDTH), n_seq, ctx_tiles * TOKEN_TILE // GLA_CHUNK)
    return odd_out(stream, mods[2], o0, o1, g, hg_norm_w, w_out, tiles_per_seq, ctx_tiles)


def kernel(x, c, ctx, c_ctx, ada_w, ada_b, norm_mix_w, norm_ffn_w, norm_f_w, ev_w_in, ev_w_out,
           sc_conv_w, ssd_conv_w, ssd_conv_b, ssd_dt_bias, ssd_a_log, ssd_d, ssd_norm_w,
           od_w_in, od_w_out, hg_lb_logits, hg_norm_w, peer_wq, peer_keys, peer_u, peer_v):
    Bsz, seq, D = x.shape
    ctx_len = ctx.shape[1]
    assert D == D_MODEL and ctx_len == TOKEN_TILE and seq % TOKEN_TILE == 0 and TOKEN_TILE % GRID_W == 0
    lb_sm = jax.nn.softmax(hg_lb_logits.astype(F32), axis=0)
    lower_bounds = jnp.cumsum(lb_sm, axis=0) - lb_sm[0]
    cond = jnp.concatenate([c, c_ctx[None, :], jnp.zeros((7 - Bsz % 8, D), F32)], axis=0)

    ctx_tiles = ctx_len // TOKEN_TILE
    tiles_per_seq = ctx_tiles + seq // TOKEN_TILE
    stream = jnp.concatenate([ctx, x], axis=1).reshape(Bsz * (ctx_len + seq), D)
    for layer in range(DEPTH):
        last = layer == DEPTH - 1
        j = layer // 2
        m = ada_modulation(cond, ada_w[layer], ada_b[layer]).reshape(-1, 6, D)
        mods = [jnp.stack([jnp.broadcast_to(m[Bsz, k], (Bsz, D)), m[:Bsz, k]], axis=1)[:, :, None, :] for k in range(6)]
        if layer % 2 == 0:
            stream = even_layer(stream, mods, Bsz, tiles_per_seq, ctx_tiles, norm_mix_w[layer], ev_w_in[j], ev_w_out[j],
                                sc_conv_w[j], ssd_conv_w[j], ssd_conv_b[j], ssd_dt_bias[j], ssd_a_log[j], ssd_d[j],
                                ssd_norm_w[j])
        else:
            stream = odd_layer(stream, mods, Bsz, tiles_per_seq, ctx_tiles, norm_mix_w[layer], od_w_in[j], od_w_out[j],
                               lower_bounds[layer], hg_norm_w[j])
        u_pack, v_pack = pack_table(peer_u, layer), pack_table(peer_v, layer)
        if not last:
            stream = peer_ffn(stream, _tok_spec(D), stream.shape[0], mods, _mod_spec(tiles_per_seq, ctx_tiles),
                              norm_ffn_w[layer], peer_wq[layer], peer_keys[layer], u_pack, v_pack)
        else:
            lat_tiles = seq // TOKEN_TILE
            lat_spec = pl.BlockSpec((TOKEN_TILE, D),
                                    lambda i: ((i // lat_tiles) * tiles_per_seq + ctx_tiles + i % lat_tiles, 0))
            out = peer_ffn(stream, lat_spec, Bsz * seq, mods, _mod_spec(lat_tiles, 0), norm_ffn_w[layer],
                           peer_wq[layer], peer_keys[layer], u_pack, v_pack, final_norm_w=norm_f_w)
    return out.reshape(Bsz, seq, D)
```

```python
import functools

import jax
import jax.numpy as jnp
from jax import lax
from jax.experimental import pallas as pl
from jax.experimental.pallas import tpu as pltpu

D_MODEL = 1024
DEPTH = 2
GRID_W = 64
CTX_LEN = 256
NORM_EPS = 1e-6
CONV_W = 3
SC_WIDTH = 1024
SSD_HEADS = 16
SSD_HEAD_DIM = 64
SSD_WIDTH = SSD_HEADS * SSD_HEAD_DIM
SSD_GROUPS = 4
SSD_STATE = 128
SSD_BC = SSD_GROUPS * SSD_STATE
SSD_CONV_DIM = SSD_WIDTH + 2 * SSD_BC
HG_HEADS = 8
HG_KDIM = 128
HG_VDIM = 128
HG_WIDTH = HG_HEADS * HG_KDIM
PEER_HEADS = 8
PEER_NKEYS = 128
PEER_QDIM = 256
PEER_HALF = PEER_QDIM // 2
PEER_TOPK = 16
PEER_SEL = PEER_HEADS * PEER_TOPK
PEER_QW = PEER_HEADS * PEER_QDIM

LANES = 128
TOKEN_TILE = 256
SSD_CHUNK = 128
GLA_CHUNK = 64
ROW_WORDS = D_MODEL // (2 * LANES)
ROW_BLOCKS = 2 * ROW_WORDS
PEER_U_GROUP = 16
PEER_V_GROUP = 16
ROUTE_HEADS_PER_STEP = 4
MIB = 1024 * 1024
PEER_VMEM_BYTES = 48 * MIB
MIX_VMEM_BYTES = 48 * MIB
BF = jnp.bfloat16
F32 = jnp.float32


def _mod_spec(tiles_per_seq, ctx_tiles):
    return pl.BlockSpec((1, 1, 1, D_MODEL),
                        lambda i: (i // tiles_per_seq, jnp.where(i % tiles_per_seq < ctx_tiles, 0, 1), 0, 0))


def _tok_spec(width):
    return pl.BlockSpec((TOKEN_TILE, width), lambda i: (i, 0))


def _row_spec(width):
    return pl.BlockSpec((1, width), lambda i: (0, 0))


_RESIDENT = pl.BlockSpec(memory_space=pltpu.VMEM)


def _norm_mod(x, nw, shift, scale):
    y = x * lax.rsqrt(jnp.mean(x * x, axis=-1, keepdims=True) + NORM_EPS) * nw
    return y * (1.0 + scale) + shift


def _silu(x):
    return x * jax.nn.sigmoid(x)


def _select_dot(a, b, select_lhs):
    sel, x = (a, b) if select_lhs else (b, a)
    sel = sel.astype(F32).astype(BF)
    hi = x.astype(BF)
    rest = x - hi.astype(F32)
    mid = rest.astype(BF)
    parts = (hi, mid, (rest - mid.astype(F32)).astype(BF))
    if select_lhs:
        return sum(jnp.dot(sel, p, preferred_element_type=F32) for p in parts)
    return sum(jnp.dot(p, sel, preferred_element_type=F32) for p in parts)


def _ada_body(c_ref, w_ref, b_ref, o_ref):
    cond = _silu(c_ref[...]).astype(BF)
    o_ref[...] = jnp.dot(cond, w_ref[...].astype(BF), preferred_element_type=F32) + b_ref[...]


def ada_modulation(cond, w, b):
    rows = cond.shape[0]
    return pl.pallas_call(
        _ada_body, grid=(6,),
        in_specs=[pl.BlockSpec((rows, D_MODEL), lambda j: (0, 0)),
                  pl.BlockSpec((D_MODEL, D_MODEL), lambda j: (0, j)),
                  pl.BlockSpec((1, D_MODEL), lambda j: (0, j))],
        out_specs=pl.BlockSpec((rows, D_MODEL), lambda j: (0, j)),
        out_shape=jax.ShapeDtypeStruct((rows, 6 * D_MODEL), F32),
        name="ada_modulation",
    )(cond, w, b.reshape(1, 6 * D_MODEL))


def _even_in_body(x_ref, shift_ref, scale_ref, nw_ref, w_ref, cw_sc_ref, cw_ssd_ref, cb_ssd_ref,
                  ya_ref, z_ref, xs_ref, bm_ref, cm_ref, dt_ref, *, tiles_per_seq, ctx_tiles):
    T = TOKEN_TILE
    h = _norm_mod(x_ref[...], nw_ref[...], shift_ref[0, 0], scale_ref[0, 0]).astype(BF)
    is_ctx = (pl.program_id(0) % tiles_per_seq) < ctx_tiles
    row_len = jnp.where(is_ctx, T, GRID_W)
    pos = lax.broadcasted_iota(jnp.int32, (T, 1), 0) & (row_len - 1)
    first, last = pos == 0, pos == row_len - 1

    def conv(u, cw_ref):
        prev = jnp.where(first, 0.0, pltpu.roll(u, 1, 0))
        nxt = jnp.where(last, 0.0, pltpu.roll(u, T - 1, 0))
        return prev * cw_ref[0:1, :] + u * cw_ref[1:2, :] + nxt * cw_ref[2:3, :]

    def proj(lo, width):
        return jnp.dot(h, w_ref[:, lo:lo + width], preferred_element_type=F32)

    sc_x, sc_b, sc_c = proj(0, SC_WIDTH), proj(SC_WIDTH, SC_WIDTH), proj(2 * SC_WIDTH, SC_WIDTH)
    ya_ref[...] = sc_b * conv(sc_c * sc_x, cw_sc_ref)
    z_ref[...] = proj(3 * SC_WIDTH, SSD_WIDTH)
    xbc = _silu(conv(proj(3 * SC_WIDTH + SSD_WIDTH, SSD_CONV_DIM), cw_ssd_ref) + cb_ssd_ref[...])
    xs_ref[...] = xbc[:, :SSD_WIDTH]
    bm_ref[...] = xbc[:, SSD_WIDTH:SSD_WIDTH + SSD_BC]
    cm_ref[...] = xbc[:, SSD_WIDTH + SSD_BC:]
    dt_ref[...] = proj(3 * SC_WIDTH + SSD_WIDTH + SSD_CONV_DIM, 2 * LANES)


def even_in(x, mod_shift, mod_scale, norm_w, w_in, sc_conv_w, ssd_conv_w, ssd_conv_b, tiles_per_seq, ctx_tiles):
    n = x.shape[0]
    main = 3 * SC_WIDTH + SSD_WIDTH + SSD_CONV_DIM
    w_dt = jnp.zeros((D_MODEL, 2, LANES), F32).at[:, :, :SSD_HEADS].set(w_in[:, main:].reshape(D_MODEL, 2, SSD_HEADS))
    w = jnp.concatenate([w_in[:, :main], w_dt.reshape(D_MODEL, 2 * LANES)], axis=1).astype(BF)
    mod = _mod_spec(tiles_per_seq, ctx_tiles)
    widths = (SC_WIDTH, SSD_WIDTH, SSD_WIDTH, SSD_BC, SSD_BC, 2 * LANES)
    return pl.pallas_call(
        functools.partial(_even_in_body, tiles_per_seq=tiles_per_seq, ctx_tiles=ctx_tiles),
        grid=(n // TOKEN_TILE,),
        in_specs=[_tok_spec(D_MODEL), mod, mod, _row_spec(D_MODEL), _RESIDENT,
                  pl.BlockSpec((CONV_W, SC_WIDTH), lambda i: (0, 0)),
                  pl.BlockSpec((CONV_W, SSD_CONV_DIM), lambda i: (0, 0)), _row_spec(SSD_CONV_DIM)],
        out_specs=[_tok_spec(wd) for wd in widths],
        out_shape=[jax.ShapeDtypeStruct((n, wd), F32) for wd in widths],
        compiler_params=pltpu.CompilerParams(vmem_limit_bytes=MIX_VMEM_BYTES),
        name="even_in",
    )(x, mod_shift, mod_scale, norm_w.reshape(1, D_MODEL), w, sc_conv_w, ssd_conv_w,
      ssd_conv_b.reshape(1, SSD_CONV_DIM))


def _scan_row(n_chunks, ctx_chunks):
    def row(b, d, c):
        if d == 0:
            return b * n_chunks + c
        return b * n_chunks + jnp.where(c < ctx_chunks, ctx_chunks - 1 - c, n_chunks - 1 + ctx_chunks - c)
    return row


def _causal_mask(L, d):
    r = lax.broadcasted_iota(jnp.int32, (L, L), 0)
    c = lax.broadcasted_iota(jnp.int32, (L, L), 1)
    return c <= r if d == 0 else c >= r


SSD_GW = (SSD_HEADS // SSD_GROUPS) * SSD_HEAD_DIM


def _alternate(*phased):
    live = list(phased)
    while live:
        live = [g for g in live if next(g, StopIteration) is not StopIteration]


def _ssd_chunk(d, xs_ref, bm_ref, cm_ref, dt_ref, bias, a, y_ref, state):
    L = SSD_CHUNK
    mask = _causal_mask(L, d)
    v = dt_ref[...] + bias
    dt = jnp.maximum(v, 0.0) + jnp.log1p(jnp.exp(-jnp.abs(v)))
    la = _select_dot(mask, dt * a, True)
    la_t = la.T
    head_of_lane = lax.broadcasted_iota(jnp.int32, (LANES, SSD_WIDTH), 1) // SSD_HEAD_DIM
    expand = lax.broadcasted_iota(jnp.int32, (LANES, SSD_WIDTH), 0) == head_of_lane
    dt_x = _select_dot(dt, expand, False)
    la_x = _select_dot(la, expand, False)
    end_x = la_x[L - 1:L] if d == 0 else la_x[0:1]
    xdt = xs_ref[...] * dt_x
    w_state = (xdt * jnp.exp(end_x - la_x)).astype(BF)
    xdt_b = xdt.astype(BF)
    e_la = jnp.exp(la_x)
    dec = jnp.exp(end_x)
    yield
    for g in range(SSD_GROUPS):
        bm = bm_ref[:, g * SSD_STATE:(g + 1) * SSD_STATE].astype(BF)
        cm = cm_ref[:, g * SSD_STATE:(g + 1) * SSD_STATE].astype(BF)
        cb = lax.dot_general(cm, bm, (((1,), (1,)), ((), ())), preferred_element_type=F32)
        lanes = slice(g * SSD_GW, (g + 1) * SSD_GW)
        s_old = state[g]
        y_in = jnp.dot(cm, s_old.astype(BF), preferred_element_type=F32) * e_la[:, lanes]
        ys = []
        for j in range(SSD_HEADS // SSD_GROUPS):
            h = g * (SSD_HEADS // SSD_GROUPS) + j
            seg = la[:, h:h + 1] - la_t[h:h + 1, :]
            m = (cb * jnp.exp(jnp.where(mask, seg, -jnp.inf))).astype(BF)
            ys.append(jnp.dot(m, xdt_b[:, h * SSD_HEAD_DIM:(h + 1) * SSD_HEAD_DIM], preferred_element_type=F32))
        y_ref[:, lanes] = y_in + jnp.concatenate(ys, axis=1)
        state[g] = dec[:, lanes] * s_old + jnp.dot(bm.T, w_state[:, lanes], preferred_element_type=F32)
        yield


def _ssd_body(xs0_ref, bm0_ref, cm0_ref, dt0_ref, xs1_ref, bm1_ref, cm1_ref, dt1_ref, bias_ref, a_ref,
              y0_ref, y1_ref, state):
    @pl.when(pl.program_id(1) == 0)
    def _():
        state[...] = jnp.zeros_like(state)

    _alternate(_ssd_chunk(0, xs0_ref, bm0_ref, cm0_ref, dt0_ref, bias_ref[0], a_ref[0], y0_ref, state.at[0]),
               _ssd_chunk(1, xs1_ref, bm1_ref, cm1_ref, dt1_ref, bias_ref[1], a_ref[1], y1_ref, state.at[1]))


def ssd_scan(xs, bm, cm, dt_raw, dt_bias, a, n_seq, ctx_chunks):
    n = xs.shape[0]
    n_chunks = n // n_seq // SSD_CHUNK
    row = _scan_row(n_chunks, ctx_chunks)
    pad = lambda p: jnp.zeros((2, 1, LANES), F32).at[:, 0, :SSD_HEADS].set(p)
    tok = lambda d, width: pl.BlockSpec((SSD_CHUNK, width), lambda b, c: (row(b, d, c), 0))
    ins = lambda d: [tok(d, SSD_WIDTH), tok(d, SSD_BC), tok(d, SSD_BC),
                     pl.BlockSpec((SSD_CHUNK, LANES), lambda b, c: (row(b, d, c), d))]
    per_dir = pl.BlockSpec((2, 1, LANES), lambda b, c: (0, 0, 0))
    sds = jax.ShapeDtypeStruct((n, SSD_WIDTH), F32)
    return pl.pallas_call(
        _ssd_body, grid=(n_seq, n_chunks),
        in_specs=ins(0) + ins(1) + [per_dir, per_dir],
        out_specs=[tok(0, SSD_WIDTH), tok(1, SSD_WIDTH)],
        out_shape=[sds, sds],
        scratch_shapes=[pltpu.VMEM((2, SSD_GROUPS, SSD_STATE, SSD_GW), F32)],
        compiler_params=pltpu.CompilerParams(dimension_semantics=("arbitrary", "arbitrary")),
        name="ssd_scan",
    )(xs, bm, cm, dt_raw, xs, bm, cm, dt_raw, pad(dt_bias), pad(a))


def _gla_chunk(d, q_ref, f_ref, v_ref, lb, o_ref, state):
    L = GLA_CHUNK
    mask = _causal_mask(L, d)
    f = lb + (1.0 - lb) * jax.nn.sigmoid(f_ref[0])
    gc = _select_dot(mask, jnp.log(f), True)
    g_end = gc[L - 1:L] if d == 0 else gc[0:1]
    k = 1.0 - f
    qg = (q_ref[...] * jnp.exp(gc)).astype(BF)
    kg = (k * jnp.exp(-gc)).astype(BF)
    k_end = (k * jnp.exp(g_end - gc)).astype(BF)
    dec = jnp.exp(g_end)
    vb = v_ref[...].astype(BF)
    yield
    for h in range(HG_HEADS):
        lanes = slice(h * HG_KDIM, (h + 1) * HG_KDIM)
        att = lax.dot_general(qg[:, lanes], kg[:, lanes], (((1,), (1,)), ((), ())), preferred_element_type=F32)
        att = jnp.where(mask, att, 0.0).astype(BF)
        s_old = state[h]
        o = jnp.dot(att, vb[:, lanes], preferred_element_type=F32)
        o_ref[:, lanes] = o + lax.dot_general(qg[:, lanes], s_old.astype(BF), (((1,), (1,)), ((), ())),
                                              preferred_element_type=F32)
        state[h] = s_old * dec[:, lanes] + jnp.dot(vb[:, lanes].T, k_end[:, lanes], preferred_element_type=F32)
        yield


def _gla_body(q0_ref, f0_ref, v0_ref, q1_ref, f1_ref, v1_ref, lb_ref, o0_ref, o1_ref, state):
    @pl.when(pl.program_id(1) == 0)
    def _():
        state[...] = jnp.zeros_like(state)

    _alternate(_gla_chunk(0, q0_ref, f0_ref, v0_ref, lb_ref[0], o0_ref, state.at[0]),
               _gla_chunk(1, q1_ref, f1_ref, v1_ref, lb_ref[1], o1_ref, state.at[1]))


def gla_scan(q, f_raw, v, lb, n_seq, ctx_chunks):
    n = q.shape[0]
    n_chunks = n // n_seq // GLA_CHUNK
    row = _scan_row(n_chunks, ctx_chunks)
    tok = lambda d: pl.BlockSpec((GLA_CHUNK, HG_WIDTH), lambda b, c: (row(b, d, c), 0))
    ins = lambda d: [tok(d), pl.BlockSpec((1, GLA_CHUNK, HG_WIDTH), lambda b, c: (d, row(b, d, c), 0)), tok(d)]
    sds = jax.ShapeDtypeStruct((n, HG_WIDTH), F32)
    return pl.pallas_call(
        _gla_body, grid=(n_seq, n_chunks),
        in_specs=ins(0) + ins(1) + [pl.BlockSpec((2, 1, HG_WIDTH), lambda b, c: (0, 0, 0))],
        out_specs=[tok(0), tok(1)],
        out_shape=[sds, sds],
        scratch_shapes=[pltpu.VMEM((2, HG_HEADS, HG_VDIM, HG_KDIM), F32)],
        compiler_params=pltpu.CompilerParams(dimension_semantics=("arbitrary", "arbitrary")),
        name="gla_scan",
    )(q, f_raw, v, q, f_raw, v, lb)


def _group_rmsnorm(y, groups):
    width = y.shape[1] // groups
    parts = []
    for g in range(groups):
        seg = y[:, g * width:(g + 1) * width]
        parts.append(seg * lax.rsqrt(jnp.mean(seg * seg, axis=-1, keepdims=True) + NORM_EPS))
    return jnp.concatenate(parts, axis=1)


def _even_out_body(x_ref, gate_ref, ya_ref, z_ref, xs_ref, y0_ref, y1_ref, dskip_ref, nw_ref, w_ref, o_ref):
    yb = (y0_ref[...] + y1_ref[...] + dskip_ref[...] * xs_ref[...]) * _silu(z_ref[...])
    yb = _group_rmsnorm(yb, SSD_GROUPS) * nw_ref[...]
    mix = jnp.concatenate([ya_ref[...], yb], axis=1).astype(BF)
    o_ref[...] = x_ref[...] + gate_ref[0, 0] * jnp.dot(mix, w_ref[...], preferred_element_type=F32)


def even_out(x, mod_gate, ya, z, xs, y0, y1, d_skip, norm_w, w_out, tiles_per_seq, ctx_tiles):
    n = x.shape[0]
    return pl.pallas_call(
        _even_out_body, grid=(n // TOKEN_TILE,),
        in_specs=[_tok_spec(D_MODEL), _mod_spec(tiles_per_seq, ctx_tiles), _tok_spec(SC_WIDTH), _tok_spec(SSD_WIDTH),
                  _tok_spec(SSD_WIDTH), _tok_spec(SSD_WIDTH), _tok_spec(SSD_WIDTH), _row_spec(SSD_WIDTH),
                  _row_spec(SSD_WIDTH), _RESIDENT],
        out_specs=_tok_spec(D_MODEL),
        out_shape=jax.ShapeDtypeStruct((n, D_MODEL), F32),
        compiler_params=pltpu.CompilerParams(vmem_limit_bytes=MIX_VMEM_BYTES),
        name="even_out",
    )(x, mod_gate, ya, z, xs, y0, y1, jnp.repeat(d_skip, SSD_HEAD_DIM).reshape(1, SSD_WIDTH),
      norm_w.reshape(1, SSD_WIDTH), w_out.astype(BF))


def _odd_in_body(x_ref, shift_ref, scale_ref, nw_ref, w_ref, q_ref, f_ref, v_ref, g_ref):
    h = _norm_mod(x_ref[...], nw_ref[...], shift_ref[0, 0], scale_ref[0, 0]).astype(BF)
    proj = lambda j: jnp.dot(h, w_ref[:, j * HG_WIDTH:(j + 1) * HG_WIDTH], preferred_element_type=F32)
    q_ref[...] = _silu(proj(0)) * (HG_KDIM ** -0.5)
    f_ref[0] = proj(1)
    f_ref[1] = proj(2)
    v_ref[...] = proj(3)
    g_ref[...] = proj(4)


def odd_in(x, mod_shift, mod_scale, norm_w, w_in, tiles_per_seq, ctx_tiles):
    n = x.shape[0]
    mod = _mod_spec(tiles_per_seq, ctx_tiles)
    tok = _tok_spec(HG_WIDTH)
    sds = jax.ShapeDtypeStruct((n, HG_WIDTH), F32)
    return pl.pallas_call(
        _odd_in_body, grid=(n // TOKEN_TILE,),
        in_specs=[_tok_spec(D_MODEL), mod, mod, _row_spec(D_MODEL), _RESIDENT],
        out_specs=[tok, pl.BlockSpec((2, TOKEN_TILE, HG_WIDTH), lambda i: (0, i, 0)), tok, tok],
        out_shape=[sds, jax.ShapeDtypeStruct((2, n, HG_WIDTH), F32), sds, sds],
        compiler_params=pltpu.CompilerParams(vmem_limit_bytes=MIX_VMEM_BYTES),
        name="odd_in",
    )(x, mod_shift, mod_scale, norm_w.reshape(1, D_MODEL), w_in.astype(BF))


def _odd_out_body(x_ref, gate_ref, o0_ref, o1_ref, g_ref, nw_ref, w_ref, out_ref):
    o = _group_rmsnorm(o0_ref[...] + o1_ref[...], HG_HEADS) * nw_ref[...] * _silu(g_ref[...])
    out_ref[...] = x_ref[...] + gate_ref[0, 0] * jnp.dot(o.astype(BF), w_ref[...], preferred_element_type=F32)


def odd_out(x, mod_gate, o0, o1, g, norm_w, w_out, tiles_per_seq, ctx_tiles):
    n = x.shape[0]
    return pl.pallas_call(
        _odd_out_body, grid=(n // TOKEN_TILE,),
        in_specs=[_tok_spec(D_MODEL), _mod_spec(tiles_per_seq, ctx_tiles), _tok_spec(HG_WIDTH), _tok_spec(HG_WIDTH),
                  _tok_spec(HG_WIDTH), _row_spec(HG_WIDTH), _RESIDENT],
        out_specs=_tok_spec(D_MODEL),
        out_shape=jax.ShapeDtypeStruct((n, D_MODEL), F32),
        compiler_params=pltpu.CompilerParams(vmem_limit_bytes=MIX_VMEM_BYTES),
        name="odd_out",
    )(x, mod_gate, o0, o1, g, norm_w.reshape(1, HG_WIDTH), w_out.astype(BF))


def _top16(vals, ids, n_ids):
    top_s, top_i = [], []
    for r in range(PEER_TOPK):
        m = jnp.max(vals, axis=0, keepdims=True)
        pos = jnp.min(jnp.where(vals == m, ids, float(n_ids)), axis=0, keepdims=True)
        if r + 1 < PEER_TOPK:
            vals = jnp.where(ids == pos, -jnp.inf, vals)
        top_s.append(m)
        top_i.append(pos)
    return jnp.concatenate(top_s, axis=0), jnp.concatenate(top_i, axis=0).astype(jnp.int32)


def _pick_row(table, row):
    r = lax.broadcasted_iota(jnp.int32, table.shape, 0)
    return jnp.sum(jnp.where(r == row, table, 0), axis=0, keepdims=True)


def _route_body(x_ref, shift_ref, scale_ref, nw_ref, wq_ref, keys_ref, h_ref, idx_ref, gate_ref,
                q_scr, idx_scr, gate_scr):
    h = _norm_mod(x_ref[...], nw_ref[...], shift_ref[0, 0], scale_ref[0, 0])
    h_ref[...] = h
    q_scr[...] = jnp.dot(h.astype(BF), wq_ref[...], preferred_element_type=F32)

    def route(hd, tok0):
        tops = []
        for a in range(2):
            col = pl.multiple_of(hd * PEER_QDIM + a * PEER_HALF, PEER_HALF)
            qs = q_scr[pl.ds(tok0, LANES), pl.ds(col, PEER_HALF)].astype(BF)
            sc = lax.dot_general(keys_ref[hd * 2 + a], qs, (((1,), (1,)), ((), ())),
                                 preferred_element_type=F32)
            tops.append(_top16(sc, lax.broadcasted_iota(jnp.int32, sc.shape, 0).astype(F32), PEER_NKEYS))
        (s0, i0), (s1, i1) = tops
        sub = lax.broadcasted_iota(jnp.int32, (8, LANES), 0).astype(F32)
        chunks = [(s0[0:8] + s1[0:1], sub * PEER_TOPK), (s0[8:16] + s1[0:1], (sub + 8) * PEER_TOPK)]
        chunks += [(s0[0:8] + s1[b:b + 1], sub * PEER_TOPK + b) for b in range(1, 8)]
        chunks.append((s0[0:1] + s1[8:16], sub + 8))
        best, pos = _top16(jnp.concatenate([c[0] for c in chunks], axis=0),
                           jnp.concatenate([c[1] for c in chunks], axis=0), PEER_TOPK * PEER_TOPK)
        ids = jnp.concatenate(
            [_pick_row(i0, pos[r:r + 1] >> 4) * PEER_NKEYS + _pick_row(i1, pos[r:r + 1] & (PEER_TOPK - 1))
             for r in range(PEER_TOPK)], axis=0)
        e = jnp.exp(best - best[0:1])
        r0 = pl.multiple_of(hd * PEER_TOPK, PEER_TOPK)
        idx_scr[pl.ds(r0, PEER_TOPK), pl.ds(tok0, LANES)] = ids * ROW_WORDS
        gate_scr[pl.ds(r0, PEER_TOPK), pl.ds(tok0, LANES)] = e / jnp.sum(e, axis=0, keepdims=True)

    def head_group(hg, carry):
        for hh in range(ROUTE_HEADS_PER_STEP):
            for lg in range(TOKEN_TILE // LANES):
                route(hg * ROUTE_HEADS_PER_STEP + hh, lg * LANES)
        return carry

    lax.fori_loop(0, PEER_HEADS // ROUTE_HEADS_PER_STEP, head_group, 0)
    idx_ref[...] = idx_scr[...].T
    gate_ref[...] = gate_scr[...].T


PACK_ROWS = 512


def _pack_body(tab_ref, out_ref):
    bits = pltpu.bitcast(tab_ref[0].astype(BF).astype(F32), jnp.int32)
    for s in range(ROW_WORDS):
        lo = bits[:, 2 * s * LANES:(2 * s + 1) * LANES]
        hi = bits[:, (2 * s + 1) * LANES:(2 * s + 2) * LANES]
        out_ref[pl.ds(s, PACK_ROWS, stride=ROW_WORDS), :] = lax.shift_right_logical(lo, 16) | hi


def pack_table(tabs, layer):
    E = tabs.shape[1]
    return pl.pallas_call(
        _pack_body, grid=(E // PACK_ROWS,),
        in_specs=[pl.BlockSpec((1, PACK_ROWS, D_MODEL), lambda i: (layer, i, 0))],
        out_specs=pl.BlockSpec((PACK_ROWS * ROW_WORDS, LANES), lambda i: (i, 0)),
        out_shape=jax.ShapeDtypeStruct((E * ROW_WORDS, LANES), jnp.int32),
        name="pack_table",
    )(tabs)


def _gathered_blocks(idx_ref, tab_ref, t):
    row = idx_ref.at[t]
    slabs = [tab_ref[pl.ds(pl.multiple_of(row[k], ROW_WORDS), ROW_WORDS), :] for k in range(PEER_SEL)]
    return pltpu.bitcast(jnp.concatenate(slabs, axis=0), BF)


def _own_block_mask():
    shape = (ROW_BLOCKS, ROW_BLOCKS * PEER_SEL)
    return lax.broadcasted_iota(jnp.int32, shape, 1) % ROW_BLOCKS == lax.broadcasted_iota(jnp.int32, shape, 0)


def _peer_u_body(idx_ref, h_ref, gate_ref, tab_ref, act_ref, part_scr):
    own_block = _own_block_mask()

    def step(s, carry):
        for i in range(PEER_U_GROUP):
            t = s * PEER_U_GROUP + i
            hrow = h_ref[pl.ds(t, 1), :]
            lhs = jnp.concatenate([hrow[:, m * LANES:(m + 1) * LANES] for m in range(ROW_BLOCKS)], axis=0).astype(BF)
            prod = lax.dot_general(lhs, _gathered_blocks(idx_ref, tab_ref, t), (((1,), (1,)), ((), ())),
                                   preferred_element_type=F32)
            part_scr[pl.ds(t, 1), :] = jnp.sum(jnp.where(own_block, prod, 0.0), axis=0, keepdims=True)
        return carry

    lax.fori_loop(0, TOKEN_TILE // PEER_U_GROUP, step, 0)
    width = ROW_BLOCKS * PEER_SEL
    fold = (lax.broadcasted_iota(jnp.int32, (width, PEER_SEL), 0) // ROW_BLOCKS
            == lax.broadcasted_iota(jnp.int32, (width, PEER_SEL), 1))
    d = _select_dot(part_scr[...], fold, False)
    act_ref[...] = 0.5 * d * (1.0 + lax.erf(d * (2.0 ** -0.5))) * gate_ref[...]


def _peer_v_body(idx_ref, act_ref, x_ref, gate_ref, tab_ref, fw_ref, out_ref, actx_scr, blk_scr, *, final_norm):
    width = ROW_BLOCKS * PEER_SEL
    spread = (lax.broadcasted_iota(jnp.int32, (PEER_SEL, width), 1) // ROW_BLOCKS
              == lax.broadcasted_iota(jnp.int32, (PEER_SEL, width), 0)).astype(F32).astype(BF)
    actx_scr[...] = jnp.dot(act_ref[...].astype(BF), spread, preferred_element_type=F32)
    own_block = _own_block_mask()

    def step(s, carry):
        for i in range(PEER_V_GROUP):
            t = s * PEER_V_GROUP + i
            lhs = jnp.where(own_block, jnp.broadcast_to(actx_scr[pl.ds(t, 1), :], (ROW_BLOCKS, width)), 0.0).astype(BF)
            blk_scr[pl.ds(pl.multiple_of(t * ROW_BLOCKS, ROW_BLOCKS), ROW_BLOCKS), :] = jnp.dot(
                lhs, _gathered_blocks(idx_ref, tab_ref, t), preferred_element_type=F32)
        return carry

    lax.fori_loop(0, TOKEN_TILE // PEER_V_GROUP, step, 0)
    acc = jnp.concatenate([blk_scr[pl.ds(m, TOKEN_TILE, stride=ROW_BLOCKS), :] for m in range(ROW_BLOCKS)], axis=1)
    y = x_ref[...] + gate_ref[0, 0] * acc
    if final_norm:
        y = y * lax.rsqrt(jnp.mean(y * y, axis=-1, keepdims=True) + NORM_EPS) * fw_ref[...]
    out_ref[...] = y


def peer_ffn(x, x_spec, n, mods, mod_spec, norm_w, wq, keys, u_pack, v_pack, final_norm_w=None):
    grid = (n // TOKEN_TILE,)
    sel_spec = _tok_spec(PEER_SEL)
    h, idx_words, gate = pl.pallas_call(
        _route_body, grid=grid,
        in_specs=[x_spec, mod_spec, mod_spec, _row_spec(D_MODEL),
                  pl.BlockSpec((D_MODEL, PEER_QW), lambda i: (0, 0)),
                  pl.BlockSpec((PEER_HEADS * 2, PEER_NKEYS, PEER_HALF), lambda i: (0, 0, 0))],
        out_specs=[_tok_spec(D_MODEL), sel_spec, sel_spec],
        out_shape=[jax.ShapeDtypeStruct((n, D_MODEL), F32),
                   jax.ShapeDtypeStruct((n, PEER_SEL), jnp.int32),
                   jax.ShapeDtypeStruct((n, PEER_SEL), F32)],
        scratch_shapes=[pltpu.VMEM((TOKEN_TILE, PEER_QW), F32),
                        pltpu.VMEM((PEER_SEL, TOKEN_TILE), jnp.int32),
                        pltpu.VMEM((PEER_SEL, TOKEN_TILE), F32)],
        compiler_params=pltpu.CompilerParams(vmem_limit_bytes=MIX_VMEM_BYTES),
        name="peer_route",
    )(x, mods[3], mods[4], norm_w.reshape(1, D_MODEL), wq.astype(BF),
      keys.reshape(PEER_HEADS * 2, PEER_NKEYS, PEER_HALF).astype(BF))

    params = pltpu.CompilerParams(vmem_limit_bytes=PEER_VMEM_BYTES)
    idx_spec = pl.BlockSpec((TOKEN_TILE, PEER_SEL), lambda i: (i, 0), memory_space=pltpu.SMEM)
    act = pl.pallas_call(
        _peer_u_body, grid=grid,
        in_specs=[idx_spec, _tok_spec(D_MODEL), sel_spec, _RESIDENT],
        out_specs=sel_spec,
        out_shape=jax.ShapeDtypeStruct((n, PEER_SEL), F32),
        scratch_shapes=[pltpu.VMEM((TOKEN_TILE, ROW_BLOCKS * PEER_SEL), F32)],
        compiler_params=params, name="peer_u",
    )(idx_words, h, gate, u_pack)
    fw = jnp.ones((D_MODEL,), F32) if final_norm_w is None else final_norm_w
    return pl.pallas_call(
        functools.partial(_peer_v_body, final_norm=final_norm_w is not None), grid=grid,
        in_specs=[idx_spec, sel_spec, x_spec, mod_spec, _RESIDENT, _row_spec(D_MODEL)],
        out_specs=_tok_spec(D_MODEL),
        out_shape=jax.ShapeDtypeStruct((n, D_MODEL), F32),
        scratch_shapes=[pltpu.VMEM((TOKEN_TILE, ROW_BLOCKS * PEER_SEL), F32),
                        pltpu.VMEM((TOKEN_TILE * ROW_BLOCKS, LANES), F32)],
        compiler_params=params, name="peer_v",
    )(idx_words, act, x, mods[5], v_pack, fw.reshape(1, D_MODEL))


def even_layer(stream, mods, n_seq, tiles_per_seq, ctx_tiles, norm_w, w_in, w_out, sc_conv_w, ssd_conv_w,
               ssd_conv_b, dt_bias, a_log, d_skip, ssd_norm_w):
    ya, z, xs, bm, cm, dt_raw = even_in(stream, mods[0], mods[1], norm_w, w_in, sc_conv_w, ssd_conv_w, ssd_conv_b,
                                        tiles_per_seq, ctx_tiles)
    y0, y1 = ssd_scan(xs, bm, cm, dt_raw, dt_bias, -jnp.exp(a_log), n_seq, ctx_tiles * TOKEN_TILE // SSD_CHUNK)
    return even_out(stream, mods[2], ya, z, xs, y0, y1, d_skip, ssd_norm_w, w_out, tiles_per_seq, ctx_tiles)


def odd_layer(stream, mods, n_seq, tiles_per_seq, ctx_tiles, norm_w, w_in, w_out, lower_bound, hg_norm_w):
    q, f_raw, v, g = odd_in(stream, mods[0], mods[1], norm_w, w_in, tiles_per_seq, ctx_tiles)
    o0, o1 = gla_scan(q, f_raw, v, lower_bound.reshape(2, 1, HG_WIDTH), n_seq, ctx_tiles * TOKEN_TILE // GLA_CHUNK)
    return odd_out(stream, mods[2], o0, o1, g, hg_norm_w, w_out, tiles_per_seq, ctx_tiles)


def kernel(x, c, ctx, c_ctx, ada_w, ada_b, norm_mix_w, norm_ffn_w, norm_f_w, ev_w_in, ev_w_out,
           sc_conv_w, ssd_conv_w, ssd_conv_b, ssd_dt_bias, ssd_a_log, ssd_d, ssd_norm_w,
           od_w_in, od_w_out, hg_lb_logits, hg_norm_w, peer_wq, peer_keys, peer_u, peer_v):
    Bsz, seq, D = x.shape
    ctx_len = ctx.shape[1]
    assert D == D_MODEL and ctx_len == TOKEN_TILE and seq % TOKEN_TILE == 0 and TOKEN_TILE % GRID_W == 0
    lb_sm = jax.nn.softmax(hg_lb_logits.astype(F32), axis=0)
    lower_bounds = jnp.cumsum(lb_sm, axis=0) - lb_sm[0]
    cond = jnp.concatenate([c, c_ctx[None, :], jnp.zeros((7 - Bsz % 8, D), F32)], axis=0)

    ctx_tiles = ctx_len // TOKEN_TILE
    tiles_per_seq = ctx_tiles + seq // TOKEN_TILE
    stream = jnp.concatenate([ctx, x], axis=1).reshape(Bsz * (ctx_len + seq), D)
    for layer in range(DEPTH):
        last = layer == DEPTH - 1
        j = layer // 2
        m = ada_modulation(cond, ada_w[layer], ada_b[layer]).reshape(-1, 6, D)
        mods = [jnp.stack([jnp.broadcast_to(m[Bsz, k], (Bsz, D)), m[:Bsz, k]], axis=1)[:, :, None, :] for k in range(6)]
        if layer % 2 == 0:
            stream = even_layer(stream, mods, Bsz, tiles_per_seq, ctx_tiles, norm_mix_w[layer], ev_w_in[j], ev_w_out[j],
                                sc_conv_w[j], ssd_conv_w[j], ssd_conv_b[j], ssd_dt_bias[j], ssd_a_log[j], ssd_d[j],
                                ssd_norm_w[j])
        else:
            stream = odd_layer(stream, mods, Bsz, tiles_per_seq, ctx_tiles, norm_mix_w[layer], od_w_in[j], od_w_out[j],
                               lower_bounds[layer], hg_norm_w[j])
        u_pack, v_pack = pack_table(peer_u, layer), pack_table(peer_v, layer)
        if not last:
            stream = peer_ffn(stream, _tok_spec(D), stream.shape[0], mods, _mod_spec(tiles_per_seq, ctx_tiles),
                              norm_ffn_w[layer], peer_wq[layer], peer_keys[layer], u_pack, v_pack)
        else:
            lat_tiles = seq // TOKEN_TILE
            lat_spec = pl.BlockSpec((TOKEN_TILE, D),
                                    lambda i: ((i // lat_tiles) * tiles_per_seq + ctx_tiles + i % lat_tiles, 0))
            out = peer_ffn(stream, lat_spec, Bsz * seq, mods, _mod_spec(lat_tiles, 0), norm_ffn_w[layer],
                           peer_wq[layer], peer_keys[layer], u_pack, v_pack, final_norm_w=norm_f_w)
    return out.reshape(Bsz, seq, D)
```

```python
import functools

import jax
import jax.numpy as jnp
from jax import lax
from jax.experimental import pallas as pl
from jax.experimental.pallas import tpu as pltpu

D_MODEL = 1024
DEPTH = 2
GRID_W = 64
CTX_LEN = 256
NORM_EPS = 1e-6
CONV_W = 3
SC_WIDTH = 1024
SSD_HEADS = 16
SSD_HEAD_DIM = 64
SSD_WIDTH = SSD_HEADS * SSD_HEAD_DIM
SSD_GROUPS = 4
SSD_STATE = 128
SSD_BC = SSD_GROUPS * SSD_STATE
SSD_CONV_DIM = SSD_WIDTH + 2 * SSD_BC
HG_HEADS = 8
HG_KDIM = 128
HG_VDIM = 128
HG_WIDTH = HG_HEADS * HG_KDIM
PEER_HEADS = 8
PEER_NKEYS = 128
PEER_QDIM = 256
PEER_HALF = PEER_QDIM // 2
PEER_TOPK = 16
PEER_SEL = PEER_HEADS * PEER_TOPK
PEER_QW = PEER_HEADS * PEER_QDIM

LANES = 128
TOKEN_TILE = 256
SSD_CHUNK = 128
GLA_CHUNK = 64
ROW_WORDS = D_MODEL // (2 * LANES)
ROW_BLOCKS = 2 * ROW_WORDS
PEER_U_GROUP = 32
PEER_V_GROUP = 32
ROUTE_HEADS_PER_STEP = 4
MIB = 1024 * 1024
PEER_VMEM_BYTES = 48 * MIB
MIX_VMEM_BYTES = 48 * MIB
BF = jnp.bfloat16
F32 = jnp.float32


def _mod_spec(tiles_per_seq, ctx_tiles):
    return pl.BlockSpec((1, 1, 1, D_MODEL),
                        lambda i: (i // tiles_per_seq, jnp.where(i % tiles_per_seq < ctx_tiles, 0, 1), 0, 0))


def _tok_spec(width):
    return pl.BlockSpec((TOKEN_TILE, width), lambda i: (i, 0))


def _row_spec(width):
    return pl.BlockSpec((1, width), lambda i: (0, 0))


_RESIDENT = pl.BlockSpec(memory_space=pltpu.VMEM)


def _norm_mod(x, nw, shift, scale):
    y = x * lax.rsqrt(jnp.mean(x * x, axis=-1, keepdims=True) + NORM_EPS) * nw
    return y * (1.0 + scale) + shift


def _silu(x):
    return x * jax.nn.sigmoid(x)


def _select_dot(a, b, select_lhs):
    sel, x = (a, b) if select_lhs else (b, a)
    sel = sel.astype(F32).astype(BF)
    hi = x.astype(BF)
    rest = x - hi.astype(F32)
    mid = rest.astype(BF)
    parts = (hi, mid, (rest - mid.astype(F32)).astype(BF))
    if select_lhs:
        return sum(jnp.dot(sel, p, preferred_element_type=F32) for p in parts)
    return sum(jnp.dot(p, sel, preferred_element_type=F32) for p in parts)


def _ada_body(c_ref, w_ref, b_ref, o_ref):
    cond = _silu(c_ref[...]).astype(BF)
    o_ref[...] = jnp.dot(cond, w_ref[...].astype(BF), preferred_element_type=F32) + b_ref[...]


def ada_modulation(cond, w, b):
    rows = cond.shape[0]
    return pl.pallas_call(
        _ada_body, grid=(6,),
        in_specs=[pl.BlockSpec((rows, D_MODEL), lambda j: (0, 0)),
                  pl.BlockSpec((D_MODEL, D_MODEL), lambda j: (0, j)),
                  pl.BlockSpec((1, D_MODEL), lambda j: (0, j))],
        out_specs=pl.BlockSpec((rows, D_MODEL), lambda j: (0, j)),
        out_shape=jax.ShapeDtypeStruct((rows, 6 * D_MODEL), F32),
        name="ada_modulation",
    )(cond, w, b.reshape(1, 6 * D_MODEL))


def _even_in_body(x_ref, shift_ref, scale_ref, nw_ref, w_ref, cw_sc_ref, cw_ssd_ref, cb_ssd_ref,
                  ya_ref, z_ref, xs_ref, bm_ref, cm_ref, dt_ref, *, tiles_per_seq, ctx_tiles):
    T = TOKEN_TILE
    h = _norm_mod(x_ref[...], nw_ref[...], shift_ref[0, 0], scale_ref[0, 0]).astype(BF)
    is_ctx = (pl.program_id(0) % tiles_per_seq) < ctx_tiles
    row_len = jnp.where(is_ctx, T, GRID_W)
    pos = lax.broadcasted_iota(jnp.int32, (T, 1), 0) & (row_len - 1)
    first, last = pos == 0, pos == row_len - 1

    def conv(u, cw_ref):
        prev = jnp.where(first, 0.0, pltpu.roll(u, 1, 0))
        nxt = jnp.where(last, 0.0, pltpu.roll(u, T - 1, 0))
        return prev * cw_ref[0:1, :] + u * cw_ref[1:2, :] + nxt * cw_ref[2:3, :]

    def proj(lo, width):
        return jnp.dot(h, w_ref[:, lo:lo + width], preferred_element_type=F32)

    sc_x, sc_b, sc_c = proj(0, SC_WIDTH), proj(SC_WIDTH, SC_WIDTH), proj(2 * SC_WIDTH, SC_WIDTH)
    ya_ref[...] = sc_b * conv(sc_c * sc_x, cw_sc_ref)
    z_ref[...] = proj(3 * SC_WIDTH, SSD_WIDTH)
    xbc = _silu(conv(proj(3 * SC_WIDTH + SSD_WIDTH, SSD_CONV_DIM), cw_ssd_ref) + cb_ssd_ref[...])
    xs_ref[...] = xbc[:, :SSD_WIDTH]
    bm_ref[...] = xbc[:, SSD_WIDTH:SSD_WIDTH + SSD_BC]
    cm_ref[...] = xbc[:, SSD_WIDTH + SSD_BC:]
    dt_ref[...] = proj(3 * SC_WIDTH + SSD_WIDTH + SSD_CONV_DIM, 2 * LANES)


def even_in(x, mod_shift, mod_scale, norm_w, w_in, sc_conv_w, ssd_conv_w, ssd_conv_b, tiles_per_seq, ctx_tiles):
    n = x.shape[0]
    main = 3 * SC_WIDTH + SSD_WIDTH + SSD_CONV_DIM
    w_dt = jnp.zeros((D_MODEL, 2, LANES), F32).at[:, :, :SSD_HEADS].set(w_in[:, main:].reshape(D_MODEL, 2, SSD_HEADS))
    w = jnp.concatenate([w_in[:, :main], w_dt.reshape(D_MODEL, 2 * LANES)], axis=1).astype(BF)
    mod = _mod_spec(tiles_per_seq, ctx_tiles)
    widths = (SC_WIDTH, SSD_WIDTH, SSD_WIDTH, SSD_BC, SSD_BC, 2 * LANES)
    return pl.pallas_call(
        functools.partial(_even_in_body, tiles_per_seq=tiles_per_seq, ctx_tiles=ctx_tiles),
        grid=(n // TOKEN_TILE,),
        in_specs=[_tok_spec(D_MODEL), mod, mod, _row_spec(D_MODEL), _RESIDENT,
                  pl.BlockSpec((CONV_W, SC_WIDTH), lambda i: (0, 0)),
                  pl.BlockSpec((CONV_W, SSD_CONV_DIM), lambda i: (0, 0)), _row_spec(SSD_CONV_DIM)],
        out_specs=[_tok_spec(wd) for wd in widths],
        out_shape=[jax.ShapeDtypeStruct((n, wd), F32) for wd in widths],
        compiler_params=pltpu.CompilerParams(vmem_limit_bytes=MIX_VMEM_BYTES),
        name="even_in",
    )(x, mod_shift, mod_scale, norm_w.reshape(1, D_MODEL), w, sc_conv_w, ssd_conv_w,
      ssd_conv_b.reshape(1, SSD_CONV_DIM))


def _scan_row(n_chunks, ctx_chunks):
    def row(b, d, c):
        if d == 0:
            return b * n_chunks + c
        return b * n_chunks + jnp.where(c < ctx_chunks, ctx_chunks - 1 - c, n_chunks - 1 + ctx_chunks - c)
    return row


def _causal_mask(L, d):
    r = lax.broadcasted_iota(jnp.int32, (L, L), 0)
    c = lax.broadcasted_iota(jnp.int32, (L, L), 1)
    return c <= r if d == 0 else c >= r


SSD_GW = (SSD_HEADS // SSD_GROUPS) * SSD_HEAD_DIM


def _alternate(*phased):
    live = list(phased)
    while live:
        live = [g for g in live if next(g, StopIteration) is not StopIteration]


def _ssd_chunk(d, xs_ref, bm_ref, cm_ref, dt_ref, bias, a, y_ref, state):
    L = SSD_CHUNK
    mask = _causal_mask(L, d)
    v = dt_ref[...] + bias
    dt = jnp.maximum(v, 0.0) + jnp.log1p(jnp.exp(-jnp.abs(v)))
    la = _select_dot(mask, dt * a, True)
    la_t = la.T
    head_of_lane = lax.broadcasted_iota(jnp.int32, (LANES, SSD_WIDTH), 1) // SSD_HEAD_DIM
    expand = lax.broadcasted_iota(jnp.int32, (LANES, SSD_WIDTH), 0) == head_of_lane
    dt_x = _select_dot(dt, expand, False)
    la_x = _select_dot(la, expand, False)
    end_x = la_x[L - 1:L] if d == 0 else la_x[0:1]
    xdt = xs_ref[...] * dt_x
    w_state = (xdt * jnp.exp(end_x - la_x)).astype(BF)
    xdt_b = xdt.astype(BF)
    e_la = jnp.exp(la_x)
    dec = jnp.exp(end_x)
    yield
    for g in range(SSD_GROUPS):
        bm = bm_ref[:, g * SSD_STATE:(g + 1) * SSD_STATE].astype(BF)
        cm = cm_ref[:, g * SSD_STATE:(g + 1) * SSD_STATE].astype(BF)
        cb = lax.dot_general(cm, bm, (((1,), (1,)), ((), ())), preferred_element_type=F32)
        lanes = slice(g * SSD_GW, (g + 1) * SSD_GW)
        s_old = state[g]
        y_in = jnp.dot(cm, s_old.astype(BF), preferred_element_type=F32) * e_la[:, lanes]
        ys = []
        for j in range(SSD_HEADS // SSD_GROUPS):
            h = g * (SSD_HEADS // SSD_GROUPS) + j
            seg = la[:, h:h + 1] - la_t[h:h + 1, :]
            m = (cb * jnp.exp(jnp.where(mask, seg, -jnp.inf))).astype(BF)
            ys.append(jnp.dot(m, xdt_b[:, h * SSD_HEAD_DIM:(h + 1) * SSD_HEAD_DIM], preferred_element_type=F32))
        y_ref[:, lanes] = y_in + jnp.concatenate(ys, axis=1)
        state[g] = dec[:, lanes] * s_old + jnp.dot(bm.T, w_state[:, lanes], preferred_element_type=F32)
        yield


def _ssd_body(xs0_ref, bm0_ref, cm0_ref, dt0_ref, xs1_ref, bm1_ref, cm1_ref, dt1_ref, bias_ref, a_ref,
              y0_ref, y1_ref, state):
    @pl.when(pl.program_id(1) == 0)
    def _():
        state[...] = jnp.zeros_like(state)

    _alternate(_ssd_chunk(0, xs0_ref, bm0_ref, cm0_ref, dt0_ref, bias_ref[0], a_ref[0], y0_ref, state.at[0]),
               _ssd_chunk(1, xs1_ref, bm1_ref, cm1_ref, dt1_ref, bias_ref[1], a_ref[1], y1_ref, state.at[1]))


def ssd_scan(xs, bm, cm, dt_raw, dt_bias, a, n_seq, ctx_chunks):
    n = xs.shape[0]
    n_chunks = n // n_seq // SSD_CHUNK
    row = _scan_row(n_chunks, ctx_chunks)
    pad = lambda p: jnp.zeros((2, 1, LANES), F32).at[:, 0, :SSD_HEADS].set(p)
    tok = lambda d, width: pl.BlockSpec((SSD_CHUNK, width), lambda b, c: (row(b, d, c), 0))
    ins = lambda d: [tok(d, SSD_WIDTH), tok(d, SSD_BC), tok(d, SSD_BC),
                     pl.BlockSpec((SSD_CHUNK, LANES), lambda b, c: (row(b, d, c), d))]
    per_dir = pl.BlockSpec((2, 1, LANES), lambda b, c: (0, 0, 0))
    sds = jax.ShapeDtypeStruct((n, SSD_WIDTH), F32)
    return pl.pallas_call(
        _ssd_body, grid=(n_seq, n_chunks),
        in_specs=ins(0) + ins(1) + [per_dir, per_dir],
        out_specs=[tok(0, SSD_WIDTH), tok(1, SSD_WIDTH)],
        out_shape=[sds, sds],
        scratch_shapes=[pltpu.VMEM((2, SSD_GROUPS, SSD_STATE, SSD_GW), F32)],
        compiler_params=pltpu.CompilerParams(dimension_semantics=("arbitrary", "arbitrary")),
        name="ssd_scan",
    )(xs, bm, cm, dt_raw, xs, bm, cm, dt_raw, pad(dt_bias), pad(a))


def _gla_chunk(d, q_ref, f_ref, v_ref, lb, o_ref, state):
    L = GLA_CHUNK
    mask = _causal_mask(L, d)
    f = lb + (1.0 - lb) * jax.nn.sigmoid(f_ref[0])
    gc = _select_dot(mask, jnp.log(f), True)
    g_end = gc[L - 1:L] if d == 0 else gc[0:1]
    k = 1.0 - f
    qg = (q_ref[...] * jnp.exp(gc)).astype(BF)
    kg = (k * jnp.exp(-gc)).astype(BF)
    k_end = (k * jnp.exp(g_end - gc)).astype(BF)
    dec = jnp.exp(g_end)
    vb = v_ref[...].astype(BF)
    yield
    for h in range(HG_HEADS):
        lanes = slice(h * HG_KDIM, (h + 1) * HG_KDIM)
        att = lax.dot_general(qg[:, lanes], kg[:, lanes], (((1,), (1,)), ((), ())), preferred_element_type=F32)
        att = jnp.where(mask, att, 0.0).astype(BF)
        s_old = state[h]
        o = jnp.dot(att, vb[:, lanes], preferred_element_type=F32)
        o_ref[:, lanes] = o + lax.dot_general(qg[:, lanes], s_old.astype(BF), (((1,), (1,)), ((), ())),
                                              preferred_element_type=F32)
        state[h] = s_old * dec[:, lanes] + jnp.dot(vb[:, lanes].T, k_end[:, lanes], preferred_element_type=F32)
        yield


def _gla_body(q0_ref, f0_ref, v0_ref, q1_ref, f1_ref, v1_ref, lb_ref, o0_ref, o1_ref, state):
    @pl.when(pl.program_id(1) == 0)
    def _():
        state[...] = jnp.zeros_like(state)

    _alternate(_gla_chunk(0, q0_ref, f0_ref, v0_ref, lb_ref[0], o0_ref, state.at[0]),
               _gla_chunk(1, q1_ref, f1_ref, v1_ref, lb_ref[1], o1_ref, state.at[1]))


def gla_scan(q, f_raw, v, lb, n_seq, ctx_chunks):
    n = q.shape[0]
    n_chunks = n // n_seq // GLA_CHUNK
    row = _scan_row(n_chunks, ctx_chunks)
    tok = lambda d: pl.BlockSpec((GLA_CHUNK, HG_WIDTH), lambda b, c: (row(b, d, c), 0))
    ins = lambda d: [tok(d), pl.BlockSpec((1, GLA_CHUNK, HG_WIDTH), lambda b, c: (d, row(b, d, c), 0)), tok(d)]
    sds = jax.ShapeDtypeStruct((n, HG_WIDTH), F32)
    return pl.pallas_call(
        _gla_body, grid=(n_seq, n_chunks),
        in_specs=ins(0) + ins(1) + [pl.BlockSpec((2, 1, HG_WIDTH), lambda b, c: (0, 0, 0))],
        out_specs=[tok(0), tok(1)],
        out_shape=[sds, sds],
        scratch_shapes=[pltpu.VMEM((2, HG_HEADS, HG_VDIM, HG_KDIM), F32)],
        compiler_params=pltpu.CompilerParams(dimension_semantics=("arbitrary", "arbitrary")),
        name="gla_scan",
    )(q, f_raw, v, q, f_raw, v, lb)


def _group_rmsnorm(y, groups):
    width = y.shape[1] // groups
    parts = []
    for g in range(groups):
        seg = y[:, g * width:(g + 1) * width]
        parts.append(seg * lax.rsqrt(jnp.mean(seg * seg, axis=-1, keepdims=True) + NORM_EPS))
    return jnp.concatenate(parts, axis=1)


def _even_out_body(x_ref, gate_ref, ya_ref, z_ref, xs_ref, y0_ref, y1_ref, dskip_ref, nw_ref, w_ref, o_ref):
    yb = (y0_ref[...] + y1_ref[...] + dskip_ref[...] * xs_ref[...]) * _silu(z_ref[...])
    yb = _group_rmsnorm(yb, SSD_GROUPS) * nw_ref[...]
    mix = jnp.concatenate([ya_ref[...], yb], axis=1).astype(BF)
    o_ref[...] = x_ref[...] + gate_ref[0, 0] * jnp.dot(mix, w_ref[...], preferred_element_type=F32)


def even_out(x, mod_gate, ya, z, xs, y0, y1, d_skip, norm_w, w_out, tiles_per_seq, ctx_tiles):
    n = x.shape[0]
    return pl.pallas_call(
        _even_out_body, grid=(n // TOKEN_TILE,),
        in_specs=[_tok_spec(D_MODEL), _mod_spec(tiles_per_seq, ctx_tiles), _tok_spec(SC_WIDTH), _tok_spec(SSD_WIDTH),
                  _tok_spec(SSD_WIDTH), _tok_spec(SSD_WIDTH), _tok_spec(SSD_WIDTH), _row_spec(SSD_WIDTH),
                  _row_spec(SSD_WIDTH), _RESIDENT],
        out_specs=_tok_spec(D_MODEL),
        out_shape=jax.ShapeDtypeStruct((n, D_MODEL), F32),
        compiler_params=pltpu.CompilerParams(vmem_limit_bytes=MIX_VMEM_BYTES),
        name="even_out",
    )(x, mod_gate, ya, z, xs, y0, y1, jnp.repeat(d_skip, SSD_HEAD_DIM).reshape(1, SSD_WIDTH),
      norm_w.reshape(1, SSD_WIDTH), w_out.astype(BF))


def _odd_in_body(x_ref, shift_ref, scale_ref, nw_ref, w_ref, q_ref, f_ref, v_ref, g_ref):
    h = _norm_mod(x_ref[...], nw_ref[...], shift_ref[0, 0], scale_ref[0, 0]).astype(BF)
    proj = lambda j: jnp.dot(h, w_ref[:, j * HG_WIDTH:(j + 1) * HG_WIDTH], preferred_element_type=F32)
    q_ref[...] = _silu(proj(0)) * (HG_KDIM ** -0.5)
    f_ref[0] = proj(1)
    f_ref[1] = proj(2)
    v_ref[...] = proj(3)
    g_ref[...] = proj(4)


def odd_in(x, mod_shift, mod_scale, norm_w, w_in, tiles_per_seq, ctx_tiles):
    n = x.shape[0]
    mod = _mod_spec(tiles_per_seq, ctx_tiles)
    tok = _tok_spec(HG_WIDTH)
    sds = jax.ShapeDtypeStruct((n, HG_WIDTH), F32)
    return pl.pallas_call(
        _odd_in_body, grid=(n // TOKEN_TILE,),
        in_specs=[_tok_spec(D_MODEL), mod, mod, _row_spec(D_MODEL), _RESIDENT],
        out_specs=[tok, pl.BlockSpec((2, TOKEN_TILE, HG_WIDTH), lambda i: (0, i, 0)), tok, tok],
        out_shape=[sds, jax.ShapeDtypeStruct((2, n, HG_WIDTH), F32), sds, sds],
        compiler_params=pltpu.CompilerParams(vmem_limit_bytes=MIX_VMEM_BYTES),
        name="odd_in",
    )(x, mod_shift, mod_scale, norm_w.reshape(1, D_MODEL), w_in.astype(BF))


def _odd_out_body(x_ref, gate_ref, o0_ref, o1_ref, g_ref, nw_ref, w_ref, out_ref):
    o = _group_rmsnorm(o0_ref[...] + o1_ref[...], HG_HEADS) * nw_ref[...] * _silu(g_ref[...])
    out_ref[...] = x_ref[...] + gate_ref[0, 0] * jnp.dot(o.astype(BF), w_ref[...], preferred_element_type=F32)


def odd_out(x, mod_gate, o0, o1, g, norm_w, w_out, tiles_per_seq, ctx_tiles):
    n = x.shape[0]
    return pl.pallas_call(
        _odd_out_body, grid=(n // TOKEN_TILE,),
        in_specs=[_tok_spec(D_MODEL), _mod_spec(tiles_per_seq, ctx_tiles), _tok_spec(HG_WIDTH), _tok_spec(HG_WIDTH),
                  _tok_spec(HG_WIDTH), _row_spec(HG_WIDTH), _RESIDENT],
        out_specs=_tok_spec(D_MODEL),
        out_shape=jax.ShapeDtypeStruct((n, D_MODEL), F32),
        compiler_params=pltpu.CompilerParams(vmem_limit_bytes=MIX_VMEM_BYTES),
        name="odd_out",
    )(x, mod_gate, o0, o1, g, norm_w.reshape(1, HG_WIDTH), w_out.astype(BF))


def _top16(vals, ids, n_ids):
    top_s, top_i = [], []
    for r in range(PEER_TOPK):
        m = jnp.max(vals, axis=0, keepdims=True)
        pos = jnp.min(jnp.where(vals == m, ids, float(n_ids)), axis=0, keepdims=True)
        if r + 1 < PEER_TOPK:
            vals = jnp.where(ids == pos, -jnp.inf, vals)
        top_s.append(m)
        top_i.append(pos)
    return jnp.concatenate(top_s, axis=0), jnp.concatenate(top_i, axis=0).astype(jnp.int32)


def _pick_row(table, row):
    r = lax.broadcasted_iota(jnp.int32, table.shape, 0)
    return jnp.sum(jnp.where(r == row, table, 0), axis=0, keepdims=True)


def _route_body(x_ref, shift_ref, scale_ref, nw_ref, wq_ref, keys_ref, h_ref, idx_ref, gate_ref,
                q_scr, idx_scr, gate_scr):
    h = _norm_mod(x_ref[...], nw_ref[...], shift_ref[0, 0], scale_ref[0, 0])
    h_ref[...] = h
    q_scr[...] = jnp.dot(h.astype(BF), wq_ref[...], preferred_element_type=F32)

    def route(hd, tok0):
        tops = []
        for a in range(2):
            col = pl.multiple_of(hd * PEER_QDIM + a * PEER_HALF, PEER_HALF)
            qs = q_scr[pl.ds(tok0, LANES), pl.ds(col, PEER_HALF)].astype(BF)
            sc = lax.dot_general(keys_ref[hd * 2 + a], qs, (((1,), (1,)), ((), ())),
                                 preferred_element_type=F32)
            tops.append(_top16(sc, lax.broadcasted_iota(jnp.int32, sc.shape, 0).astype(F32), PEER_NKEYS))
        (s0, i0), (s1, i1) = tops
        sub = lax.broadcasted_iota(jnp.int32, (8, LANES), 0).astype(F32)
        chunks = [(s0[0:8] + s1[0:1], sub * PEER_TOPK), (s0[8:16] + s1[0:1], (sub + 8) * PEER_TOPK)]
        chunks += [(s0[0:8] + s1[b:b + 1], sub * PEER_TOPK + b) for b in range(1, 8)]
        chunks.append((s0[0:1] + s1[8:16], sub + 8))
        best, pos = _top16(jnp.concatenate([c[0] for c in chunks], axis=0),
                           jnp.concatenate([c[1] for c in chunks], axis=0), PEER_TOPK * PEER_TOPK)
        ids = jnp.concatenate(
            [_pick_row(i0, pos[r:r + 1] >> 4) * PEER_NKEYS + _pick_row(i1, pos[r:r + 1] & (PEER_TOPK - 1))
             for r in range(PEER_TOPK)], axis=0)
        e = jnp.exp(best - best[0:1])
        r0 = pl.multiple_of(hd * PEER_TOPK, PEER_TOPK)
        idx_scr[pl.ds(r0, PEER_TOPK), pl.ds(tok0, LANES)] = ids * ROW_WORDS
        gate_scr[pl.ds(r0, PEER_TOPK), pl.ds(tok0, LANES)] = e / jnp.sum(e, axis=0, keepdims=True)

    def head_group(hg, carry):
        for hh in range(ROUTE_HEADS_PER_STEP):
            for lg in range(TOKEN_TILE // LANES):
                route(hg * ROUTE_HEADS_PER_STEP + hh, lg * LANES)
        return carry

    lax.fori_loop(0, PEER_HEADS // ROUTE_HEADS_PER_STEP, head_group, 0)
    idx_ref[...] = idx_scr[...].T
    gate_ref[...] = gate_scr[...].T


PACK_ROWS = 512


def _pack_body(tab_ref, out_ref):
    bits = pltpu.bitcast(tab_ref[0].astype(BF).astype(F32), jnp.int32)
    for s in range(ROW_WORDS):
        lo = bits[:, 2 * s * LANES:(2 * s + 1) * LANES]
        hi = bits[:, (2 * s + 1) * LANES:(2 * s + 2) * LANES]
        out_ref[pl.ds(s, PACK_ROWS, stride=ROW_WORDS), :] = lax.shift_right_logical(lo, 16) | hi


def pack_table(tabs, layer):
    E = tabs.shape[1]
    return pl.pallas_call(
        _pack_body, grid=(E // PACK_ROWS,),
        in_specs=[pl.BlockSpec((1, PACK_ROWS, D_MODEL), lambda i: (layer, i, 0))],
        out_specs=pl.BlockSpec((PACK_ROWS * ROW_WORDS, LANES), lambda i: (i, 0)),
        out_shape=jax.ShapeDtypeStruct((E * ROW_WORDS, LANES), jnp.int32),
        name="pack_table",
    )(tabs)


def _gathered_blocks(idx_ref, tab_ref, t):
    row = idx_ref.at[t]
    slabs = [tab_ref[pl.ds(pl.multiple_of(row[k], ROW_WORDS), ROW_WORDS), :] for k in range(PEER_SEL)]
    return pltpu.bitcast(jnp.concatenate(slabs, axis=0), BF)


def _own_block_mask():
    shape = (ROW_BLOCKS, ROW_BLOCKS * PEER_SEL)
    return lax.broadcasted_iota(jnp.int32, shape, 1) % ROW_BLOCKS == lax.broadcasted_iota(jnp.int32, shape, 0)


def _peer_u_body(idx_ref, h_ref, gate_ref, tab_ref, act_ref, part_scr):
    own_block = _own_block_mask()

    def step(s, carry):
        for i in range(PEER_U_GROUP):
            t = s * PEER_U_GROUP + i
            hrow = h_ref[pl.ds(t, 1), :]
            lhs = jnp.concatenate([hrow[:, m * LANES:(m + 1) * LANES] for m in range(ROW_BLOCKS)], axis=0).astype(BF)
            prod = lax.dot_general(lhs, _gathered_blocks(idx_ref, tab_ref, t), (((1,), (1,)), ((), ())),
                                   preferred_element_type=F32)
            part_scr[pl.ds(t, 1), :] = jnp.sum(jnp.where(own_block, prod, 0.0), axis=0, keepdims=True)
        return carry

    lax.fori_loop(0, TOKEN_TILE // PEER_U_GROUP, step, 0)
    width = ROW_BLOCKS * PEER_SEL
    fold = (lax.broadcasted_iota(jnp.int32, (width, PEER_SEL), 0) // ROW_BLOCKS
            == lax.broadcasted_iota(jnp.int32, (width, PEER_SEL), 1))
    d = _select_dot(part_scr[...], fold, False)
    act_ref[...] = 0.5 * d * (1.0 + lax.erf(d * (2.0 ** -0.5))) * gate_ref[...]


def _peer_v_body(idx_ref, act_ref, x_ref, gate_ref, tab_ref, fw_ref, out_ref, actx_scr, blk_scr, *, final_norm):
    width = ROW_BLOCKS * PEER_SEL
    spread = (lax.broadcasted_iota(jnp.int32, (PEER_SEL, width), 1) // ROW_BLOCKS
              == lax.broadcasted_iota(jnp.int32, (PEER_SEL, width), 0)).astype(F32).astype(BF)
    actx_scr[...] = jnp.dot(act_ref[...].astype(BF), spread, preferred_element_type=F32)
    own_block = _own_block_mask()

    def step(s, carry):
        for i in range(PEER_V_GROUP):
            t = s * PEER_V_GROUP + i
            lhs = jnp.where(own_block, jnp.broadcast_to(actx_scr[pl.ds(t, 1), :], (ROW_BLOCKS, width)), 0.0).astype(BF)
            blk_scr[pl.ds(pl.multiple_of(t * ROW_BLOCKS, ROW_BLOCKS), ROW_BLOCKS), :] = jnp.dot(
                lhs, _gathered_blocks(idx_ref, tab_ref, t), preferred_element_type=F32)
        return carry

    lax.fori_loop(0, TOKEN_TILE // PEER_V_GROUP, step, 0)
    acc = jnp.concatenate([blk_scr[pl.ds(m, TOKEN_TILE, stride=ROW_BLOCKS), :] for m in range(ROW_BLOCKS)], axis=1)
    y = x_ref[...] + gate_ref[0, 0] * acc
    if final_norm:
        y = y * lax.rsqrt(jnp.mean(y * y, axis=-1, keepdims=True) + NORM_EPS) * fw_ref[...]
    out_ref[...] = y


def peer_ffn(x, x_spec, n, mods, mod_spec, norm_w, wq, keys, u_pack, v_pack, final_norm_w=None):
    grid = (n // TOKEN_TILE,)
    sel_spec = _tok_spec(PEER_SEL)
    h, idx_words, gate = pl.pallas_call(
        _route_body, grid=grid,
        in_specs=[x_spec, mod_spec, mod_spec, _row_spec(D_MODEL),
                  pl.BlockSpec((D_MODEL, PEER_QW), lambda i: (0, 0)),
                  pl.BlockSpec((PEER_HEADS * 2, PEER_NKEYS, PEER_HALF), lambda i: (0, 0, 0))],
        out_specs=[_tok_spec(D_MODEL), sel_spec, sel_spec],
        out_shape=[jax.ShapeDtypeStruct((n, D_MODEL), F32),
                   jax.ShapeDtypeStruct((n, PEER_SEL), jnp.int32),
                   jax.ShapeDtypeStruct((n, PEER_SEL), F32)],
        scratch_shapes=[pltpu.VMEM((TOKEN_TILE, PEER_QW), F32),
                        pltpu.VMEM((PEER_SEL, TOKEN_TILE), jnp.int32),
                        pltpu.VMEM((PEER_SEL, TOKEN_TILE), F32)],
        compiler_params=pltpu.CompilerParams(vmem_limit_bytes=MIX_VMEM_BYTES),
        name="peer_route",
    )(x, mods[3], mods[4], norm_w.reshape(1, D_MODEL), wq.astype(BF),
      keys.reshape(PEER_HEADS * 2, PEER_NKEYS, PEER_HALF).astype(BF))

    params = pltpu.CompilerParams(vmem_limit_bytes=PEER_VMEM_BYTES)
    idx_spec = pl.BlockSpec((TOKEN_TILE, PEER_SEL), lambda i: (i, 0), memory_space=pltpu.SMEM)
    act = pl.pallas_call(
        _peer_u_body, grid=grid,
        in_specs=[idx_spec, _tok_spec(D_MODEL), sel_spec, _RESIDENT],
        out_specs=sel_spec,
        out_shape=jax.ShapeDtypeStruct((n, PEER_SEL), F32),
        scratch_shapes=[pltpu.VMEM((TOKEN_TILE, ROW_BLOCKS * PEER_SEL), F32)],
        compiler_params=params, name="peer_u",
    )(idx_words, h, gate, u_pack)
    fw = jnp.ones((D_MODEL,), F32) if final_norm_w is None else final_norm_w
    return pl.pallas_call(
        functools.partial(_peer_v_body, final_norm=final_norm_w is not None), grid=grid,
        in_specs=[idx_spec, sel_spec, x_spec, mod_spec, _RESIDENT, _row_spec(D_MODEL)],
        out_specs=_tok_spec(D_MODEL),
        out_shape=jax.ShapeDtypeStruct((n, D_MODEL), F32),
        scratch_shapes=[pltpu.VMEM((TOKEN_TILE, ROW_BLOCKS * PEER_SEL), F32),
                        pltpu.VMEM((TOKEN_TILE * ROW_BLOCKS, LANES), F32)],
        compiler_params=params, name="peer_v",
    )(idx_words, act, x, mods[5], v_pack, fw.reshape(1, D_MODEL))


def even_layer(stream, mods, n_seq, tiles_per_seq, ctx_tiles, norm_w, w_in, w_out, sc_conv_w, ssd_conv_w,
               ssd_conv_b, dt_bias, a_log, d_skip, ssd_norm_w):
    ya, z, xs, bm, cm, dt_raw = even_in(stream, mods[0], mods[1], norm_w, w_in, sc_conv_w, ssd_conv_w, ssd_conv_b,
                                        tiles_per_seq, ctx_tiles)
    y0, y1 = ssd_scan(xs, bm, cm, dt_raw, dt_bias, -jnp.exp(a_log), n_seq, ctx_tiles * TOKEN_TILE // SSD_CHUNK)
    return even_out(stream, mods[2], ya, z, xs, y0, y1, d_skip, ssd_norm_w, w_out, tiles_per_seq, ctx_tiles)


def odd_layer(stream, mods, n_seq, tiles_per_seq, ctx_tiles, norm_w, w_in, w_out, lower_bound, hg_norm_w):
    q, f_raw, v, g = odd_in(stream, mods[0], mods[1], norm_w, w_in, tiles_per_seq, ctx_tiles)
    o0, o1 = gla_scan(q, f_raw, v, lower_bound.reshape(2, 1, HG_WIDTH), n_seq, ctx_tiles * TOKEN_TILE // GLA_CHUNK)
    return odd_out(stream, mods[2], o0, o1, g, hg_norm_w, w_out, tiles_per_seq, ctx_tiles)


def kernel(x, c, ctx, c_ctx, ada_w, ada_b, norm_mix_w, norm_ffn_w, norm_f_w, ev_w_in, ev_w_out,
           sc_conv_w, ssd_conv_w, ssd_conv_b, ssd_dt_bias, ssd_a_log, ssd_d, ssd_norm_w,
           od_w_in, od_w_out, hg_lb_logits, hg_norm_w, peer_wq, peer_keys, peer_u, peer_v):
    Bsz, seq, D = x.shape
    ctx_len = ctx.shape[1]
    assert D == D_MODEL and ctx_len == TOKEN_TILE and seq % TOKEN_TILE == 0 and TOKEN_TILE % GRID_W == 0
    lb_sm = jax.nn.softmax(hg_lb_logits.astype(F32), axis=0)
    lower_bounds = jnp.cumsum(lb_sm, axis=0) - lb_sm[0]
    cond = jnp.concatenate([c, c_ctx[None, :], jnp.zeros((7 - Bsz % 8, D), F32)], axis=0)

    ctx_tiles = ctx_len // TOKEN_TILE
    tiles_per_seq = ctx_tiles + seq // TOKEN_TILE
    stream = jnp.concatenate([ctx, x], axis=1).reshape(Bsz * (ctx_len + seq), D)
    for layer in range(DEPTH):
        last = layer == DEPTH - 1
        j = layer // 2
        m = ada_modulation(cond, ada_w[layer], ada_b[layer]).reshape(-1, 6, D)
        mods = [jnp.stack([jnp.broadcast_to(m[Bsz, k], (Bsz, D)), m[:Bsz, k]], axis=1)[:, :, None, :] for k in range(6)]
        if layer % 2 == 0:
            stream = even_layer(stream, mods, Bsz, tiles_per_seq, ctx_tiles, norm_mix_w[layer], ev_w_in[j], ev_w_out[j],
                                sc_conv_w[j], ssd_conv_w[j], ssd_conv_b[j], ssd_dt_bias[j], ssd_a_log[j], ssd_d[j],
                                ssd_norm_w[j])
        else:
            stream = odd_layer(stream, mods, Bsz, tiles_per_seq, ctx_tiles, norm_mix_w[layer], od_w_in[j], od_w_out[j],
                               lower_bounds[layer], hg_norm_w[j])
        u_pack, v_pack = pack_table(peer_u, layer), pack_table(peer_v, layer)
        if not last:
            stream = peer_ffn(stream, _tok_spec(D), stream.shape[0], mods, _mod_spec(tiles_per_seq, ctx_tiles),
                              norm_ffn_w[layer], peer_wq[layer], peer_keys[layer], u_pack, v_pack)
        else:
            lat_tiles = seq // TOKEN_TILE
            lat_spec = pl.BlockSpec((TOKEN_TILE, D),
                                    lambda i: ((i // lat_tiles) * tiles_per_seq + ctx_tiles + i % lat_tiles, 0))
            out = peer_ffn(stream, lat_spec, Bsz * seq, mods, _mod_spec(lat_tiles, 0), norm_ffn_w[layer],
                           peer_wq[layer], peer_keys[layer], u_pack, v_pack, final_norm_w=norm_f_w)
    return out.reshape(Bsz, seq, D)
```

```python
import functools

import jax
import jax.numpy as jnp
from jax import lax
from jax.experimental import pallas as pl
from jax.experimental.pallas import tpu as pltpu

D_MODEL = 1024
DEPTH = 2
GRID_W = 64
CTX_LEN = 256
NORM_EPS = 1e-6
CONV_W = 3
SC_WIDTH = 1024
SSD_HEADS = 16
SSD_HEAD_DIM = 64
SSD_WIDTH = SSD_HEADS * SSD_HEAD_DIM
SSD_GROUPS = 4
SSD_STATE = 128
SSD_BC = SSD_GROUPS * SSD_STATE
SSD_CONV_DIM = SSD_WIDTH + 2 * SSD_BC
HG_HEADS = 8
HG_KDIM = 128
HG_VDIM = 128
HG_WIDTH = HG_HEADS * HG_KDIM
PEER_HEADS = 8
PEER_NKEYS = 128
PEER_QDIM = 256
PEER_HALF = PEER_QDIM // 2
PEER_TOPK = 16
PEER_SEL = PEER_HEADS * PEER_TOPK
PEER_QW = PEER_HEADS * PEER_QDIM

LANES = 128
TOKEN_TILE = 256
SSD_CHUNK = 128
GLA_CHUNK = 64
ROW_WORDS = D_MODEL // (2 * LANES)
ROW_BLOCKS = 2 * ROW_WORDS
PEER_U_GROUP = 64
PEER_V_GROUP = 64
ROUTE_HEADS_PER_STEP = 4
MIB = 1024 * 1024
PEER_VMEM_BYTES = 48 * MIB
MIX_VMEM_BYTES = 48 * MIB
BF = jnp.bfloat16
F32 = jnp.float32


def _mod_spec(tiles_per_seq, ctx_tiles):
    return pl.BlockSpec((1, 1, 1, D_MODEL),
                        lambda i: (i // tiles_per_seq, jnp.where(i % tiles_per_seq < ctx_tiles, 0, 1), 0, 0))


def _tok_spec(width):
    return pl.BlockSpec((TOKEN_TILE, width), lambda i: (i, 0))


def _row_spec(width):
    return pl.BlockSpec((1, width), lambda i: (0, 0))


_RESIDENT = pl.BlockSpec(memory_space=pltpu.VMEM)


def _norm_mod(x, nw, shift, scale):
    y = x * lax.rsqrt(jnp.mean(x * x, axis=-1, keepdims=True) + NORM_EPS) * nw
    return y * (1.0 + scale) + shift


def _silu(x):
    return x * jax.nn.sigmoid(x)


def _select_dot(a, b, select_lhs):
    sel, x = (a, b) if select_lhs else (b, a)
    sel = sel.astype(F32).astype(BF)
    hi = x.astype(BF)
    rest = x - hi.astype(F32)
    mid = rest.astype(BF)
    parts = (hi, mid, (rest - mid.astype(F32)).astype(BF))
    if select_lhs:
        return sum(jnp.dot(sel, p, preferred_element_type=F32) for p in parts)
    return sum(jnp.dot(p, sel, preferred_element_type=F32) for p in parts)


def _ada_body(c_ref, w_ref, b_ref, o_ref):
    cond = _silu(c_ref[...]).astype(BF)
    o_ref[...] = jnp.dot(cond, w_ref[...].astype(BF), preferred_element_type=F32) + b_ref[...]


def ada_modulation(cond, w, b):
    rows = cond.shape[0]
    return pl.pallas_call(
        _ada_body, grid=(6,),
        in_specs=[pl.BlockSpec((rows, D_MODEL), lambda j: (0, 0)),
                  pl.BlockSpec((D_MODEL, D_MODEL), lambda j: (0, j)),
                  pl.BlockSpec((1, D_MODEL), lambda j: (0, j))],
        out_specs=pl.BlockSpec((rows, D_MODEL), lambda j: (0, j)),
        out_shape=jax.ShapeDtypeStruct((rows, 6 * D_MODEL), F32),
        name="ada_modulation",
    )(cond, w, b.reshape(1, 6 * D_MODEL))


def _even_in_body(x_ref, shift_ref, scale_ref, nw_ref, w_ref, cw_sc_ref, cw_ssd_ref, cb_ssd_ref,
                  ya_ref, z_ref, xs_ref, bm_ref, cm_ref, dt_ref, *, tiles_per_seq, ctx_tiles):
    T = TOKEN_TILE
    h = _norm_mod(x_ref[...], nw_ref[...], shift_ref[0, 0], scale_ref[0, 0]).astype(BF)
    is_ctx = (pl.program_id(0) % tiles_per_seq) < ctx_tiles
    row_len = jnp.where(is_ctx, T, GRID_W)
    pos = lax.broadcasted_iota(jnp.int32, (T, 1), 0) & (row_len - 1)
    first, last = pos == 0, pos == row_len - 1

    def conv(u, cw_ref):
        prev = jnp.where(first, 0.0, pltpu.roll(u, 1, 0))
        nxt = jnp.where(last, 0.0, pltpu.roll(u, T - 1, 0))
        return prev * cw_ref[0:1, :] + u * cw_ref[1:2, :] + nxt * cw_ref[2:3, :]

    def proj(lo, width):
        return jnp.dot(h, w_ref[:, lo:lo + width], preferred_element_type=F32)

    sc_x, sc_b, sc_c = proj(0, SC_WIDTH), proj(SC_WIDTH, SC_WIDTH), proj(2 * SC_WIDTH, SC_WIDTH)
    ya_ref[...] = sc_b * conv(sc_c * sc_x, cw_sc_ref)
    z_ref[...] = proj(3 * SC_WIDTH, SSD_WIDTH)
    xbc = _silu(conv(proj(3 * SC_WIDTH + SSD_WIDTH, SSD_CONV_DIM), cw_ssd_ref) + cb_ssd_ref[...])
    xs_ref[...] = xbc[:, :SSD_WIDTH]
    bm_ref[...] = xbc[:, SSD_WIDTH:SSD_WIDTH + SSD_BC]
    cm_ref[...] = xbc[:, SSD_WIDTH + SSD_BC:]
    dt_ref[...] = proj(3 * SC_WIDTH + SSD_WIDTH + SSD_CONV_DIM, 2 * LANES)


def even_in(x, mod_shift, mod_scale, norm_w, w_in, sc_conv_w, ssd_conv_w, ssd_conv_b, tiles_per_seq, ctx_tiles):
    n = x.shape[0]
    main = 3 * SC_WIDTH + SSD_WIDTH + SSD_CONV_DIM
    w_dt = jnp.zeros((D_MODEL, 2, LANES), F32).at[:, :, :SSD_HEADS].set(w_in[:, main:].reshape(D_MODEL, 2, SSD_HEADS))
    w = jnp.concatenate([w_in[:, :main], w_dt.reshape(D_MODEL, 2 * LANES)], axis=1).astype(BF)
    mod = _mod_spec(tiles_per_seq, ctx_tiles)
    widths = (SC_WIDTH, SSD_WIDTH, SSD_WIDTH, SSD_BC, SSD_BC, 2 * LANES)
    return pl.pallas_call(
        functools.partial(_even_in_body, tiles_per_seq=tiles_per_seq, ctx_tiles=ctx_tiles),
        grid=(n // TOKEN_TILE,),
        in_specs=[_tok_spec(D_MODEL), mod, mod, _row_spec(D_MODEL), _RESIDENT,
                  pl.BlockSpec((CONV_W, SC_WIDTH), lambda i: (0, 0)),
                  pl.BlockSpec((CONV_W, SSD_CONV_DIM), lambda i: (0, 0)), _row_spec(SSD_CONV_DIM)],
        out_specs=[_tok_spec(wd) for wd in widths],
        out_shape=[jax.ShapeDtypeStruct((n, wd), F32) for wd in widths],
        compiler_params=pltpu.CompilerParams(vmem_limit_bytes=MIX_VMEM_BYTES),
        name="even_in",
    )(x, mod_shift, mod_scale, norm_w.reshape(1, D_MODEL), w, sc_conv_w, ssd_conv_w,
      ssd_conv_b.reshape(1, SSD_CONV_DIM))


def _scan_row(n_chunks, ctx_chunks):
    def row(b, d, c):
        if d == 0:
            return b * n_chunks + c
        return b * n_chunks + jnp.where(c < ctx_chunks, ctx_chunks - 1 - c, n_chunks - 1 + ctx_chunks - c)
    return row


def _causal_mask(L, d):
    r = lax.broadcasted_iota(jnp.int32, (L, L), 0)
    c = lax.broadcasted_iota(jnp.int32, (L, L), 1)
    return c <= r if d == 0 else c >= r


SSD_GW = (SSD_HEADS // SSD_GROUPS) * SSD_HEAD_DIM


def _alternate(*phased):
    live = list(phased)
    while live:
        live = [g for g in live if next(g, StopIteration) is not StopIteration]


def _ssd_chunk(d, xs_ref, bm_ref, cm_ref, dt_ref, bias, a, y_ref, state):
    L = SSD_CHUNK
    mask = _causal_mask(L, d)
    v = dt_ref[...] + bias
    dt = jnp.maximum(v, 0.0) + jnp.log1p(jnp.exp(-jnp.abs(v)))
    la = _select_dot(mask, dt * a, True)
    la_t = la.T
    head_of_lane = lax.broadcasted_iota(jnp.int32, (LANES, SSD_WIDTH), 1) // SSD_HEAD_DIM
    expand = lax.broadcasted_iota(jnp.int32, (LANES, SSD_WIDTH), 0) == head_of_lane
    dt_x = _select_dot(dt, expand, False)
    la_x = _select_dot(la, expand, False)
    end_x = la_x[L - 1:L] if d == 0 else la_x[0:1]
    xdt = xs_ref[...] * dt_x
    w_state = (xdt * jnp.exp(end_x - la_x)).astype(BF)
    xdt_b = xdt.astype(BF)
    e_la = jnp.exp(la_x)
    dec = jnp.exp(end_x)
    yield
    for g in range(SSD_GROUPS):
        bm = bm_ref[:, g * SSD_STATE:(g + 1) * SSD_STATE].astype(BF)
        cm = cm_ref[:, g * SSD_STATE:(g + 1) * SSD_STATE].astype(BF)
        cb = lax.dot_general(cm, bm, (((1,), (1,)), ((), ())), preferred_element_type=F32)
        lanes = slice(g * SSD_GW, (g + 1) * SSD_GW)
        s_old = state[g]
        y_in = jnp.dot(cm, s_old.astype(BF), preferred_element_type=F32) * e_la[:, lanes]
        ys = []
        for j in range(SSD_HEADS // SSD_GROUPS):
            h = g * (SSD_HEADS // SSD_GROUPS) + j
            seg = la[:, h:h + 1] - la_t[h:h + 1, :]
            m = (cb * jnp.exp(jnp.where(mask, seg, -jnp.inf))).astype(BF)
            ys.append(jnp.dot(m, xdt_b[:, h * SSD_HEAD_DIM:(h + 1) * SSD_HEAD_DIM], preferred_element_type=F32))
        y_ref[:, lanes] = y_in + jnp.concatenate(ys, axis=1)
        state[g] = dec[:, lanes] * s_old + jnp.dot(bm.T, w_state[:, lanes], preferred_element_type=F32)
        yield


def _ssd_body(xs0_ref, bm0_ref, cm0_ref, dt0_ref, xs1_ref, bm1_ref, cm1_ref, dt1_ref, bias_ref, a_ref,
              y0_ref, y1_ref, state):
    @pl.when(pl.program_id(1) == 0)
    def _():
        state[...] = jnp.zeros_like(state)

    _alternate(_ssd_chunk(0, xs0_ref, bm0_ref, cm0_ref, dt0_ref, bias_ref[0], a_ref[0], y0_ref, state.at[0]),
               _ssd_chunk(1, xs1_ref, bm1_ref, cm1_ref, dt1_ref, bias_ref[1], a_ref[1], y1_ref, state.at[1]))


def ssd_scan(xs, bm, cm, dt_raw, dt_bias, a, n_seq, ctx_chunks):
    n = xs.shape[0]
    n_chunks = n // n_seq // SSD_CHUNK
    row = _scan_row(n_chunks, ctx_chunks)
    pad = lambda p: jnp.zeros((2, 1, LANES), F32).at[:, 0, :SSD_HEADS].set(p)
    tok = lambda d, width: pl.BlockSpec((SSD_CHUNK, width), lambda b, c: (row(b, d, c), 0))
    ins = lambda d: [tok(d, SSD_WIDTH), tok(d, SSD_BC), tok(d, SSD_BC),
                     pl.BlockSpec((SSD_CHUNK, LANES), lambda b, c: (row(b, d, c), d))]
    per_dir = pl.BlockSpec((2, 1, LANES), lambda b, c: (0, 0, 0))
    sds = jax.ShapeDtypeStruct((n, SSD_WIDTH), F32)
    return pl.pallas_call(
        _ssd_body, grid=(n_seq, n_chunks),
        in_specs=ins(0) + ins(1) + [per_dir, per_dir],
        out_specs=[tok(0, SSD_WIDTH), tok(1, SSD_WIDTH)],
        out_shape=[sds, sds],
        scratch_shapes=[pltpu.VMEM((2, SSD_GROUPS, SSD_STATE, SSD_GW), F32)],
        compiler_params=pltpu.CompilerParams(dimension_semantics=("arbitrary", "arbitrary")),
        name="ssd_scan",
    )(xs, bm, cm, dt_raw, xs, bm, cm, dt_raw, pad(dt_bias), pad(a))


def _gla_chunk(d, q_ref, f_ref, v_ref, lb, o_ref, state):
    L = GLA_CHUNK
    mask = _causal_mask(L, d)
    f = lb + (1.0 - lb) * jax.nn.sigmoid(f_ref[0])
    gc = _select_dot(mask, jnp.log(f), True)
    g_end = gc[L - 1:L] if d == 0 else gc[0:1]
    k = 1.0 - f
    qg = (q_ref[...] * jnp.exp(gc)).astype(BF)
    kg = (k * jnp.exp(-gc)).astype(BF)
    k_end = (k * jnp.exp(g_end - gc)).astype(BF)
    dec = jnp.exp(g_end)
    vb = v_ref[...].astype(BF)
    yield
    for h in range(HG_HEADS):
        lanes = slice(h * HG_KDIM, (h + 1) * HG_KDIM)
        att = lax.dot_general(qg[:, lanes], kg[:, lanes], (((1,), (1,)), ((), ())), preferred_element_type=F32)
        att = jnp.where(mask, att, 0.0).astype(BF)
        s_old = state[h]
        o = jnp.dot(att, vb[:, lanes], preferred_element_type=F32)
        o_ref[:, lanes] = o + lax.dot_general(qg[:, lanes], s_old.astype(BF), (((1,), (1,)), ((), ())),
                                              preferred_element_type=F32)
        state[h] = s_old * dec[:, lanes] + jnp.dot(vb[:, lanes].T, k_end[:, lanes], preferred_element_type=F32)
        yield


def _gla_body(q0_ref, f0_ref, v0_ref, q1_ref, f1_ref, v1_ref, lb_ref, o0_ref, o1_ref, state):
    @pl.when(pl.program_id(1) == 0)
    def _():
        state[...] = jnp.zeros_like(state)

    _alternate(_gla_chunk(0, q0_ref, f0_ref, v0_ref, lb_ref[0], o0_ref, state.at[0]),
               _gla_chunk(1, q1_ref, f1_ref, v1_ref, lb_ref[1], o1_ref, state.at[1]))


def gla_scan(q, f_raw, v, lb, n_seq, ctx_chunks):
    n = q.shape[0]
    n_chunks = n // n_seq // GLA_CHUNK
    row = _scan_row(n_chunks, ctx_chunks)
    tok = lambda d: pl.BlockSpec((GLA_CHUNK, HG_WIDTH), lambda b, c: (row(b, d, c), 0))
    ins = lambda d: [tok(d), pl.BlockSpec((1, GLA_CHUNK, HG_WIDTH), lambda b, c: (d, row(b, d, c), 0)), tok(d)]
    sds = jax.ShapeDtypeStruct((n, HG_WIDTH), F32)
    return pl.pallas_call(
        _gla_body, grid=(n_seq, n_chunks),
        in_specs=ins(0) + ins(1) + [pl.BlockSpec((2, 1, HG_WIDTH), lambda b, c: (0, 0, 0))],
        out_specs=[tok(0), tok(1)],
        out_shape=[sds, sds],
        scratch_shapes=[pltpu.VMEM((2, HG_HEADS, HG_VDIM, HG_KDIM), F32)],
        compiler_params=pltpu.CompilerParams(dimension_semantics=("arbitrary", "arbitrary")),
        name="gla_scan",
    )(q, f_raw, v, q, f_raw, v, lb)


def _group_rmsnorm(y, groups):
    width = y.shape[1] // groups
    parts = []
    for g in range(groups):
        seg = y[:, g * width:(g + 1) * width]
        parts.append(seg * lax.rsqrt(jnp.mean(seg * seg, axis=-1, keepdims=True) + NORM_EPS))
    return jnp.concatenate(parts, axis=1)


def _even_out_body(x_ref, gate_ref, ya_ref, z_ref, xs_ref, y0_ref, y1_ref, dskip_ref, nw_ref, w_ref, o_ref):
    yb = (y0_ref[...] + y1_ref[...] + dskip_ref[...] * xs_ref[...]) * _silu(z_ref[...])
    yb = _group_rmsnorm(yb, SSD_GROUPS) * nw_ref[...]
    mix = jnp.concatenate([ya_ref[...], yb], axis=1).astype(BF)
    o_ref[...] = x_ref[...] + gate_ref[0, 0] * jnp.dot(mix, w_ref[...], preferred_element_type=F32)


def even_out(x, mod_gate, ya, z, xs, y0, y1, d_skip, norm_w, w_out, tiles_per_seq, ctx_tiles):
    n = x.shape[0]
    return pl.pallas_call(
        _even_out_body, grid=(n // TOKEN_TILE,),
        in_specs=[_tok_spec(D_MODEL), _mod_spec(tiles_per_seq, ctx_tiles), _tok_spec(SC_WIDTH), _tok_spec(SSD_WIDTH),
                  _tok_spec(SSD_WIDTH), _tok_spec(SSD_WIDTH), _tok_spec(SSD_WIDTH), _row_spec(SSD_WIDTH),
                  _row_spec(SSD_WIDTH), _RESIDENT],
        out_specs=_tok_spec(D_MODEL),
        out_shape=jax.ShapeDtypeStruct((n, D_MODEL), F32),
        compiler_params=pltpu.CompilerParams(vmem_limit_bytes=MIX_VMEM_BYTES),
        name="even_out",
    )(x, mod_gate, ya, z, xs, y0, y1, jnp.repeat(d_skip, SSD_HEAD_DIM).reshape(1, SSD_WIDTH),
      norm_w.reshape(1, SSD_WIDTH), w_out.astype(BF))


def _odd_in_body(x_ref, shift_ref, scale_ref, nw_ref, w_ref, q_ref, f_ref, v_ref, g_ref):
    h = _norm_mod(x_ref[...], nw_ref[...], shift_ref[0, 0], scale_ref[0, 0]).astype(BF)
    proj = lambda j: jnp.dot(h, w_ref[:, j * HG_WIDTH:(j + 1) * HG_WIDTH], preferred_element_type=F32)
    q_ref[...] = _silu(proj(0)) * (HG_KDIM ** -0.5)
    f_ref[0] = proj(1)
    f_ref[1] = proj(2)
    v_ref[...] = proj(3)
    g_ref[...] = proj(4)


def odd_in(x, mod_shift, mod_scale, norm_w, w_in, tiles_per_seq, ctx_tiles):
    n = x.shape[0]
    mod = _mod_spec(tiles_per_seq, ctx_tiles)
    tok = _tok_spec(HG_WIDTH)
    sds = jax.ShapeDtypeStruct((n, HG_WIDTH), F32)
    return pl.pallas_call(
        _odd_in_body, grid=(n // TOKEN_TILE,),
        in_specs=[_tok_spec(D_MODEL), mod, mod, _row_spec(D_MODEL), _RESIDENT],
        out_specs=[tok, pl.BlockSpec((2, TOKEN_TILE, HG_WIDTH), lambda i: (0, i, 0)), tok, tok],
        out_shape=[sds, jax.ShapeDtypeStruct((2, n, HG_WIDTH), F32), sds, sds],
        compiler_params=pltpu.CompilerParams(vmem_limit_bytes=MIX_VMEM_BYTES),
        name="odd_in",
    )(x, mod_shift, mod_scale, norm_w.reshape(1, D_MODEL), w_in.astype(BF))


def _odd_out_body(x_ref, gate_ref, o0_ref, o1_ref, g_ref, nw_ref, w_ref, out_ref):
    o = _group_rmsnorm(o0_ref[...] + o1_ref[...], HG_HEADS) * nw_ref[...] * _silu(g_ref[...])
    out_ref[...] = x_ref[...] + gate_ref[0, 0] * jnp.dot(o.astype(BF), w_ref[...], preferred_element_type=F32)


def odd_out(x, mod_gate, o0, o1, g, norm_w, w_out, tiles_per_seq, ctx_tiles):
    n = x.shape[0]
    return pl.pallas_call(
        _odd_out_body, grid=(n // TOKEN_TILE,),
        in_specs=[_tok_spec(D_MODEL), _mod_spec(tiles_per_seq, ctx_tiles), _tok_spec(HG_WIDTH), _tok_spec(HG_WIDTH),
                  _tok_spec(HG_WIDTH), _row_spec(HG_WIDTH), _RESIDENT],
        out_specs=_tok_spec(D_MODEL),
        out_shape=jax.ShapeDtypeStruct((n, D_MODEL), F32),
        compiler_params=pltpu.CompilerParams(vmem_limit_bytes=MIX_VMEM_BYTES),
        name="odd_out",
    )(x, mod_gate, o0, o1, g, norm_w.reshape(1, HG_WIDTH), w_out.astype(BF))


def _top16(vals, ids, n_ids):
    top_s, top_i = [], []
    for r in range(PEER_TOPK):
        m = jnp.max(vals, axis=0, keepdims=True)
        pos = jnp.min(jnp.where(vals == m, ids, float(n_ids)), axis=0, keepdims=True)
        if r + 1 < PEER_TOPK:
            vals = jnp.where(ids == pos, -jnp.inf, vals)
        top_s.append(m)
        top_i.append(pos)
    return jnp.concatenate(top_s, axis=0), jnp.concatenate(top_i, axis=0).astype(jnp.int32)


def _pick_row(table, row):
    r = lax.broadcasted_iota(jnp.int32, table.shape, 0)
    return jnp.sum(jnp.where(r == row, table, 0), axis=0, keepdims=True)


def _route_body(x_ref, shift_ref, scale_ref, nw_ref, wq_ref, keys_ref, h_ref, idx_ref, gate_ref,
                q_scr, idx_scr, gate_scr):
    h = _norm_mod(x_ref[...], nw_ref[...], shift_ref[0, 0], scale_ref[0, 0])
    h_ref[...] = h
    q_scr[...] = jnp.dot(h.astype(BF), wq_ref[...], preferred_element_type=F32)

    def route(hd, tok0):
        tops = []
        for a in range(2):
            col = pl.multiple_of(hd * PEER_QDIM + a * PEER_HALF, PEER_HALF)
            qs = q_scr[pl.ds(tok0, LANES), pl.ds(col, PEER_HALF)].astype(BF)
            sc = lax.dot_general(keys_ref[hd * 2 + a], qs, (((1,), (1,)), ((), ())),
                                 preferred_element_type=F32)
            tops.append(_top16(sc, lax.broadcasted_iota(jnp.int32, sc.shape, 0).astype(F32), PEER_NKEYS))
        (s0, i0), (s1, i1) = tops
        sub = lax.broadcasted_iota(jnp.int32, (8, LANES), 0).astype(F32)
        chunks = [(s0[0:8] + s1[0:1], sub * PEER_TOPK), (s0[8:16] + s1[0:1], (sub + 8) * PEER_TOPK)]
        chunks += [(s0[0:8] + s1[b:b + 1], sub * PEER_TOPK + b) for b in range(1, 8)]
        chunks.append((s0[0:1] + s1[8:16], sub + 8))
        best, pos = _top16(jnp.concatenate([c[0] for c in chunks], axis=0),
                           jnp.concatenate([c[1] for c in chunks], axis=0), PEER_TOPK * PEER_TOPK)
        ids = jnp.concatenate(
            [_pick_row(i0, pos[r:r + 1] >> 4) * PEER_NKEYS + _pick_row(i1, pos[r:r + 1] & (PEER_TOPK - 1))
             for r in range(PEER_TOPK)], axis=0)
        e = jnp.exp(best - best[0:1])
        r0 = pl.multiple_of(hd * PEER_TOPK, PEER_TOPK)
        idx_scr[pl.ds(r0, PEER_TOPK), pl.ds(tok0, LANES)] = ids * ROW_WORDS
        gate_scr[pl.ds(r0, PEER_TOPK), pl.ds(tok0, LANES)] = e / jnp.sum(e, axis=0, keepdims=True)

    def head_group(hg, carry):
        for hh in range(ROUTE_HEADS_PER_STEP):
            for lg in range(TOKEN_TILE // LANES):
                route(hg * ROUTE_HEADS_PER_STEP + hh, lg * LANES)
        return carry

    lax.fori_loop(0, PEER_HEADS // ROUTE_HEADS_PER_STEP, head_group, 0)
    idx_ref[...] = idx_scr[...].T
    gate_ref[...] = gate_scr[...].T


PACK_ROWS = 512


def _pack_body(tab_ref, out_ref):
    bits = pltpu.bitcast(tab_ref[0].astype(BF).astype(F32), jnp.int32)
    for s in range(ROW_WORDS):
        lo = bits[:, 2 * s * LANES:(2 * s + 1) * LANES]
        hi = bits[:, (2 * s + 1) * LANES:(2 * s + 2) * LANES]
        out_ref[pl.ds(s, PACK_ROWS, stride=ROW_WORDS), :] = lax.shift_right_logical(lo, 16) | hi


def pack_table(tabs, layer):
    E = tabs.shape[1]
    return pl.pallas_call(
        _pack_body, grid=(E // PACK_ROWS,),
        in_specs=[pl.BlockSpec((1, PACK_ROWS, D_MODEL), lambda i: (layer, i, 0))],
        out_specs=pl.BlockSpec((PACK_ROWS * ROW_WORDS, LANES), lambda i: (i, 0)),
        out_shape=jax.ShapeDtypeStruct((E * ROW_WORDS, LANES), jnp.int32),
        name="pack_table",
    )(tabs)


def _gathered_blocks(idx_ref, tab_ref, t):
    row = idx_ref.at[t]
    slabs = [tab_ref[pl.ds(pl.multiple_of(row[k], ROW_WORDS), ROW_WORDS), :] for k in range(PEER_SEL)]
    return pltpu.bitcast(jnp.concatenate(slabs, axis=0), BF)


def _own_block_mask():
    shape = (ROW_BLOCKS, ROW_BLOCKS * PEER_SEL)
    return lax.broadcasted_iota(jnp.int32, shape, 1) % ROW_BLOCKS == lax.broadcasted_iota(jnp.int32, shape, 0)


def _peer_u_body(idx_ref, h_ref, gate_ref, tab_ref, act_ref, part_scr):
    own_block = _own_block_mask()

    def step(s, carry):
        for i in range(PEER_U_GROUP):
            t = s * PEER_U_GROUP + i
            hrow = h_ref[pl.ds(t, 1), :]
            lhs = jnp.concatenate([hrow[:, m * LANES:(m + 1) * LANES] for m in range(ROW_BLOCKS)], axis=0).astype(BF)
            prod = lax.dot_general(lhs, _gathered_blocks(idx_ref, tab_ref, t), (((1,), (1,)), ((), ())),
                                   preferred_element_type=F32)
            part_scr[pl.ds(t, 1), :] = jnp.sum(jnp.where(own_block, prod, 0.0), axis=0, keepdims=True)
        return carry

    lax.fori_loop(0, TOKEN_TILE // PEER_U_GROUP, step, 0)
    width = ROW_BLOCKS * PEER_SEL
    fold = (lax.broadcasted_iota(jnp.int32, (width, PEER_SEL), 0) // ROW_BLOCKS
            == lax.broadcasted_iota(jnp.int32, (width, PEER_SEL), 1))
    d = _select_dot(part_scr[...], fold, False)
    act_ref[...] = 0.5 * d * (1.0 + lax.erf(d * (2.0 ** -0.5))) * gate_ref[...]


def _peer_v_body(idx_ref, act_ref, x_ref, gate_ref, tab_ref, fw_ref, out_ref, actx_scr, blk_scr, *, final_norm):
    width = ROW_BLOCKS * PEER_SEL
    spread = (lax.broadcasted_iota(jnp.int32, (PEER_SEL, width), 1) // ROW_BLOCKS
              == lax.broadcasted_iota(jnp.int32, (PEER_SEL, width), 0)).astype(F32).astype(BF)
    actx_scr[...] = jnp.dot(act_ref[...].astype(BF), spread, preferred_element_type=F32)
    own_block = _own_block_mask()

    def step(s, carry):
        for i in range(PEER_V_GROUP):
            t = s * PEER_V_GROUP + i
            lhs = jnp.where(own_block, jnp.broadcast_to(actx_scr[pl.ds(t, 1), :], (ROW_BLOCKS, width)), 0.0).astype(BF)
            blk_scr[pl.ds(pl.multiple_of(t * ROW_BLOCKS, ROW_BLOCKS), ROW_BLOCKS), :] = jnp.dot(
                lhs, _gathered_blocks(idx_ref, tab_ref, t), preferred_element_type=F32)
        return carry

    lax.fori_loop(0, TOKEN_TILE // PEER_V_GROUP, step, 0)
    acc = jnp.concatenate([blk_scr[pl.ds(m, TOKEN_TILE, stride=ROW_BLOCKS), :] for m in range(ROW_BLOCKS)], axis=1)
    y = x_ref[...] + gate_ref[0, 0] * acc
    if final_norm:
        y = y * lax.rsqrt(jnp.mean(y * y, axis=-1, keepdims=True) + NORM_EPS) * fw_ref[...]
    out_ref[...] = y


def peer_ffn(x, x_spec, n, mods, mod_spec, norm_w, wq, keys, u_pack, v_pack, final_norm_w=None):
    grid = (n // TOKEN_TILE,)
    sel_spec = _tok_spec(PEER_SEL)
    h, idx_words, gate = pl.pallas_call(
        _route_body, grid=grid,
        in_specs=[x_spec, mod_spec, mod_spec, _row_spec(D_MODEL),
                  pl.BlockSpec((D_MODEL, PEER_QW), lambda i: (0, 0)),
                  pl.BlockSpec((PEER_HEADS * 2, PEER_NKEYS, PEER_HALF), lambda i: (0, 0, 0))],
        out_specs=[_tok_spec(D_MODEL), sel_spec, sel_spec],
        out_shape=[jax.ShapeDtypeStruct((n, D_MODEL), F32),
                   jax.ShapeDtypeStruct((n, PEER_SEL), jnp.int32),
                   jax.ShapeDtypeStruct((n, PEER_SEL), F32)],
        scratch_shapes=[pltpu.VMEM((TOKEN_TILE, PEER_QW), F32),
                        pltpu.VMEM((PEER_SEL, TOKEN_TILE), jnp.int32),
                        pltpu.VMEM((PEER_SEL, TOKEN_TILE), F32)],
        compiler_params=pltpu.CompilerParams(vmem_limit_bytes=MIX_VMEM_BYTES),
        name="peer_route",
    )(x, mods[3], mods[4], norm_w.reshape(1, D_MODEL), wq.astype(BF),
      keys.reshape(PEER_HEADS * 2, PEER_NKEYS, PEER_HALF).astype(BF))

    params = pltpu.CompilerParams(vmem_limit_bytes=PEER_VMEM_BYTES)
    idx_spec = pl.BlockSpec((TOKEN_TILE, PEER_SEL), lambda i: (i, 0), memory_space=pltpu.SMEM)
    act = pl.pallas_call(
        _peer_u_body, grid=grid,
        in_specs=[idx_spec, _tok_spec(D_MODEL), sel_spec, _RESIDENT],
        out_specs=sel_spec,
        out_shape=jax.ShapeDtypeStruct((n, PEER_SEL), F32),
        scratch_shapes=[pltpu.VMEM((TOKEN_TILE, ROW_BLOCKS * PEER_SEL), F32)],
        compiler_params=params, name="peer_u",
    )(idx_words, h, gate, u_pack)
    fw = jnp.ones((D_MODEL,), F32) if final_norm_w is None else final_norm_w
    return pl.pallas_call(
        functools.partial(_peer_v_body, final_norm=final_norm_w is not None), grid=grid,
        in_specs=[idx_spec, sel_spec, x_spec, mod_spec, _RESIDENT, _row_spec(D_MODEL)],
        out_specs=_tok_spec(D_MODEL),
        out_shape=jax.ShapeDtypeStruct((n, D_MODEL), F32),
        scratch_shapes=[pltpu.VMEM((TOKEN_TILE, ROW_BLOCKS * PEER_SEL), F32),
                        pltpu.VMEM((TOKEN_TILE * ROW_BLOCKS, LANES), F32)],
        compiler_params=params, name="peer_v",
    )(idx_words, act, x, mods[5], v_pack, fw.reshape(1, D_MODEL))


def even_layer(stream, mods, n_seq, tiles_per_seq, ctx_tiles, norm_w, w_in, w_out, sc_conv_w, ssd_conv_w,
               ssd_conv_b, dt_bias, a_log, d_skip, ssd_norm_w):
    ya, z, xs, bm, cm, dt_raw = even_in(stream, mods[0], mods[1], norm_w, w_in, sc_conv_w, ssd_conv_w, ssd_conv_b,
                                        tiles_per_seq, ctx_tiles)
    y0, y1 = ssd_scan(xs, bm, cm, dt_raw, dt_bias, -jnp.exp(a_log), n_seq, ctx_tiles * TOKEN_TILE // SSD_CHUNK)
    return even_out(stream, mods[2], ya, z, xs, y0, y1, d_skip, ssd_norm_w, w_out, tiles_per_seq, ctx_tiles)


def odd_layer(stream, mods, n_seq, tiles_per_seq, ctx_tiles, norm_w, w_in, w_out, lower_bound, hg_norm_w):
    q, f_raw, v, g = odd_in(stream, mods[0], mods[1], norm_w, w_in, tiles_per_seq, ctx_tiles)
    o0, o1 = gla_scan(q, f_raw, v, lower_bound.reshape(2, 1, HG_WIDTH), n_seq, ctx_tiles * TOKEN_TILE // GLA_CHUNK)
    return odd_out(stream, mods[2], o0, o1, g, hg_norm_w, w_out, tiles_per_seq, ctx_tiles)


def kernel(x, c, ctx, c_ctx, ada_w, ada_b, norm_mix_w, norm_ffn_w, norm_f_w, ev_w_in, ev_w_out,
           sc_conv_w, ssd_conv_w, ssd_conv_b, ssd_dt_bias, ssd_a_log, ssd_d, ssd_norm_w,
           od_w_in, od_w_out, hg_lb_logits, hg_norm_w, peer_wq, peer_keys, peer_u, peer_v):
    Bsz, seq, D = x.shape
    ctx_len = ctx.shape[1]
    assert D == D_MODEL and ctx_len == TOKEN_TILE and seq % TOKEN_TILE == 0 and TOKEN_TILE % GRID_W == 0
    lb_sm = jax.nn.softmax(hg_lb_logits.astype(F32), axis=0)
    lower_bounds = jnp.cumsum(lb_sm, axis=0) - lb_sm[0]
    cond = jnp.concatenate([c, c_ctx[None, :], jnp.zeros((7 - Bsz % 8, D), F32)], axis=0)

    ctx_tiles = ctx_len // TOKEN_TILE
    tiles_per_seq = ctx_tiles + seq // TOKEN_TILE
    stream = jnp.concatenate([ctx, x], axis=1).reshape(Bsz * (ctx_len + seq), D)
    for layer in range(DEPTH):
        last = layer == DEPTH - 1
        j = layer // 2
        m = ada_modulation(cond, ada_w[layer], ada_b[layer]).reshape(-1, 6, D)
        mods = [jnp.stack([jnp.broadcast_to(m[Bsz, k], (Bsz, D)), m[:Bsz, k]], axis=1)[:, :, None, :] for k in range(6)]
        if layer % 2 == 0:
            stream = even_layer(stream, mods, Bsz, tiles_per_seq, ctx_tiles, norm_mix_w[layer], ev_w_in[j], ev_w_out[j],
                                sc_conv_w[j], ssd_conv_w[j], ssd_conv_b[j], ssd_dt_bias[j], ssd_a_log[j], ssd_d[j],
                                ssd_norm_w[j])
        else:
            stream = odd_layer(stream, mods, Bsz, tiles_per_seq, ctx_tiles, norm_mix_w[layer], od_w_in[j], od_w_out[j],
                               lower_bounds[layer], hg_norm_w[j])
        u_pack, v_pack = pack_table(peer_u, layer), pack_table(peer_v, layer)
        if not last:
            stream = peer_ffn(stream, _tok_spec(D), stream.shape[0], mods, _mod_spec(tiles_per_seq, ctx_tiles),
                              norm_ffn_w[layer], peer_wq[layer], peer_keys[layer], u_pack, v_pack)
        else:
            lat_tiles = seq // TOKEN_TILE
            lat_spec = pl.BlockSpec((TOKEN_TILE, D),
                                    lambda i: ((i // lat_tiles) * tiles_per_seq + ctx_tiles + i % lat_tiles, 0))
            out = peer_ffn(stream, lat_spec, Bsz * seq, mods, _mod_spec(lat_tiles, 0), norm_ffn_w[layer],
                           peer_wq[layer], peer_keys[layer], u_pack, v_pack, final_norm_w=norm_f_w)
    return out.reshape(Bsz, seq, D)
```

```python
import functools

import jax
import jax.numpy as jnp
from jax import lax
from jax.experimental import pallas as pl
from jax.experimental.pallas import tpu as pltpu

D_MODEL = 1024
DEPTH = 2
GRID_W = 64
CTX_LEN = 256
NORM_EPS = 1e-6
CONV_W = 3
SC_WIDTH = 1024
SSD_HEADS = 16
SSD_HEAD_DIM = 64
SSD_WIDTH = SSD_HEADS * SSD_HEAD_DIM
SSD_GROUPS = 4
SSD_STATE = 128
SSD_BC = SSD_GROUPS * SSD_STATE
SSD_CONV_DIM = SSD_WIDTH + 2 * SSD_BC
HG_HEADS = 8
HG_KDIM = 128
HG_VDIM = 128
HG_WIDTH = HG_HEADS * HG_KDIM
PEER_HEADS = 8
PEER_NKEYS = 128
PEER_QDIM = 256
PEER_HALF = PEER_QDIM // 2
PEER_TOPK = 16
PEER_SEL = PEER_HEADS * PEER_TOPK
PEER_QW = PEER_HEADS * PEER_QDIM

LANES = 128
TOKEN_TILE = 256
SSD_CHUNK = 128
GLA_CHUNK = 64
ROW_WORDS = D_MODEL // (2 * LANES)
ROW_BLOCKS = 2 * ROW_WORDS
PEER_U_GROUP = 128
PEER_V_GROUP = 128
ROUTE_HEADS_PER_STEP = 4
MIB = 1024 * 1024
PEER_VMEM_BYTES = 48 * MIB
MIX_VMEM_BYTES = 48 * MIB
BF = jnp.bfloat16
F32 = jnp.float32


def _mod_spec(tiles_per_seq, ctx_tiles):
    return pl.BlockSpec((1, 1, 1, D_MODEL),
                        lambda i: (i // tiles_per_seq, jnp.where(i % tiles_per_seq < ctx_tiles, 0, 1), 0, 0))


def _tok_spec(width):
    return pl.BlockSpec((TOKEN_TILE, width), lambda i: (i, 0))


def _row_spec(width):
    return pl.BlockSpec((1, width), lambda i: (0, 0))


_RESIDENT = pl.BlockSpec(memory_space=pltpu.VMEM)


def _norm_mod(x, nw, shift, scale):
    y = x * lax.rsqrt(jnp.mean(x * x, axis=-1, keepdims=True) + NORM_EPS) * nw
    return y * (1.0 + scale) + shift


def _silu(x):
    return x * jax.nn.sigmoid(x)


def _select_dot(a, b, select_lhs):
    sel, x = (a, b) if select_lhs else (b, a)
    sel = sel.astype(F32).astype(BF)
    hi = x.astype(BF)
    rest = x - hi.astype(F32)
    mid = rest.astype(BF)
    parts = (hi, mid, (rest - mid.astype(F32)).astype(BF))
    if select_lhs:
        return sum(jnp.dot(sel, p, preferred_element_type=F32) for p in parts)
    return sum(jnp.dot(p, sel, preferred_element_type=F32) for p in parts)


def _ada_body(c_ref, w_ref, b_ref, o_ref):
    cond = _silu(c_ref[...]).astype(BF)
    o_ref[...] = jnp.dot(cond, w_ref[...].astype(BF), preferred_element_type=F32) + b_ref[...]


def ada_modulation(cond, w, b):
    rows = cond.shape[0]
    return pl.pallas_call(
        _ada_body, grid=(6,),
        in_specs=[pl.BlockSpec((rows, D_MODEL), lambda j: (0, 0)),
                  pl.BlockSpec((D_MODEL, D_MODEL), lambda j: (0, j)),
                  pl.BlockSpec((1, D_MODEL), lambda j: (0, j))],
        out_specs=pl.BlockSpec((rows, D_MODEL), lambda j: (0, j)),
        out_shape=jax.ShapeDtypeStruct((rows, 6 * D_MODEL), F32),
        name="ada_modulation",
    )(cond, w, b.reshape(1, 6 * D_MODEL))


def _even_in_body(x_ref, shift_ref, scale_ref, nw_ref, w_ref, cw_sc_ref, cw_ssd_ref, cb_ssd_ref,
                  ya_ref, z_ref, xs_ref, bm_ref, cm_ref, dt_ref, *, tiles_per_seq, ctx_tiles):
    T = TOKEN_TILE
    h = _norm_mod(x_ref[...], nw_ref[...], shift_ref[0, 0], scale_ref[0, 0]).astype(BF)
    is_ctx = (pl.program_id(0) % tiles_per_seq) < ctx_tiles
    row_len = jnp.where(is_ctx, T, GRID_W)
    pos = lax.broadcasted_iota(jnp.int32, (T, 1), 0) & (row_len - 1)
    first, last = pos == 0, pos == row_len - 1

    def conv(u, cw_ref):
        prev = jnp.where(first, 0.0, pltpu.roll(u, 1, 0))
        nxt = jnp.where(last, 0.0, pltpu.roll(u, T - 1, 0))
        return prev * cw_ref[0:1, :] + u * cw_ref[1:2, :] + nxt * cw_ref[2:3, :]

    def proj(lo, width):
        return jnp.dot(h, w_ref[:, lo:lo + width], preferred_element_type=F32)

    sc_x, sc_b, sc_c = proj(0, SC_WIDTH), proj(SC_WIDTH, SC_WIDTH), proj(2 * SC_WIDTH, SC_WIDTH)
    ya_ref[...] = sc_b * conv(sc_c * sc_x, cw_sc_ref)
    z_ref[...] = proj(3 * SC_WIDTH, SSD_WIDTH)
    xbc = _silu(conv(proj(3 * SC_WIDTH + SSD_WIDTH, SSD_CONV_DIM), cw_ssd_ref) + cb_ssd_ref[...])
    xs_ref[...] = xbc[:, :SSD_WIDTH]
    bm_ref[...] = xbc[:, SSD_WIDTH:SSD_WIDTH + SSD_BC]
    cm_ref[...] = xbc[:, SSD_WIDTH + SSD_BC:]
    dt_ref[...] = proj(3 * SC_WIDTH + SSD_WIDTH + SSD_CONV_DIM, 2 * LANES)


def even_in(x, mod_shift, mod_scale, norm_w, w_in, sc_conv_w, ssd_conv_w, ssd_conv_b, tiles_per_seq, ctx_tiles):
    n = x.shape[0]
    main = 3 * SC_WIDTH + SSD_WIDTH + SSD_CONV_DIM
    w_dt = jnp.zeros((D_MODEL, 2, LANES), F32).at[:, :, :SSD_HEADS].set(w_in[:, main:].reshape(D_MODEL, 2, SSD_HEADS))
    w = jnp.concatenate([w_in[:, :main], w_dt.reshape(D_MODEL, 2 * LANES)], axis=1).astype(BF)
    mod = _mod_spec(tiles_per_seq, ctx_tiles)
    widths = (SC_WIDTH, SSD_WIDTH, SSD_WIDTH, SSD_BC, SSD_BC, 2 * LANES)
    return pl.pallas_call(
        functools.partial(_even_in_body, tiles_per_seq=tiles_per_seq, ctx_tiles=ctx_tiles),
        grid=(n // TOKEN_TILE,),
        in_specs=[_tok_spec(D_MODEL), mod, mod, _row_spec(D_MODEL), _RESIDENT,
                  pl.BlockSpec((CONV_W, SC_WIDTH), lambda i: (0, 0)),
                  pl.BlockSpec((CONV_W, SSD_CONV_DIM), lambda i: (0, 0)), _row_spec(SSD_CONV_DIM)],
        out_specs=[_tok_spec(wd) for wd in widths],
        out_shape=[jax.ShapeDtypeStruct((n, wd), F32) for wd in widths],
        compiler_params=pltpu.CompilerParams(vmem_limit_bytes=MIX_VMEM_BYTES),
        name="even_in",
    )(x, mod_shift, mod_scale, norm_w.reshape(1, D_MODEL), w, sc_conv_w, ssd_conv_w,
      ssd_conv_b.reshape(1, SSD_CONV_DIM))


def _scan_row(n_chunks, ctx_chunks):
    def row(b, d, c):
        if d == 0:
            return b * n_chunks + c
        return b * n_chunks + jnp.where(c < ctx_chunks, ctx_chunks - 1 - c, n_chunks - 1 + ctx_chunks - c)
    return row


def _causal_mask(L, d):
    r = lax.broadcasted_iota(jnp.int32, (L, L), 0)
    c = lax.broadcasted_iota(jnp.int32, (L, L), 1)
    return c <= r if d == 0 else c >= r


SSD_GW = (SSD_HEADS // SSD_GROUPS) * SSD_HEAD_DIM


def _alternate(*phased):
    live = list(phased)
    while live:
        live = [g for g in live if next(g, StopIteration) is not StopIteration]


def _ssd_chunk(d, xs_ref, bm_ref, cm_ref, dt_ref, bias, a, y_ref, state):
    L = SSD_CHUNK
    mask = _causal_mask(L, d)
    v = dt_ref[...] + bias
    dt = jnp.maximum(v, 0.0) + jnp.log1p(jnp.exp(-jnp.abs(v)))
    la = _select_dot(mask, dt * a, True)
    la_t = la.T
    head_of_lane = lax.broadcasted_iota(jnp.int32, (LANES, SSD_WIDTH), 1) // SSD_HEAD_DIM
    expand = lax.broadcasted_iota(jnp.int32, (LANES, SSD_WIDTH), 0) == head_of_lane
    dt_x = _select_dot(dt, expand, False)
    la_x = _select_dot(la, expand, False)
    end_x = la_x[L - 1:L] if d == 0 else la_x[0:1]
    xdt = xs_ref[...] * dt_x
    w_state = (xdt * jnp.exp(end_x - la_x)).astype(BF)
    xdt_b = xdt.astype(BF)
    e_la = jnp.exp(la_x)
    dec = jnp.exp(end_x)
    yield
    for g in range(SSD_GROUPS):
        bm = bm_ref[:, g * SSD_STATE:(g + 1) * SSD_STATE].astype(BF)
        cm = cm_ref[:, g * SSD_STATE:(g + 1) * SSD_STATE].astype(BF)
        cb = lax.dot_general(cm, bm, (((1,), (1,)), ((), ())), preferred_element_type=F32)
        lanes = slice(g * SSD_GW, (g + 1) * SSD_GW)
        s_old = state[g]
        y_in = jnp.dot(cm, s_old.astype(BF), preferred_element_type=F32) * e_la[:, lanes]
        ys = []
        for j in range(SSD_HEADS // SSD_GROUPS):
            h = g * (SSD_HEADS // SSD_GROUPS) + j
            seg = la[:, h:h + 1] - la_t[h:h + 1, :]
            m = (cb * jnp.exp(jnp.where(mask, seg, -jnp.inf))).astype(BF)
            ys.append(jnp.dot(m, xdt_b[:, h * SSD_HEAD_DIM:(h + 1) * SSD_HEAD_DIM], preferred_element_type=F32))
        y_ref[:, lanes] = y_in + jnp.concatenate(ys, axis=1)
        state[g] = dec[:, lanes] * s_old + jnp.dot(bm.T, w_state[:, lanes], preferred_element_type=F32)
        yield


def _ssd_body(xs0_ref, bm0_ref, cm0_ref, dt0_ref, xs1_ref, bm1_ref, cm1_ref, dt1_ref, bias_ref, a_ref,
              y0_ref, y1_ref, state):
    @pl.when(pl.program_id(1) == 0)
    def _():
        state[...] = jnp.zeros_like(state)

    _alternate(_ssd_chunk(0, xs0_ref, bm0_ref, cm0_ref, dt0_ref, bias_ref[0], a_ref[0], y0_ref, state.at[0]),
               _ssd_chunk(1, xs1_ref, bm1_ref, cm1_ref, dt1_ref, bias_ref[1], a_ref[1], y1_ref, state.at[1]))


def ssd_scan(xs, bm, cm, dt_raw, dt_bias, a, n_seq, ctx_chunks):
    n = xs.shape[0]
    n_chunks = n // n_seq // SSD_CHUNK
    row = _scan_row(n_chunks, ctx_chunks)
    pad = lambda p: jnp.zeros((2, 1, LANES), F32).at[:, 0, :SSD_HEADS].set(p)
    tok = lambda d, width: pl.BlockSpec((SSD_CHUNK, width), lambda b, c: (row(b, d, c), 0))
    ins = lambda d: [tok(d, SSD_WIDTH), tok(d, SSD_BC), tok(d, SSD_BC),
                     pl.BlockSpec((SSD_CHUNK, LANES), lambda b, c: (row(b, d, c), d))]
    per_dir = pl.BlockSpec((2, 1, LANES), lambda b, c: (0, 0, 0))
    sds = jax.ShapeDtypeStruct((n, SSD_WIDTH), F32)
    return pl.pallas_call(
        _ssd_body, grid=(n_seq, n_chunks),
        in_specs=ins(0) + ins(1) + [per_dir, per_dir],
        out_specs=[tok(0, SSD_WIDTH), tok(1, SSD_WIDTH)],
        out_shape=[sds, sds],
        scratch_shapes=[pltpu.VMEM((2, SSD_GROUPS, SSD_STATE, SSD_GW), F32)],
        compiler_params=pltpu.CompilerParams(dimension_semantics=("arbitrary", "arbitrary")),
        name="ssd_scan",
    )(xs, bm, cm, dt_raw, xs, bm, cm, dt_raw, pad(dt_bias), pad(a))


def _gla_chunk(d, q_ref, f_ref, v_ref, lb, o_ref, state):
    L = GLA_CHUNK
    mask = _causal_mask(L, d)
    f = lb + (1.0 - lb) * jax.nn.sigmoid(f_ref[0])
    gc = _select_dot(mask, jnp.log(f), True)
    g_end = gc[L - 1:L] if d == 0 else gc[0:1]
    k = 1.0 - f
    qg = (q_ref[...] * jnp.exp(gc)).astype(BF)
    kg = (k * jnp.exp(-gc)).astype(BF)
    k_end = (k * jnp.exp(g_end - gc)).astype(BF)
    dec = jnp.exp(g_end)
    vb = v_ref[...].astype(BF)
    yield
    for h in range(HG_HEADS):
        lanes = slice(h * HG_KDIM, (h + 1) * HG_KDIM)
        att = lax.dot_general(qg[:, lanes], kg[:, lanes], (((1,), (1,)), ((), ())), preferred_element_type=F32)
        att = jnp.where(mask, att, 0.0).astype(BF)
        s_old = state[h]
        o = jnp.dot(att, vb[:, lanes], preferred_element_type=F32)
        o_ref[:, lanes] = o + lax.dot_general(qg[:, lanes], s_old.astype(BF), (((1,), (1,)), ((), ())),
                                              preferred_element_type=F32)
        state[h] = s_old * dec[:, lanes] + jnp.dot(vb[:, lanes].T, k_end[:, lanes], preferred_element_type=F32)
        yield


def _gla_body(q0_ref, f0_ref, v0_ref, q1_ref, f1_ref, v1_ref, lb_ref, o0_ref, o1_ref, state):
    @pl.when(pl.program_id(1) == 0)
    def _():
        state[...] = jnp.zeros_like(state)

    _alternate(_gla_chunk(0, q0_ref, f0_ref, v0_ref, lb_ref[0], o0_ref, state.at[0]),
               _gla_chunk(1, q1_ref, f1_ref, v1_ref, lb_ref[1], o1_ref, state.at[1]))


def gla_scan(q, f_raw, v, lb, n_seq, ctx_chunks):
    n = q.shape[0]
    n_chunks = n // n_seq // GLA_CHUNK
    row = _scan_row(n_chunks, ctx_chunks)
    tok = lambda d: pl.BlockSpec((GLA_CHUNK, HG_WIDTH), lambda b, c: (row(b, d, c), 0))
    ins = lambda d: [tok(d), pl.BlockSpec((1, GLA_CHUNK, HG_WIDTH), lambda b, c: (d, row(b, d, c), 0)), tok(d)]
    sds = jax.ShapeDtypeStruct((n, HG_WIDTH), F32)
    return pl.pallas_call(
        _gla_body, grid=(n_seq, n_chunks),
        in_specs=ins(0) + ins(1) + [pl.BlockSpec((2, 1, HG_WIDTH), lambda b, c: (0, 0, 0))],
        out_specs=[tok(0), tok(1)],
        out_shape=[sds, sds],
        scratch_shapes=[pltpu.VMEM((2, HG_HEADS, HG_VDIM, HG_KDIM), F32)],
        compiler_params=pltpu.CompilerParams(dimension_semantics=("arbitrary", "arbitrary")),
        name="gla_scan",
    )(q, f_raw, v, q, f_raw, v, lb)


def _group_rmsnorm(y, groups):
    width = y.shape[1] // groups
    parts = []
    for g in range(groups):
        seg = y[:, g * width:(g + 1) * width]
        parts.append(seg * lax.rsqrt(jnp.mean(seg * seg, axis=-1, keepdims=True) + NORM_EPS))
    return jnp.concatenate(parts, axis=1)


def _even_out_body(x_ref, gate_ref, ya_ref, z_ref, xs_ref, y0_ref, y1_ref, dskip_ref, nw_ref, w_ref, o_ref):
    yb = (y0_ref[...] + y1_ref[...] + dskip_ref[...] * xs_ref[...]) * _silu(z_ref[...])
    yb = _group_rmsnorm(yb, SSD_GROUPS) * nw_ref[...]
    mix = jnp.concatenate([ya_ref[...], yb], axis=1).astype(BF)
    o_ref[...] = x_ref[...] + gate_ref[0, 0] * jnp.dot(mix, w_ref[...], preferred_element_type=F32)


def even_out(x, mod_gate, ya, z, xs, y0, y1, d_skip, norm_w, w_out, tiles_per_seq, ctx_tiles):
    n = x.shape[0]
    return pl.pallas_call(
        _even_out_body, grid=(n // TOKEN_TILE,),
        in_specs=[_tok_spec(D_MODEL), _mod_spec(tiles_per_seq, ctx_tiles), _tok_spec(SC_WIDTH), _tok_spec(SSD_WIDTH),
                  _tok_spec(SSD_WIDTH), _tok_spec(SSD_WIDTH), _tok_spec(SSD_WIDTH), _row_spec(SSD_WIDTH),
                  _row_spec(SSD_WIDTH), _RESIDENT],
        out_specs=_tok_spec(D_MODEL),
        out_shape=jax.ShapeDtypeStruct((n, D_MODEL), F32),
        compiler_params=pltpu.CompilerParams(vmem_limit_bytes=MIX_VMEM_BYTES),
        name="even_out",
    )(x, mod_gate, ya, z, xs, y0, y1, jnp.repeat(d_skip, SSD_HEAD_DIM).reshape(1, SSD_WIDTH),
      norm_w.reshape(1, SSD_WIDTH), w_out.astype(BF))


def _odd_in_body(x_ref, shift_ref, scale_ref, nw_ref, w_ref, q_ref, f_ref, v_ref, g_ref):
    h = _norm_mod(x_ref[...], nw_ref[...], shift_ref[0, 0], scale_ref[0, 0]).astype(BF)
    proj = lambda j: jnp.dot(h, w_ref[:, j * HG_WIDTH:(j + 1) * HG_WIDTH], preferred_element_type=F32)
    q_ref[...] = _silu(proj(0)) * (HG_KDIM ** -0.5)
    f_ref[0] = proj(1)
    f_ref[1] = proj(2)
    v_ref[...] = proj(3)
    g_ref[...] = proj(4)


def odd_in(x, mod_shift, mod_scale, norm_w, w_in, tiles_per_seq, ctx_tiles):
    n = x.shape[0]
    mod = _mod_spec(tiles_per_seq, ctx_tiles)
    tok = _tok_spec(HG_WIDTH)
    sds = jax.ShapeDtypeStruct((n, HG_WIDTH), F32)
    return pl.pallas_call(
        _odd_in_body, grid=(n // TOKEN_TILE,),
        in_specs=[_tok_spec(D_MODEL), mod, mod, _row_spec(D_MODEL), _RESIDENT],
        out_specs=[tok, pl.BlockSpec((2, TOKEN_TILE, HG_WIDTH), lambda i: (0, i, 0)), tok, tok],
        out_shape=[sds, jax.ShapeDtypeStruct((2, n, HG_WIDTH), F32), sds, sds],
        compiler_params=pltpu.CompilerParams(vmem_limit_bytes=MIX_VMEM_BYTES),
        name="odd_in",
    )(x, mod_shift, mod_scale, norm_w.reshape(1, D_MODEL), w_in.astype(BF))


def _odd_out_body(x_ref, gate_ref, o0_ref, o1_ref, g_ref, nw_ref, w_ref, out_ref):
    o = _group_rmsnorm(o0_ref[...] + o1_ref[...], HG_HEADS) * nw_ref[...] * _silu(g_ref[...])
    out_ref[...] = x_ref[...] + gate_ref[0, 0] * jnp.dot(o.astype(BF), w_ref[...], preferred_element_type=F32)


def odd_out(x, mod_gate, o0, o1, g, norm_w, w_out, tiles_per_seq, ctx_tiles):
    n = x.shape[0]
    return pl.pallas_call(
        _odd_out_body, grid=(n // TOKEN_TILE,),
        in_specs=[_tok_spec(D_MODEL), _mod_spec(tiles_per_seq, ctx_tiles), _tok_spec(HG_WIDTH), _tok_spec(HG_WIDTH),
                  _tok_spec(HG_WIDTH), _row_spec(HG_WIDTH), _RESIDENT],
        out_specs=_tok_spec(D_MODEL),
        out_shape=jax.ShapeDtypeStruct((n, D_MODEL), F32),
        compiler_params=pltpu.CompilerParams(vmem_limit_bytes=MIX_VMEM_BYTES),
        name="odd_out",
    )(x, mod_gate, o0, o1, g, norm_w.reshape(1, HG_WIDTH), w_out.astype(BF))


def _top16(vals, ids, n_ids):
    top_s, top_i = [], []
    for r in range(PEER_TOPK):
        m = jnp.max(vals, axis=0, keepdims=True)
        pos = jnp.min(jnp.where(vals == m, ids, float(n_ids)), axis=0, keepdims=True)
        if r + 1 < PEER_TOPK:
            vals = jnp.where(ids == pos, -jnp.inf, vals)
        top_s.append(m)
        top_i.append(pos)
    return jnp.concatenate(top_s, axis=0), jnp.concatenate(top_i, axis=0).astype(jnp.int32)


def _pick_row(table, row):
    r = lax.broadcasted_iota(jnp.int32, table.shape, 0)
    return jnp.sum(jnp.where(r == row, table, 0), axis=0, keepdims=True)


def _route_body(x_ref, shift_ref, scale_ref, nw_ref, wq_ref, keys_ref, h_ref, idx_ref, gate_ref,
                q_scr, idx_scr, gate_scr):
    h = _norm_mod(x_ref[...], nw_ref[...], shift_ref[0, 0], scale_ref[0, 0])
    h_ref[...] = h
    q_scr[...] = jnp.dot(h.astype(BF), wq_ref[...], preferred_element_type=F32)

    def route(hd, tok0):
        tops = []
        for a in range(2):
            col = pl.multiple_of(hd * PEER_QDIM + a * PEER_HALF, PEER_HALF)
            qs = q_scr[pl.ds(tok0, LANES), pl.ds(col, PEER_HALF)].astype(BF)
            sc = lax.dot_general(keys_ref[hd * 2 + a], qs, (((1,), (1,)), ((), ())),
                                 preferred_element_type=F32)
            tops.append(_top16(sc, lax.broadcasted_iota(jnp.int32, sc.shape, 0).astype(F32), PEER_NKEYS))
        (s0, i0), (s1, i1) = tops
        sub = lax.broadcasted_iota(jnp.int32, (8, LANES), 0).astype(F32)
        chunks = [(s0[0:8] + s1[0:1], sub * PEER_TOPK), (s0[8:16] + s1[0:1], (sub + 8) * PEER_TOPK)]
        chunks += [(s0[0:8] + s1[b:b + 1], sub * PEER_TOPK + b) for b in range(1, 8)]
        chunks.append((s0[0:1] + s1[8:16], sub + 8))
        best, pos = _top16(jnp.concatenate([c[0] for c in chunks], axis=0),
                           jnp.concatenate([c[1] for c in chunks], axis=0), PEER_TOPK * PEER_TOPK)
        ids = jnp.concatenate(
            [_pick_row(i0, pos[r:r + 1] >> 4) * PEER_NKEYS + _pick_row(i1, pos[r:r + 1] & (PEER_TOPK - 1))
             for r in range(PEER_TOPK)], axis=0)
        e = jnp.exp(best - best[0:1])
        r0 = pl.multiple_of(hd * PEER_TOPK, PEER_TOPK)
        idx_scr[pl.ds(r0, PEER_TOPK), pl.ds(tok0, LANES)] = ids * ROW_WORDS
        gate_scr[pl.ds(r0, PEER_TOPK), pl.ds(tok0, LANES)] = e / jnp.sum(e, axis=0, keepdims=True)

    def head_group(hg, carry):
        for hh in range(ROUTE_HEADS_PER_STEP):
            for lg in range(TOKEN_TILE // LANES):
                route(hg * ROUTE_HEADS_PER_STEP + hh, lg * LANES)
        return carry

    lax.fori_loop(0, PEER_HEADS // ROUTE_HEADS_PER_STEP, head_group, 0)
    idx_ref[...] = idx_scr[...].T
    gate_ref[...] = gate_scr[...].T


PACK_ROWS = 512


def _pack_body(tab_ref, out_ref):
    bits = pltpu.bitcast(tab_ref[0].astype(BF).astype(F32), jnp.int32)
    for s in range(ROW_WORDS):
        lo = bits[:, 2 * s * LANES:(2 * s + 1) * LANES]
        hi = bits[:, (2 * s + 1) * LANES:(2 * s + 2) * LANES]
        out_ref[pl.ds(s, PACK_ROWS, stride=ROW_WORDS), :] = lax.shift_right_logical(lo, 16) | hi


def pack_table(tabs, layer):
    E = tabs.shape[1]
    return pl.pallas_call(
        _pack_body, grid=(E // PACK_ROWS,),
        in_specs=[pl.BlockSpec((1, PACK_ROWS, D_MODEL), lambda i: (layer, i, 0))],
        out_specs=pl.BlockSpec((PACK_ROWS * ROW_WORDS, LANES), lambda i: (i, 0)),
        out_shape=jax.ShapeDtypeStruct((E * ROW_WORDS, LANES), jnp.int32),
        name="pack_table",
    )(tabs)


def _gathered_blocks(idx_ref, tab_ref, t):
    row = idx_ref.at[t]
    slabs = [tab_ref[pl.ds(pl.multiple_of(row[k], ROW_WORDS), ROW_WORDS), :] for k in range(PEER_SEL)]
    return pltpu.bitcast(jnp.concatenate(slabs, axis=0), BF)


def _own_block_mask():
    shape = (ROW_BLOCKS, ROW_BLOCKS * PEER_SEL)
    return lax.broadcasted_iota(jnp.int32, shape, 1) % ROW_BLOCKS == lax.broadcasted_iota(jnp.int32, shape, 0)


def _peer_u_body(idx_ref, h_ref, gate_ref, tab_ref, act_ref, part_scr):
    own_block = _own_block_mask()

    def step(s, carry):
        for i in range(PEER_U_GROUP):
            t = s * PEER_U_GROUP + i
            hrow = h_ref[pl.ds(t, 1), :]
            lhs = jnp.concatenate([hrow[:, m * LANES:(m + 1) * LANES] for m in range(ROW_BLOCKS)], axis=0).astype(BF)
            prod = lax.dot_general(lhs, _gathered_blocks(idx_ref, tab_ref, t), (((1,), (1,)), ((), ())),
                                   preferred_element_type=F32)
            part_scr[pl.ds(t, 1), :] = jnp.sum(jnp.where(own_block, prod, 0.0), axis=0, keepdims=True)
        return carry

    lax.fori_loop(0, TOKEN_TILE // PEER_U_GROUP, step, 0)
    width = ROW_BLOCKS * PEER_SEL
    fold = (lax.broadcasted_iota(jnp.int32, (width, PEER_SEL), 0) // ROW_BLOCKS
            == lax.broadcasted_iota(jnp.int32, (width, PEER_SEL), 1))
    d = _select_dot(part_scr[...], fold, False)
    act_ref[...] = 0.5 * d * (1.0 + lax.erf(d * (2.0 ** -0.5))) * gate_ref[...]


def _peer_v_body(idx_ref, act_ref, x_ref, gate_ref, tab_ref, fw_ref, out_ref, actx_scr, blk_scr, *, final_norm):
    width = ROW_BLOCKS * PEER_SEL
    spread = (lax.broadcasted_iota(jnp.int32, (PEER_SEL, width), 1) // ROW_BLOCKS
              == lax.broadcasted_iota(jnp.int32, (PEER_SEL, width), 0)).astype(F32).astype(BF)
    actx_scr[...] = jnp.dot(act_ref[...].astype(BF), spread, preferred_element_type=F32)
    own_block = _own_block_mask()

    def step(s, carry):
        for i in range(PEER_V_GROUP):
            t = s * PEER_V_GROUP + i
            lhs = jnp.where(own_block, jnp.broadcast_to(actx_scr[pl.ds(t, 1), :], (ROW_BLOCKS, width)), 0.0).astype(BF)
            blk_scr[pl.ds(pl.multiple_of(t * ROW_BLOCKS, ROW_BLOCKS), ROW_BLOCKS), :] = jnp.dot(
                lhs, _gathered_blocks(idx_ref, tab_ref, t), preferred_element_type=F32)
        return carry

    lax.fori_loop(0, TOKEN_TILE // PEER_V_GROUP, step, 0)
    acc = jnp.concatenate([blk_scr[pl.ds(m, TOKEN_TILE, stride=ROW_BLOCKS), :] for m in range(ROW_BLOCKS)], axis=1)
    y = x_ref[...] + gate_ref[0, 0] * acc
    if final_norm:
        y = y * lax.rsqrt(jnp.mean(y * y, axis=-1, keepdims=True) + NORM_EPS) * fw_ref[...]
    out_ref[...] = y


def peer_ffn(x, x_spec, n, mods, mod_spec, norm_w, wq, keys, u_pack, v_pack, final_norm_w=None):
    grid = (n // TOKEN_TILE,)
    sel_spec = _tok_spec(PEER_SEL)
    h, idx_words, gate = pl.pallas_call(
        _route_body, grid=grid,
        in_specs=[x_spec, mod_spec, mod_spec, _row_spec(D_MODEL),
                  pl.BlockSpec((D_MODEL, PEER_QW), lambda i: (0, 0)),
                  pl.BlockSpec((PEER_HEADS * 2, PEER_NKEYS, PEER_HALF), lambda i: (0, 0, 0))],
        out_specs=[_tok_spec(D_MODEL), sel_spec, sel_spec],
        out_shape=[jax.ShapeDtypeStruct((n, D_MODEL), F32),
                   jax.ShapeDtypeStruct((n, PEER_SEL), jnp.int32),
                   jax.ShapeDtypeStruct((n, PEER_SEL), F32)],
        scratch_shapes=[pltpu.VMEM((TOKEN_TILE, PEER_QW), F32),
                        pltpu.VMEM((PEER_SEL, TOKEN_TILE), jnp.int32),
                        pltpu.VMEM((PEER_SEL, TOKEN_TILE), F32)],
        compiler_params=pltpu.CompilerParams(vmem_limit_bytes=MIX_VMEM_BYTES),
        name="peer_route",
    )(x, mods[3], mods[4], norm_w.reshape(1, D_MODEL), wq.astype(BF),
      keys.reshape(PEER_HEADS * 2, PEER_NKEYS, PEER_HALF).astype(BF))

    params = pltpu.CompilerParams(vmem_limit_bytes=PEER_VMEM_BYTES)
    idx_spec = pl.BlockSpec((TOKEN_TILE, PEER_SEL), lambda i: (i, 0), memory_space=pltpu.SMEM)
    act = pl.pallas_call(
        _peer_u_body, grid=grid,
        in_specs=[idx_spec, _tok_spec(D_MODEL), sel_spec, _RESIDENT],
        out_specs=sel_spec,
        out_shape=jax.ShapeDtypeStruct((n, PEER_SEL), F32),
        scratch_shapes=[pltpu.VMEM((TOKEN_TILE, ROW_BLOCKS * PEER_SEL), F32)],
        compiler_params=params, name="peer_u",
    )(idx_words, h, gate, u_pack)
    fw = jnp.ones((D_MODEL,), F32) if final_norm_w is None else final_norm_w
    return pl.pallas_call(
        functools.partial(_peer_v_body, final_norm=final_norm_w is not None), grid=grid,
        in_specs=[idx_spec, sel_spec, x_spec, mod_spec, _RESIDENT, _row_spec(D_MODEL)],
        out_specs=_tok_spec(D_MODEL),
        out_shape=jax.ShapeDtypeStruct((n, D_MODEL), F32),
        scratch_shapes=[pltpu.VMEM((TOKEN_TILE, ROW_BLOCKS * PEER_SEL), F32),
                        pltpu.VMEM((TOKEN_TILE * ROW_BLOCKS, LANES), F32)],
        compiler_params=params, name="peer_v",
    )(idx_words, act, x, mods[5], v_pack, fw.reshape(1, D_MODEL))


def even_layer(stream, mods, n_seq, tiles_per_seq, ctx_tiles, norm_w, w_in, w_out, sc_conv_w, ssd_conv_w,
               ssd_conv_b, dt_bias, a_log, d_skip, ssd_norm_w):
    ya, z, xs, bm, cm, dt_raw = even_in(stream, mods[0], mods[1], norm_w, w_in, sc_conv_w, ssd_conv_w, ssd_conv_b,
                                        tiles_per_seq, ctx_tiles)
    y0, y1 = ssd_scan(xs, bm, cm, dt_raw, dt_bias, -jnp.exp(a_log), n_seq, ctx_tiles * TOKEN_TILE // SSD_CHUNK)
    return even_out(stream, mods[2], ya, z, xs, y0, y1, d_skip, ssd_norm_w, w_out, tiles_per_seq, ctx_tiles)


def odd_layer(stream, mods, n_seq, tiles_per_seq, ctx_tiles, norm_w, w_in, w_out, lower_bound, hg_norm_w):
    q, f_raw, v, g = odd_in(stream, mods[0], mods[1], norm_w, w_in, tiles_per_seq, ctx_tiles)
    o0, o1 = gla_scan(q, f_raw, v, lower_bound.reshape(2, 1, HG_WIDTH), n_seq, ctx_tiles * TOKEN_TILE // GLA_CHUNK)
    return odd_out(stream, mods[2], o0, o1, g, hg_norm_w, w_out, tiles_per_seq, ctx_tiles)


def kernel(x, c, ctx, c_ctx, ada_w, ada_b, norm_mix_w, norm_ffn_w, norm_f_w, ev_w_in, ev_w_out,
           sc_conv_w, ssd_conv_w, ssd_conv_b, ssd_dt_bias, ssd_a_log, ssd_d, ssd_norm_w,
           od_w_in, od_w_out, hg_lb_logits, hg_norm_w, peer_wq, peer_keys, peer_u, peer_v):
    Bsz, seq, D = x.shape
    ctx_len = ctx.shape[1]
    assert D == D_MODEL and ctx_len == TOKEN_TILE and seq % TOKEN_TILE == 0 and TOKEN_TILE % GRID_W == 0
    lb_sm = jax.nn.softmax(hg_lb_logits.astype(F32), axis=0)
    lower_bounds = jnp.cumsum(lb_sm, axis=0) - lb_sm[0]
    cond = jnp.concatenate([c, c_ctx[None, :], jnp.zeros((7 - Bsz % 8, D), F32)], axis=0)

    ctx_tiles = ctx_len // TOKEN_TILE
    tiles_per_seq = ctx_tiles + seq // TOKEN_TILE
    stream = jnp.concatenate([ctx, x], axis=1).reshape(Bsz * (ctx_len + seq), D)
    for layer in range(DEPTH):
        last = layer == DEPTH - 1
        j = layer // 2
        m = ada_modulation(cond, ada_w[layer], ada_b[layer]).reshape(-1, 6, D)
        mods = [jnp.stack([jnp.broadcast_to(m[Bsz, k], (Bsz, D)), m[:Bsz, k]], axis=1)[:, :, None, :] for k in range(6)]
        if layer % 2 == 0:
            stream = even_layer(stream, mods, Bsz, tiles_per_seq, ctx_tiles, norm_mix_w[layer], ev_w_in[j], ev_w_out[j],
                                sc_conv_w[j], ssd_conv_w[j], ssd_conv_b[j], ssd_dt_bias[j], ssd_a_log[j], ssd_d[j],
                                ssd_norm_w[j])
        else:
            stream = odd_layer(stream, mods, Bsz, tiles_per_seq, ctx_tiles, norm_mix_w[layer], od_w_in[j], od_w_out[j],
                               lower_bounds[layer], hg_norm_w[j])
        u_pack, v_pack = pack_table(peer_u, layer), pack_table(peer_v, layer)
        if not last:
            stream = peer_ffn(stream, _tok_spec(D), stream.shape[0], mods, _mod_spec(tiles_per_seq, ctx_tiles),
                              norm_ffn_w[layer], peer_wq[layer], peer_keys[layer], u_pack, v_pack)
        else:
            lat_tiles = seq // TOKEN_TILE
            lat_spec = pl.BlockSpec((TOKEN_TILE, D),
                                    lambda i: ((i // lat_tiles) * tiles_per_seq + ctx_tiles + i % lat_tiles, 0))
            out = peer_ffn(stream, lat_spec, Bsz * seq, mods, _mod_spec(lat_tiles, 0), norm_ffn_w[layer],
                           peer_wq[layer], peer_keys[layer], u_pack, v_pack, final_norm_w=norm_f_w)
    return out.reshape(Bsz, seq, D)
```

```python
import functools

import jax
import jax.numpy as jnp
from jax import lax
from jax.experimental import pallas as pl
from jax.experimental.pallas import tpu as pltpu

D_MODEL = 1024
DEPTH = 2
GRID_W = 64
CTX_LEN = 256
NORM_EPS = 1e-6
CONV_W = 3
SC_WIDTH = 1024
SSD_HEADS = 16
SSD_HEAD_DIM = 64
SSD_WIDTH = SSD_HEADS * SSD_HEAD_DIM
SSD_GROUPS = 4
SSD_STATE = 128
SSD_BC = SSD_GROUPS * SSD_STATE
SSD_CONV_DIM = SSD_WIDTH + 2 * SSD_BC
HG_HEADS = 8
HG_KDIM = 128
HG_VDIM = 128
HG_WIDTH = HG_HEADS * HG_KDIM
PEER_HEADS = 8
PEER_NKEYS = 128
PEER_QDIM = 256
PEER_HALF = PEER_QDIM // 2
PEER_TOPK = 16
PEER_SEL = PEER_HEADS * PEER_TOPK
PEER_QW = PEER_HEADS * PEER_QDIM

LANES = 128
TOKEN_TILE = 256
SSD_CHUNK = 128
GLA_CHUNK = 64
ROW_WORDS = D_MODEL // (2 * LANES)
ROW_BLOCKS = 2 * ROW_WORDS
PEER_U_GROUP = 128
PEER_V_GROUP = 128
ROUTE_HEADS_PER_STEP = 8
MIB = 1024 * 1024
PEER_VMEM_BYTES = 48 * MIB
MIX_VMEM_BYTES = 48 * MIB
BF = jnp.bfloat16
F32 = jnp.float32


def _mod_spec(tiles_per_seq, ctx_tiles):
    return pl.BlockSpec((1, 1, 1, D_MODEL),
                        lambda i: (i // tiles_per_seq, jnp.where(i % tiles_per_seq < ctx_tiles, 0, 1), 0, 0))


def _tok_spec(width):
    return pl.BlockSpec((TOKEN_TILE, width), lambda i: (i, 0))


def _row_spec(width):
    return pl.BlockSpec((1, width), lambda i: (0, 0))


_RESIDENT = pl.BlockSpec(memory_space=pltpu.VMEM)


def _norm_mod(x, nw, shift, scale):
    y = x * lax.rsqrt(jnp.mean(x * x, axis=-1, keepdims=True) + NORM_EPS) * nw
    return y * (1.0 + scale) + shift


def _silu(x):
    return x * jax.nn.sigmoid(x)


def _select_dot(a, b, select_lhs):
    sel, x = (a, b) if select_lhs else (b, a)
    sel = sel.astype(F32).astype(BF)
    hi = x.astype(BF)
    rest = x - hi.astype(F32)
    mid = rest.astype(BF)
    parts = (hi, mid, (rest - mid.astype(F32)).astype(BF))
    if select_lhs:
        return sum(jnp.dot(sel, p, preferred_element_type=F32) for p in parts)
    return sum(jnp.dot(p, sel, preferred_element_type=F32) for p in parts)


def _ada_body(c_ref, w_ref, b_ref, o_ref):
    cond = _silu(c_ref[...]).astype(BF)
    o_ref[...] = jnp.dot(cond, w_ref[...].astype(BF), preferred_element_type=F32) + b_ref[...]


def ada_modulation(cond, w, b):
    rows = cond.shape[0]
    return pl.pallas_call(
        _ada_body, grid=(6,),
        in_specs=[pl.BlockSpec((rows, D_MODEL), lambda j: (0, 0)),
                  pl.BlockSpec((D_MODEL, D_MODEL), lambda j: (0, j)),
                  pl.BlockSpec((1, D_MODEL), lambda j: (0, j))],
        out_specs=pl.BlockSpec((rows, D_MODEL), lambda j: (0, j)),
        out_shape=jax.ShapeDtypeStruct((rows, 6 * D_MODEL), F32),
        name="ada_modulation",
    )(cond, w, b.reshape(1, 6 * D_MODEL))


def _even_in_body(x_ref, shift_ref, scale_ref, nw_ref, w_ref, cw_sc_ref, cw_ssd_ref, cb_ssd_ref,
                  ya_ref, z_ref, xs_ref, bm_ref, cm_ref, dt_ref, *, tiles_per_seq, ctx_tiles):
    T = TOKEN_TILE
    h = _norm_mod(x_ref[...], nw_ref[...], shift_ref[0, 0], scale_ref[0, 0]).astype(BF)
    is_ctx = (pl.program_id(0) % tiles_per_seq) < ctx_tiles
    row_len = jnp.where(is_ctx, T, GRID_W)
    pos = lax.broadcasted_iota(jnp.int32, (T, 1), 0) & (row_len - 1)
    first, last = pos == 0, pos == row_len - 1

    def conv(u, cw_ref):
        prev = jnp.where(first, 0.0, pltpu.roll(u, 1, 0))
        nxt = jnp.where(last, 0.0, pltpu.roll(u, T - 1, 0))
        return prev * cw_ref[0:1, :] + u * cw_ref[1:2, :] + nxt * cw_ref[2:3, :]

    def proj(lo, width):
        return jnp.dot(h, w_ref[:, lo:lo + width], preferred_element_type=F32)

    sc_x, sc_b, sc_c = proj(0, SC_WIDTH), proj(SC_WIDTH, SC_WIDTH), proj(2 * SC_WIDTH, SC_WIDTH)
    ya_ref[...] = sc_b * conv(sc_c * sc_x, cw_sc_ref)
    z_ref[...] = proj(3 * SC_WIDTH, SSD_WIDTH)
    xbc = _silu(conv(proj(3 * SC_WIDTH + SSD_WIDTH, SSD_CONV_DIM), cw_ssd_ref) + cb_ssd_ref[...])
    xs_ref[...] = xbc[:, :SSD_WIDTH]
    bm_ref[...] = xbc[:, SSD_WIDTH:SSD_WIDTH + SSD_BC]
    cm_ref[...] = xbc[:, SSD_WIDTH + SSD_BC:]
    dt_ref[...] = proj(3 * SC_WIDTH + SSD_WIDTH + SSD_CONV_DIM, 2 * LANES)


def even_in(x, mod_shift, mod_scale, norm_w, w_in, sc_conv_w, ssd_conv_w, ssd_conv_b, tiles_per_seq, ctx_tiles):
    n = x.shape[0]
    main = 3 * SC_WIDTH + SSD_WIDTH + SSD_CONV_DIM
    w_dt = jnp.zeros((D_MODEL, 2, LANES), F32).at[:, :, :SSD_HEADS].set(w_in[:, main:].reshape(D_MODEL, 2, SSD_HEADS))
    w = jnp.concatenate([w_in[:, :main], w_dt.reshape(D_MODEL, 2 * LANES)], axis=1).astype(BF)
    mod = _mod_spec(tiles_per_seq, ctx_tiles)
    widths = (SC_WIDTH, SSD_WIDTH, SSD_WIDTH, SSD_BC, SSD_BC, 2 * LANES)
    return pl.pallas_call(
        functools.partial(_even_in_body, tiles_per_seq=tiles_per_seq, ctx_tiles=ctx_tiles),
        grid=(n // TOKEN_TILE,),
        in_specs=[_tok_spec(D_MODEL), mod, mod, _row_spec(D_MODEL), _RESIDENT,
                  pl.BlockSpec((CONV_W, SC_WIDTH), lambda i: (0, 0)),
                  pl.BlockSpec((CONV_W, SSD_CONV_DIM), lambda i: (0, 0)), _row_spec(SSD_CONV_DIM)],
        out_specs=[_tok_spec(wd) for wd in widths],
        out_shape=[jax.ShapeDtypeStruct((n, wd), F32) for wd in widths],
        compiler_params=pltpu.CompilerParams(vmem_limit_bytes=MIX_VMEM_BYTES),
        name="even_in",
    )(x, mod_shift, mod_scale, norm_w.reshape(1, D_MODEL), w, sc_conv_w, ssd_conv_w,
      ssd_conv_b.reshape(1, SSD_CONV_DIM))


def _scan_row(n_chunks, ctx_chunks):
    def row(b, d, c):
        if d == 0:
            return b * n_chunks + c
        return b * n_chunks + jnp.where(c < ctx_chunks, ctx_chunks - 1 - c, n_chunks - 1 + ctx_chunks - c)
    return row


def _causal_mask(L, d):
    r = lax.broadcasted_iota(jnp.int32, (L, L), 0)
    c = lax.broadcasted_iota(jnp.int32, (L, L), 1)
    return c <= r if d == 0 else c >= r


SSD_GW = (SSD_HEADS // SSD_GROUPS) * SSD_HEAD_DIM


def _alternate(*phased):
    live = list(phased)
    while live:
        live = [g for g in live if next(g, StopIteration) is not StopIteration]


def _ssd_chunk(d, xs_ref, bm_ref, cm_ref, dt_ref, bias, a, y_ref, state):
    L = SSD_CHUNK
    mask = _causal_mask(L, d)
    v = dt_ref[...] + bias
    dt = jnp.maximum(v, 0.0) + jnp.log1p(jnp.exp(-jnp.abs(v)))
    la = _select_dot(mask, dt * a, True)
    la_t = la.T
    head_of_lane = lax.broadcasted_iota(jnp.int32, (LANES, SSD_WIDTH), 1) // SSD_HEAD_DIM
    expand = lax.broadcasted_iota(jnp.int32, (LANES, SSD_WIDTH), 0) == head_of_lane
    dt_x = _select_dot(dt, expand, False)
    la_x = _select_dot(la, expand, False)
    end_x = la_x[L - 1:L] if d == 0 else la_x[0:1]
    xdt = xs_ref[...] * dt_x
    w_state = (xdt * jnp.exp(end_x - la_x)).astype(BF)
    xdt_b = xdt.astype(BF)
    e_la = jnp.exp(la_x)
    dec = jnp.exp(end_x)
    yield
    for g in range(SSD_GROUPS):
        bm = bm_ref[:, g * SSD_STATE:(g + 1) * SSD_STATE].astype(BF)
        cm = cm_ref[:, g * SSD_STATE:(g + 1) * SSD_STATE].astype(BF)
        cb = lax.dot_general(cm, bm, (((1,), (1,)), ((), ())), preferred_element_type=F32)
        lanes = slice(g * SSD_GW, (g + 1) * SSD_GW)
        s_old = state[g]
        y_in = jnp.dot(cm, s_old.astype(BF), preferred_element_type=F32) * e_la[:, lanes]
        ys = []
        for j in range(SSD_HEADS // SSD_GROUPS):
            h = g * (SSD_HEADS // SSD_GROUPS) + j
            seg = la[:, h:h + 1] - la_t[h:h + 1, :]
            m = (cb * jnp.exp(jnp.where(mask, seg, -jnp.inf))).astype(BF)
            ys.append(jnp.dot(m, xdt_b[:, h * SSD_HEAD_DIM:(h + 1) * SSD_HEAD_DIM], preferred_element_type=F32))
        y_ref[:, lanes] = y_in + jnp.concatenate(ys, axis=1)
        state[g] = dec[:, lanes] * s_old + jnp.dot(bm.T, w_state[:, lanes], preferred_element_type=F32)
        yield


def _ssd_body(xs0_ref, bm0_ref, cm0_ref, dt0_ref, xs1_ref, bm1_ref, cm1_ref, dt1_ref, bias_ref, a_ref,
              y0_ref, y1_ref, state):
    @pl.when(pl.program_id(1) == 0)
    def _():
        state[...] = jnp.zeros_like(state)

    _alternate(_ssd_chunk(0, xs0_ref, bm0_ref, cm0_ref, dt0_ref, bias_ref[0], a_ref[0], y0_ref, state.at[0]),
               _ssd_chunk(1, xs1_ref, bm1_ref, cm1_ref, dt1_ref, bias_ref[1], a_ref[1], y1_ref, state.at[1]))


def ssd_scan(xs, bm, cm, dt_raw, dt_bias, a, n_seq, ctx_chunks):
    n = xs.shape[0]
    n_chunks = n // n_seq // SSD_CHUNK
    row = _scan_row(n_chunks, ctx_chunks)
    pad = lambda p: jnp.zeros((2, 1, LANES), F32).at[:, 0, :SSD_HEADS].set(p)
    tok = lambda d, width: pl.BlockSpec((SSD_CHUNK, width), lambda b, c: (row(b, d, c), 0))
    ins = lambda d: [tok(d, SSD_WIDTH), tok(d, SSD_BC), tok(d, SSD_BC),
                     pl.BlockSpec((SSD_CHUNK, LANES), lambda b, c: (row(b, d, c), d))]
    per_dir = pl.BlockSpec((2, 1, LANES), lambda b, c: (0, 0, 0))
    sds = jax.ShapeDtypeStruct((n, SSD_WIDTH), F32)
    return pl.pallas_call(
        _ssd_body, grid=(n_seq, n_chunks),
        in_specs=ins(0) + ins(1) + [per_dir, per_dir],
        out_specs=[tok(0, SSD_WIDTH), tok(1, SSD_WIDTH)],
        out_shape=[sds, sds],
        scratch_shapes=[pltpu.VMEM((2, SSD_GROUPS, SSD_STATE, SSD_GW), F32)],
        compiler_params=pltpu.CompilerParams(dimension_semantics=("arbitrary", "arbitrary")),
        name="ssd_scan",
    )(xs, bm, cm, dt_raw, xs, bm, cm, dt_raw, pad(dt_bias), pad(a))


def _gla_chunk(d, q_ref, f_ref, v_ref, lb, o_ref, state):
    L = GLA_CHUNK
    mask = _causal_mask(L, d)
    f = lb + (1.0 - lb) * jax.nn.sigmoid(f_ref[0])
    gc = _select_dot(mask, jnp.log(f), True)
    g_end = gc[L - 1:L] if d == 0 else gc[0:1]
    k = 1.0 - f
    qg = (q_ref[...] * jnp.exp(gc)).astype(BF)
    kg = (k * jnp.exp(-gc)).astype(BF)
    k_end = (k * jnp.exp(g_end - gc)).astype(BF)
    dec = jnp.exp(g_end)
    vb = v_ref[...].astype(BF)
    yield
    for h in range(HG_HEADS):
        lanes = slice(h * HG_KDIM, (h + 1) * HG_KDIM)
        att = lax.dot_general(qg[:, lanes], kg[:, lanes], (((1,), (1,)), ((), ())), preferred_element_type=F32)
        att = jnp.where(mask, att, 0.0).astype(BF)
        s_old = state[h]
        o = jnp.dot(att, vb[:, lanes], preferred_element_type=F32)
        o_ref[:, lanes] = o + lax.dot_general(qg[:, lanes], s_old.astype(BF), (((1,), (1,)), ((), ())),
                                              preferred_element_type=F32)
        state[h] = s_old * dec[:, lanes] + jnp.dot(vb[:, lanes].T, k_end[:, lanes], preferred_element_type=F32)
        yield


def _gla_body(q0_ref, f0_ref, v0_ref, q1_ref, f1_ref, v1_ref, lb_ref, o0_ref, o1_ref, state):
    @pl.when(pl.program_id(1) == 0)
    def _():
        state[...] = jnp.zeros_like(state)

    _alternate(_gla_chunk(0, q0_ref, f0_ref, v0_ref, lb_ref[0], o0_ref, state.at[0]),
               _gla_chunk(1, q1_ref, f1_ref, v1_ref, lb_ref[1], o1_ref, state.at[1]))


def gla_scan(q, f_raw, v, lb, n_seq, ctx_chunks):
    n = q.shape[0]
    n_chunks = n // n_seq // GLA_CHUNK
    row = _scan_row(n_chunks, ctx_chunks)
    tok = lambda d: pl.BlockSpec((GLA_CHUNK, HG_WIDTH), lambda b, c: (row(b, d, c), 0))
    ins = lambda d: [tok(d), pl.BlockSpec((1, GLA_CHUNK, HG_WIDTH), lambda b, c: (d, row(b, d, c), 0)), tok(d)]
    sds = jax.ShapeDtypeStruct((n, HG_WIDTH), F32)
    return pl.pallas_call(
        _gla_body, grid=(n_seq, n_chunks),
        in_specs=ins(0) + ins(1) + [pl.BlockSpec((2, 1, HG_WIDTH), lambda b, c: (0, 0, 0))],
        out_specs=[tok(0), tok(1)],
        out_shape=[sds, sds],
        scratch_shapes=[pltpu.VMEM((2, HG_HEADS, HG_VDIM, HG_KDIM), F32)],
        compiler_params=pltpu.CompilerParams(dimension_semantics=("arbitrary", "arbitrary")),
        name="gla_scan",
    )(q, f_raw, v, q, f_raw, v, lb)


def _group_rmsnorm(y, groups):
    width = y.shape[1] // groups
    parts = []
    for g in range(groups):
        seg = y[:, g * width:(g + 1) * width]
        parts.append(seg * lax.rsqrt(jnp.mean(seg * seg, axis=-1, keepdims=True) + NORM_EPS))
    return jnp.concatenate(parts, axis=1)


def _even_out_body(x_ref, gate_ref, ya_ref, z_ref, xs_ref, y0_ref, y1_ref, dskip_ref, nw_ref, w_ref, o_ref):
    yb = (y0_ref[...] + y1_ref[...] + dskip_ref[...] * xs_ref[...]) * _silu(z_ref[...])
    yb = _group_rmsnorm(yb, SSD_GROUPS) * nw_ref[...]
    mix = jnp.concatenate([ya_ref[...], yb], axis=1).astype(BF)
    o_ref[...] = x_ref[...] + gate_ref[0, 0] * jnp.dot(mix, w_ref[...], preferred_element_type=F32)


def even_out(x, mod_gate, ya, z, xs, y0, y1, d_skip, norm_w, w_out, tiles_per_seq, ctx_tiles):
    n = x.shape[0]
    return pl.pallas_call(
        _even_out_body, grid=(n // TOKEN_TILE,),
        in_specs=[_tok_spec(D_MODEL), _mod_spec(tiles_per_seq, ctx_tiles), _tok_spec(SC_WIDTH), _tok_spec(SSD_WIDTH),
                  _tok_spec(SSD_WIDTH), _tok_spec(SSD_WIDTH), _tok_spec(SSD_WIDTH), _row_spec(SSD_WIDTH),
                  _row_spec(SSD_WIDTH), _RESIDENT],
        out_specs=_tok_spec(D_MODEL),
        out_shape=jax.ShapeDtypeStruct((n, D_MODEL), F32),
        compiler_params=pltpu.CompilerParams(vmem_limit_bytes=MIX_VMEM_BYTES),
        name="even_out",
    )(x, mod_gate, ya, z, xs, y0, y1, jnp.repeat(d_skip, SSD_HEAD_DIM).reshape(1, SSD_WIDTH),
      norm_w.reshape(1, SSD_WIDTH), w_out.astype(BF))


def _odd_in_body(x_ref, shift_ref, scale_ref, nw_ref, w_ref, q_ref, f_ref, v_ref, g_ref):
    h = _norm_mod(x_ref[...], nw_ref[...], shift_ref[0, 0], scale_ref[0, 0]).astype(BF)
    proj = lambda j: jnp.dot(h, w_ref[:, j * HG_WIDTH:(j + 1) * HG_WIDTH], preferred_element_type=F32)
    q_ref[...] = _silu(proj(0)) * (HG_KDIM ** -0.5)
    f_ref[0] = proj(1)
    f_ref[1] = proj(2)
    v_ref[...] = proj(3)
    g_ref[...] = proj(4)


def odd_in(x, mod_shift, mod_scale, norm_w, w_in, tiles_per_seq, ctx_tiles):
    n = x.shape[0]
    mod = _mod_spec(tiles_per_seq, ctx_tiles)
    tok = _tok_spec(HG_WIDTH)
    sds = jax.ShapeDtypeStruct((n, HG_WIDTH), F32)
    return pl.pallas_call(
        _odd_in_body, grid=(n // TOKEN_TILE,),
        in_specs=[_tok_spec(D_MODEL), mod, mod, _row_spec(D_MODEL), _RESIDENT],
        out_specs=[tok, pl.BlockSpec((2, TOKEN_TILE, HG_WIDTH), lambda i: (0, i, 0)), tok, tok],
        out_shape=[sds, jax.ShapeDtypeStruct((2, n, HG_WIDTH), F32), sds, sds],
        compiler_params=pltpu.CompilerParams(vmem_limit_bytes=MIX_VMEM_BYTES),
        name="odd_in",
    )(x, mod_shift, mod_scale, norm_w.reshape(1, D_MODEL), w_in.astype(BF))


def _odd_out_body(x_ref, gate_ref, o0_ref, o1_ref, g_ref, nw_ref, w_ref, out_ref):
    o = _group_rmsnorm(o0_ref[...] + o1_ref[...], HG_HEADS) * nw_ref[...] * _silu(g_ref[...])
    out_ref[...] = x_ref[...] + gate_ref[0, 0] * jnp.dot(o.astype(BF), w_ref[...], preferred_element_type=F32)


def odd_out(x, mod_gate, o0, o1, g, norm_w, w_out, tiles_per_seq, ctx_tiles):
    n = x.shape[0]
    return pl.pallas_call(
        _odd_out_body, grid=(n // TOKEN_TILE,),
        in_specs=[_tok_spec(D_MODEL), _mod_spec(tiles_per_seq, ctx_tiles), _tok_spec(HG_WIDTH), _tok_spec(HG_WIDTH),
                  _tok_spec(HG_WIDTH), _row_spec(HG_WIDTH), _RESIDENT],
        out_specs=_tok_spec(D_MODEL),
        out_shape=jax.ShapeDtypeStruct((n, D_MODEL), F32),
        compiler_params=pltpu.CompilerParams(vmem_limit_bytes=MIX_VMEM_BYTES),
        name="odd_out",
    )(x, mod_gate, o0, o1, g, norm_w.reshape(1, HG_WIDTH), w_out.astype(BF))


def _top16(vals, ids, n_ids):
    top_s, top_i = [], []
    for r in range(PEER_TOPK):
        m = jnp.max(vals, axis=0, keepdims=True)
        pos = jnp.min(jnp.where(vals == m, ids, float(n_ids)), axis=0, keepdims=True)
        if r + 1 < PEER_TOPK:
            vals = jnp.where(ids == pos, -jnp.inf, vals)
        top_s.append(m)
        top_i.append(pos)
    return jnp.concatenate(top_s, axis=0), jnp.concatenate(top_i, axis=0).astype(jnp.int32)


def _pick_row(table, row):
    r = lax.broadcasted_iota(jnp.int32, table.shape, 0)
    return jnp.sum(jnp.where(r == row, table, 0), axis=0, keepdims=True)


def _route_body(x_ref, shift_ref, scale_ref, nw_ref, wq_ref, keys_ref, h_ref, idx_ref, gate_ref,
                q_scr, idx_scr, gate_scr):
    h = _norm_mod(x_ref[...], nw_ref[...], shift_ref[0, 0], scale_ref[0, 0])
    h_ref[...] = h
    q_scr[...] = jnp.dot(h.astype(BF), wq_ref[...], preferred_element_type=F32)

    def route(hd, tok0):
        tops = []
        for a in range(2):
            col = pl.multiple_of(hd * PEER_QDIM + a * PEER_HALF, PEER_HALF)
            qs = q_scr[pl.ds(tok0, LANES), pl.ds(col, PEER_HALF)].astype(BF)
            sc = lax.dot_general(keys_ref[hd * 2 + a], qs, (((1,), (1,)), ((), ())),
                                 preferred_element_type=F32)
            tops.append(_top16(sc, lax.broadcasted_iota(jnp.int32, sc.shape, 0).astype(F32), PEER_NKEYS))
        (s0, i0), (s1, i1) = tops
        sub = lax.broadcasted_iota(jnp.int32, (8, LANES), 0).astype(F32)
        chunks = [(s0[0:8] + s1[0:1], sub * PEER_TOPK), (s0[8:16] + s1[0:1], (sub + 8) * PEER_TOPK)]
        chunks += [(s0[0:8] + s1[b:b + 1], sub * PEER_TOPK + b) for b in range(1, 8)]
        chunks.append((s0[0:1] + s1[8:16], sub + 8))
        best, pos = _top16(jnp.concatenate([c[0] for c in chunks], axis=0),
                           jnp.concatenate([c[1] for c in chunks], axis=0), PEER_TOPK * PEER_TOPK)
        ids = jnp.concatenate(
            [_pick_row(i0, pos[r:r + 1] >> 4) * PEER_NKEYS + _pick_row(i1, pos[r:r + 1] & (PEER_TOPK - 1))
             for r in range(PEER_TOPK)], axis=0)
        e = jnp.exp(best - best[0:1])
        r0 = pl.multiple_of(hd * PEER_TOPK, PEER_TOPK)
        idx_scr[pl.ds(r0, PEER_TOPK), pl.ds(tok0, LANES)] = ids * ROW_WORDS
        gate_scr[pl.ds(r0, PEER_TOPK), pl.ds(tok0, LANES)] = e / jnp.sum(e, axis=0, keepdims=True)

    def head_group(hg, carry):
        for hh in range(ROUTE_HEADS_PER_STEP):
            for lg in range(TOKEN_TILE // LANES):
                route(hg * ROUTE_HEADS_PER_STEP + hh, lg * LANES)
        return carry

    lax.fori_loop(0, PEER_HEADS // ROUTE_HEADS_PER_STEP, head_group, 0)
    idx_ref[...] = idx_scr[...].T
    gate_ref[...] = gate_scr[...].T


PACK_ROWS = 512


def _pack_body(tab_ref, out_ref):
    bits = pltpu.bitcast(tab_ref[0].astype(BF).astype(F32), jnp.int32)
    for s in range(ROW_WORDS):
        lo = bits[:, 2 * s * LANES:(2 * s + 1) * LANES]
        hi = bits[:, (2 * s + 1) * LANES:(2 * s + 2) * LANES]
        out_ref[pl.ds(s, PACK_ROWS, stride=ROW_WORDS), :] = lax.shift_right_logical(lo, 16) | hi


def pack_table(tabs, layer):
    E = tabs.shape[1]
    return pl.pallas_call(
        _pack_body, grid=(E // PACK_ROWS,),
        in_specs=[pl.BlockSpec((1, PACK_ROWS, D_MODEL), lambda i: (layer, i, 0))],
        out_specs=pl.BlockSpec((PACK_ROWS * ROW_WORDS, LANES), lambda i: (i, 0)),
        out_shape=jax.ShapeDtypeStruct((E * ROW_WORDS, LANES), jnp.int32),
        name="pack_table",
    )(tabs)


def _gathered_blocks(idx_ref, tab_ref, t):
    row = idx_ref.at[t]
    slabs = [tab_ref[pl.ds(pl.multiple_of(row[k], ROW_WORDS), ROW_WORDS), :] for k in range(PEER_SEL)]
    return pltpu.bitcast(jnp.concatenate(slabs, axis=0), BF)


def _own_block_mask():
    shape = (ROW_BLOCKS, ROW_BLOCKS * PEER_SEL)
    return lax.broadcasted_iota(jnp.int32, shape, 1) % ROW_BLOCKS == lax.broadcasted_iota(jnp.int32, shape, 0)


def _peer_u_body(idx_ref, h_ref, gate_ref, tab_ref, act_ref, part_scr):
    own_block = _own_block_mask()

    def step(s, carry):
        for i in range(PEER_U_GROUP):
            t = s * PEER_U_GROUP + i
            hrow = h_ref[pl.ds(t, 1), :]
            lhs = jnp.concatenate([hrow[:, m * LANES:(m + 1) * LANES] for m in range(ROW_BLOCKS)], axis=0).astype(BF)
            prod = lax.dot_general(lhs, _gathered_blocks(idx_ref, tab_ref, t), (((1,), (1,)), ((), ())),
                                   preferred_element_type=F32)
            part_scr[pl.ds(t, 1), :] = jnp.sum(jnp.where(own_block, prod, 0.0), axis=0, keepdims=True)
        return carry

    lax.fori_loop(0, TOKEN_TILE // PEER_U_GROUP, step, 0)
    width = ROW_BLOCKS * PEER_SEL
    fold = (lax.broadcasted_iota(jnp.int32, (width, PEER_SEL), 0) // ROW_BLOCKS
            == lax.broadcasted_iota(jnp.int32, (width, PEER_SEL), 1))
    d = _select_dot(part_scr[...], fold, False)
    act_ref[...] = 0.5 * d * (1.0 + lax.erf(d * (2.0 ** -0.5))) * gate_ref[...]


def _peer_v_body(idx_ref, act_ref, x_ref, gate_ref, tab_ref, fw_ref, out_ref, actx_scr, blk_scr, *, final_norm):
    width = ROW_BLOCKS * PEER_SEL
    spread = (lax.broadcasted_iota(jnp.int32, (PEER_SEL, width), 1) // ROW_BLOCKS
              == lax.broadcasted_iota(jnp.int32, (PEER_SEL, width), 0)).astype(F32).astype(BF)
    actx_scr[...] = jnp.dot(act_ref[...].astype(BF), spread, preferred_element_type=F32)
    own_block = _own_block_mask()

    def step(s, carry):
        for i in range(PEER_V_GROUP):
            t = s * PEER_V_GROUP + i
            lhs = jnp.where(own_block, jnp.broadcast_to(actx_scr[pl.ds(t, 1), :], (ROW_BLOCKS, width)), 0.0).astype(BF)
            blk_scr[pl.ds(pl.multiple_of(t * ROW_BLOCKS, ROW_BLOCKS), ROW_BLOCKS), :] = jnp.dot(
                lhs, _gathered_blocks(idx_ref, tab_ref, t), preferred_element_type=F32)
        return carry

    lax.fori_loop(0, TOKEN_TILE // PEER_V_GROUP, step, 0)
    acc = jnp.concatenate([blk_scr[pl.ds(m, TOKEN_TILE, stride=ROW_BLOCKS), :] for m in range(ROW_BLOCKS)], axis=1)
    y = x_ref[...] + gate_ref[0, 0] * acc
    if final_norm:
        y = y * lax.rsqrt(jnp.mean(y * y, axis=-1, keepdims=True) + NORM_EPS) * fw_ref[...]
    out_ref[...] = y


def peer_ffn(x, x_spec, n, mods, mod_spec, norm_w, wq, keys, u_pack, v_pack, final_norm_w=None):
    grid = (n // TOKEN_TILE,)
    sel_spec = _tok_spec(PEER_SEL)
    h, idx_words, gate = pl.pallas_call(
        _route_body, grid=grid,
        in_specs=[x_spec, mod_spec, mod_spec, _row_spec(D_MODEL),
                  pl.BlockSpec((D_MODEL, PEER_QW), lambda i: (0, 0)),
                  pl.BlockSpec((PEER_HEADS * 2, PEER_NKEYS, PEER_HALF), lambda i: (0, 0, 0))],
        out_specs=[_tok_spec(D_MODEL), sel_spec, sel_spec],
        out_shape=[jax.ShapeDtypeStruct((n, D_MODEL), F32),
                   jax.ShapeDtypeStruct((n, PEER_SEL), jnp.int32),
                   jax.ShapeDtypeStruct((n, PEER_SEL), F32)],
        scratch_shapes=[pltpu.VMEM((TOKEN_TILE, PEER_QW), F32),
                        pltpu.VMEM((PEER_SEL, TOKEN_TILE), jnp.int32),
                        pltpu.VMEM((PEER_SEL, TOKEN_TILE), F32)],
        compiler_params=pltpu.CompilerParams(vmem_limit_bytes=MIX_VMEM_BYTES),
        name="peer_route",
    )(x, mods[3], mods[4], norm_w.reshape(1, D_MODEL), wq.astype(BF),
      keys.reshape(PEER_HEADS * 2, PEER_NKEYS, PEER_HALF).astype(BF))

    params = pltpu.CompilerParams(vmem_limit_bytes=PEER_VMEM_BYTES)
    idx_spec = pl.BlockSpec((TOKEN_TILE, PEER_SEL), lambda i: (i, 0), memory_space=pltpu.SMEM)
    act = pl.pallas_call(
        _peer_u_body, grid=grid,
        in_specs=[idx_spec, _tok_spec(D_MODEL), sel_spec, _RESIDENT],
        out_specs=sel_spec,
        out_shape=jax.ShapeDtypeStruct((n, PEER_SEL), F32),
        scratch_shapes=[pltpu.VMEM((TOKEN_TILE, ROW_BLOCKS * PEER_SEL), F32)],
        compiler_params=params, name="peer_u",
    )(idx_words, h, gate, u_pack)
    fw = jnp.ones((D_MODEL,), F32) if final_norm_w is None else final_norm_w
    return pl.pallas_call(
        functools.partial(_peer_v_body, final_norm=final_norm_w is not None), grid=grid,
        in_specs=[idx_spec, sel_spec, x_spec, mod_spec, _RESIDENT, _row_spec(D_MODEL)],
        out_specs=_tok_spec(D_MODEL),
        out_shape=jax.ShapeDtypeStruct((n, D_MODEL), F32),
        scratch_shapes=[pltpu.VMEM((TOKEN_TILE, ROW_BLOCKS * PEER_SEL), F32),
                        pltpu.VMEM((TOKEN_TILE * ROW_BLOCKS, LANES), F32)],
        compiler_params=params, name="peer_v",
    )(idx_words, act, x, mods[5], v_pack, fw.reshape(1, D_MODEL))


def even_layer(stream, mods, n_seq, tiles_per_seq, ctx_tiles, norm_w, w_in, w_out, sc_conv_w, ssd_conv_w,
               ssd_conv_b, dt_bias, a_log, d_skip, ssd_norm_w):
    ya, z, xs, bm, cm, dt_raw = even_in(stream, mods[0], mods[1], norm_w, w_in, sc_conv_w, ssd_conv_w, ssd_conv_b,
                                        tiles_per_seq, ctx_tiles)
    y0, y1 = ssd_scan(xs, bm, cm, dt_raw, dt_bias, -jnp.exp(a_log), n_seq, ctx_tiles * TOKEN_TILE // SSD_CHUNK)
    return even_out(stream, mods[2], ya, z, xs, y0, y1, d_skip, ssd_norm_w, w_out, tiles_per_seq, ctx_tiles)


def odd_layer(stream, mods, n_seq, tiles_per_seq, ctx_tiles, norm_w, w_in, w_out, lower_bound, hg_norm_w):
    q, f_raw, v, g = odd_in(stream, mods[0], mods[1], norm_w, w_in, tiles_per_seq, ctx_tiles)
    o0, o1 = gla_scan(q, f_raw, v, lower_bound.reshape(2, 1, HG_WIDTH), n_seq, ctx_tiles * TOKEN_TILE // GLA_CHUNK)
    return odd_out(stream, mods[2], o0, o1, g, hg_norm_w, w_out, tiles_per_seq, ctx_tiles)


def kernel(x, c, ctx, c_ctx, ada_w, ada_b, norm_mix_w, norm_ffn_w, norm_f_w, ev_w_in, ev_w_out,
           sc_conv_w, ssd_conv_w, ssd_conv_b, ssd_dt_bias, ssd_a_log, ssd_d, ssd_norm_w,
           od_w_in, od_w_out, hg_lb_logits, hg_norm_w, peer_wq, peer_keys, peer_u, peer_v):
    Bsz, seq, D = x.shape
    ctx_len = ctx.shape[1]
    assert D == D_MODEL and ctx_len == TOKEN_TILE and seq % TOKEN_TILE == 0 and TOKEN_TILE % GRID_W == 0
    lb_sm = jax.nn.softmax(hg_lb_logits.astype(F32), axis=0)
    lower_bounds = jnp.cumsum(lb_sm, axis=0) - lb_sm[0]
    cond = jnp.concatenate([c, c_ctx[None, :], jnp.zeros((7 - Bsz % 8, D), F32)], axis=0)

    ctx_tiles = ctx_len // TOKEN_TILE
    tiles_per_seq = ctx_tiles + seq // TOKEN_TILE
    stream = jnp.concatenate([ctx, x], axis=1).reshape(Bsz * (ctx_len + seq), D)
    for layer in range(DEPTH):
        last = layer == DEPTH - 1
        j = layer // 2
        m = ada_modulation(cond, ada_w[layer], ada_b[layer]).reshape(-1, 6, D)
        mods = [jnp.stack([jnp.broadcast_to(m[Bsz, k], (Bsz, D)), m[:Bsz, k]], axis=1)[:, :, None, :] for k in range(6)]
        if layer % 2 == 0:
            stream = even_layer(stream, mods, Bsz, tiles_per_seq, ctx_tiles, norm_mix_w[layer], ev_w_in[j], ev_w_out[j],
                                sc_conv_w[j], ssd_conv_w[j], ssd_conv_b[j], ssd_dt_bias[j], ssd_a_log[j], ssd_d[j],
                                ssd_norm_w[j])
        else:
            stream = odd_layer(stream, mods, Bsz, tiles_per_seq, ctx_tiles, norm_mix_w[layer], od_w_in[j], od_w_out[j],
                               lower_bounds[layer], hg_norm_w[j])
        u_pack, v_pack = pack_table(peer_u, layer), pack_table(peer_v, layer)
        if not last:
            stream = peer_ffn(stream, _tok_spec(D), stream.shape[0], mods, _mod_spec(tiles_per_seq, ctx_tiles),
                              norm_ffn_w[layer], peer_wq[layer], peer_keys[layer], u_pack, v_pack)
        else:
            lat_tiles = seq // TOKEN_TILE
            lat_spec = pl.BlockSpec((TOKEN_TILE, D),
                                    lambda i: ((i // lat_tiles) * tiles_per_seq + ctx_tiles + i % lat_tiles, 0))
            out = peer_ffn(stream, lat_spec, Bsz * seq, mods, _mod_spec(lat_tiles, 0), norm_ffn_w[layer],
                           peer_wq[layer], peer_keys[layer], u_pack, v_pack, final_norm_w=norm_f_w)
    return out.reshape(Bsz, seq, D)
```

```python
import functools

import jax
import jax.numpy as jnp
from jax import lax
from jax.experimental import pallas as pl
from jax.experimental.pallas import tpu as pltpu

D_MODEL = 1024
DEPTH = 2
GRID_W = 64
CTX_LEN = 256
NORM_EPS = 1e-6
CONV_W = 3
SC_WIDTH = 1024
SSD_HEADS = 16
SSD_HEAD_DIM = 64
SSD_WIDTH = SSD_HEADS * SSD_HEAD_DIM
SSD_GROUPS = 4
SSD_STATE = 128
SSD_BC = SSD_GROUPS * SSD_STATE
SSD_CONV_DIM = SSD_WIDTH + 2 * SSD_BC
HG_HEADS = 8
HG_KDIM = 128
HG_VDIM = 128
HG_WIDTH = HG_HEADS * HG_KDIM
PEER_HEADS = 8
PEER_NKEYS = 128
PEER_QDIM = 256
PEER_HALF = PEER_QDIM // 2
PEER_TOPK = 16
PEER_SEL = PEER_HEADS * PEER_TOPK
PEER_QW = PEER_HEADS * PEER_QDIM

LANES = 128
TOKEN_TILE = 256
SSD_CHUNK = 128
GLA_CHUNK = 64
ROW_WORDS = D_MODEL // (2 * LANES)
ROW_BLOCKS = 2 * ROW_WORDS
PEER_U_GROUP = 128
PEER_V_GROUP = 128
ROUTE_HEADS_PER_STEP = 8
MIB = 1024 * 1024
PEER_VMEM_BYTES = 48 * MIB
MIX_VMEM_BYTES = 48 * MIB
BF = jnp.bfloat16
F32 = jnp.float32


def _mod_spec(tiles_per_seq, ctx_tiles):
    return pl.BlockSpec((1, 1, 1, D_MODEL),
                        lambda i: (i // tiles_per_seq, jnp.where(i % tiles_per_seq < ctx_tiles, 0, 1), 0, 0))


def _tok_spec(width):
    return pl.BlockSpec((TOKEN_TILE, width), lambda i: (i, 0))


def _row_spec(width):
    return pl.BlockSpec((1, width), lambda i: (0, 0))


_RESIDENT = pl.BlockSpec(memory_space=pltpu.VMEM)


def _norm_mod(x, nw, shift, scale):
    y = x * lax.rsqrt(jnp.mean(x * x, axis=-1, keepdims=True) + NORM_EPS) * nw
    return y * (1.0 + scale) + shift


def _silu(x):
    return x * jax.nn.sigmoid(x)


def _select_dot(a, b, select_lhs):
    sel, x = (a, b) if select_lhs else (b, a)
    sel = sel.astype(F32).astype(BF)
    parts = x if isinstance(x, tuple) else _split3(x)
    if select_lhs:
        return sum(jnp.dot(sel, p, preferred_element_type=F32) for p in parts)
    return sum(jnp.dot(p, sel, preferred_element_type=F32) for p in parts)


def _split3(x):
    hi = x.astype(BF)
    rest = x - hi.astype(F32)
    mid = rest.astype(BF)
    return hi, mid, (rest - mid.astype(F32)).astype(BF)


def _ada_body(c_ref, w_ref, b_ref, o_ref):
    cond = _silu(c_ref[...]).astype(BF)
    o_ref[...] = jnp.dot(cond, w_ref[...].astype(BF), preferred_element_type=F32) + b_ref[...]


def ada_modulation(cond, w, b):
    rows = cond.shape[0]
    return pl.pallas_call(
        _ada_body, grid=(6,),
        in_specs=[pl.BlockSpec((rows, D_MODEL), lambda j: (0, 0)),
                  pl.BlockSpec((D_MODEL, D_MODEL), lambda j: (0, j)),
                  pl.BlockSpec((1, D_MODEL), lambda j: (0, j))],
        out_specs=pl.BlockSpec((rows, D_MODEL), lambda j: (0, j)),
        out_shape=jax.ShapeDtypeStruct((rows, 6 * D_MODEL), F32),
        name="ada_modulation",
    )(cond, w, b.reshape(1, 6 * D_MODEL))


def _even_in_body(x_ref, shift_ref, scale_ref, nw_ref, w_ref, cw_sc_ref, cw_ssd_ref, cb_ssd_ref,
                  ya_ref, z_ref, xs_ref, bm_ref, cm_ref, dt_ref, *, tiles_per_seq, ctx_tiles):
    T = TOKEN_TILE
    h = _norm_mod(x_ref[...], nw_ref[...], shift_ref[0, 0], scale_ref[0, 0]).astype(BF)
    is_ctx = (pl.program_id(0) % tiles_per_seq) < ctx_tiles
    row_len = jnp.where(is_ctx, T, GRID_W)
    pos = lax.broadcasted_iota(jnp.int32, (T, 1), 0) & (row_len - 1)
    first, last = pos == 0, pos == row_len - 1

    def conv(u, cw_ref):
        prev = jnp.where(first, 0.0, pltpu.roll(u, 1, 0))
        nxt = jnp.where(last, 0.0, pltpu.roll(u, T - 1, 0))
        return prev * cw_ref[0:1, :] + u * cw_ref[1:2, :] + nxt * cw_ref[2:3, :]

    def proj(lo, width):
        return jnp.dot(h, w_ref[:, lo:lo + width], preferred_element_type=F32)

    sc_x, sc_b, sc_c = proj(0, SC_WIDTH), proj(SC_WIDTH, SC_WIDTH), proj(2 * SC_WIDTH, SC_WIDTH)
    ya_ref[...] = sc_b * conv(sc_c * sc_x, cw_sc_ref)
    z_ref[...] = proj(3 * SC_WIDTH, SSD_WIDTH)
    xbc = _silu(conv(proj(3 * SC_WIDTH + SSD_WIDTH, SSD_CONV_DIM), cw_ssd_ref) + cb_ssd_ref[...])
    xs_ref[...] = xbc[:, :SSD_WIDTH]
    bm_ref[...] = xbc[:, SSD_WIDTH:SSD_WIDTH + SSD_BC]
    cm_ref[...] = xbc[:, SSD_WIDTH + SSD_BC:]
    dt_ref[...] = proj(3 * SC_WIDTH + SSD_WIDTH + SSD_CONV_DIM, 2 * LANES)


def even_in(x, mod_shift, mod_scale, norm_w, w_in, sc_conv_w, ssd_conv_w, ssd_conv_b, tiles_per_seq, ctx_tiles):
    n = x.shape[0]
    main = 3 * SC_WIDTH + SSD_WIDTH + SSD_CONV_DIM
    w_dt = jnp.zeros((D_MODEL, 2, LANES), F32).at[:, :, :SSD_HEADS].set(w_in[:, main:].reshape(D_MODEL, 2, SSD_HEADS))
    w = jnp.concatenate([w_in[:, :main], w_dt.reshape(D_MODEL, 2 * LANES)], axis=1).astype(BF)
    mod = _mod_spec(tiles_per_seq, ctx_tiles)
    widths = (SC_WIDTH, SSD_WIDTH, SSD_WIDTH, SSD_BC, SSD_BC, 2 * LANES)
    return pl.pallas_call(
        functools.partial(_even_in_body, tiles_per_seq=tiles_per_seq, ctx_tiles=ctx_tiles),
        grid=(n // TOKEN_TILE,),
        in_specs=[_tok_spec(D_MODEL), mod, mod, _row_spec(D_MODEL), _RESIDENT,
                  pl.BlockSpec((CONV_W, SC_WIDTH), lambda i: (0, 0)),
                  pl.BlockSpec((CONV_W, SSD_CONV_DIM), lambda i: (0, 0)), _row_spec(SSD_CONV_DIM)],
        out_specs=[_tok_spec(wd) for wd in widths],
        out_shape=[jax.ShapeDtypeStruct((n, wd), F32) for wd in widths],
        compiler_params=pltpu.CompilerParams(vmem_limit_bytes=MIX_VMEM_BYTES),
        name="even_in",
    )(x, mod_shift, mod_scale, norm_w.reshape(1, D_MODEL), w, sc_conv_w, ssd_conv_w,
      ssd_conv_b.reshape(1, SSD_CONV_DIM))


def _scan_row(n_chunks, ctx_chunks):
    def row(b, d, c):
        if d == 0:
            return b * n_chunks + c
        return b * n_chunks + jnp.where(c < ctx_chunks, ctx_chunks - 1 - c, n_chunks - 1 + ctx_chunks - c)
    return row


def _causal_mask(L, d):
    r = lax.broadcasted_iota(jnp.int32, (L, L), 0)
    c = lax.broadcasted_iota(jnp.int32, (L, L), 1)
    return c <= r if d == 0 else c >= r


SSD_GW = (SSD_HEADS // SSD_GROUPS) * SSD_HEAD_DIM


def _alternate(*phased):
    live = list(phased)
    while live:
        live = [g for g in live if next(g, StopIteration) is not StopIteration]


def _ssd_chunk(d, xs_ref, bm_ref, cm_ref, dt_ref, bias, a, y_ref, state):
    L = SSD_CHUNK
    mask = _causal_mask(L, d)
    v = dt_ref[...] + bias
    dt = jnp.maximum(v, 0.0) + jnp.log1p(jnp.exp(-jnp.abs(v)))
    la = _select_dot(mask, dt * a, True)
    la_t = la.T
    dt_parts, la_parts = _split3(dt), _split3(la)
    yield
    hpg = SSD_HEADS // SSD_GROUPS
    for g in range(SSD_GROUPS):
        lanes = slice(g * SSD_GW, (g + 1) * SSD_GW)
        head_of_lane = g * hpg + lax.broadcasted_iota(jnp.int32, (LANES, SSD_GW), 1) // SSD_HEAD_DIM
        expand = lax.broadcasted_iota(jnp.int32, (LANES, SSD_GW), 0) == head_of_lane
        dt_x = _select_dot(dt_parts, expand, False)
        la_x = _select_dot(la_parts, expand, False)
        end_x = la_x[L - 1:L] if d == 0 else la_x[0:1]
        xdt = xs_ref[:, lanes] * dt_x
        w_state = (xdt * jnp.exp(end_x - la_x)).astype(BF)
        xdt_b = xdt.astype(BF)
        bm = bm_ref[:, g * SSD_STATE:(g + 1) * SSD_STATE].astype(BF)
        cm = cm_ref[:, g * SSD_STATE:(g + 1) * SSD_STATE].astype(BF)
        cb = lax.dot_general(cm, bm, (((1,), (1,)), ((), ())), preferred_element_type=F32)
        s_old = state[g]
        y_in = jnp.dot(cm, s_old.astype(BF), preferred_element_type=F32) * jnp.exp(la_x)
        ys = []
        for j in range(hpg):
            h = g * hpg + j
            seg = la[:, h:h + 1] - la_t[h:h + 1, :]
            m = (cb * jnp.exp(jnp.where(mask, seg, -jnp.inf))).astype(BF)
            ys.append(jnp.dot(m, xdt_b[:, j * SSD_HEAD_DIM:(j + 1) * SSD_HEAD_DIM], preferred_element_type=F32))
        y_ref[:, lanes] = y_in + jnp.concatenate(ys, axis=1)
        state[g] = jnp.exp(end_x) * s_old + jnp.dot(bm.T, w_state, preferred_element_type=F32)
        yield


def _ssd_body(xs0_ref, bm0_ref, cm0_ref, dt0_ref, xs1_ref, bm1_ref, cm1_ref, dt1_ref, bias_ref, a_ref,
              y0_ref, y1_ref, state):
    @pl.when(pl.program_id(1) == 0)
    def _():
        state[...] = jnp.zeros_like(state)

    _alternate(_ssd_chunk(0, xs0_ref, bm0_ref, cm0_ref, dt0_ref, bias_ref[0], a_ref[0], y0_ref, state.at[0]),
               _ssd_chunk(1, xs1_ref, bm1_ref, cm1_ref, dt1_ref, bias_ref[1], a_ref[1], y1_ref, state.at[1]))


def ssd_scan(xs, bm, cm, dt_raw, dt_bias, a, n_seq, ctx_chunks):
    n = xs.shape[0]
    n_chunks = n // n_seq // SSD_CHUNK
    row = _scan_row(n_chunks, ctx_chunks)
    pad = lambda p: jnp.zeros((2, 1, LANES), F32).at[:, 0, :SSD_HEADS].set(p)
    tok = lambda d, width: pl.BlockSpec((SSD_CHUNK, width), lambda b, c: (row(b, d, c), 0))
    ins = lambda d: [tok(d, SSD_WIDTH), tok(d, SSD_BC), tok(d, SSD_BC),
                     pl.BlockSpec((SSD_CHUNK, LANES), lambda b, c: (row(b, d, c), d))]
    per_dir = pl.BlockSpec((2, 1, LANES), lambda b, c: (0, 0, 0))
    sds = jax.ShapeDtypeStruct((n, SSD_WIDTH), F32)
    return pl.pallas_call(
        _ssd_body, grid=(n_seq, n_chunks),
        in_specs=ins(0) + ins(1) + [per_dir, per_dir],
        out_specs=[tok(0, SSD_WIDTH), tok(1, SSD_WIDTH)],
        out_shape=[sds, sds],
        scratch_shapes=[pltpu.VMEM((2, SSD_GROUPS, SSD_STATE, SSD_GW), F32)],
        compiler_params=pltpu.CompilerParams(dimension_semantics=("arbitrary", "arbitrary")),
        name="ssd_scan",
    )(xs, bm, cm, dt_raw, xs, bm, cm, dt_raw, pad(dt_bias), pad(a))


def _gla_chunk(d, q_ref, f_ref, v_ref, lb, o_ref, state):
    L = GLA_CHUNK
    mask = _causal_mask(L, d)
    f = lb + (1.0 - lb) * jax.nn.sigmoid(f_ref[0])
    gc = _select_dot(mask, jnp.log(f), True)
    g_end = gc[L - 1:L] if d == 0 else gc[0:1]
    k = 1.0 - f
    qg = (q_ref[...] * jnp.exp(gc)).astype(BF)
    kg = (k * jnp.exp(-gc)).astype(BF)
    k_end = (k * jnp.exp(g_end - gc)).astype(BF)
    dec = jnp.exp(g_end)
    vb = v_ref[...].astype(BF)
    yield
    for h in range(HG_HEADS):
        lanes = slice(h * HG_KDIM, (h + 1) * HG_KDIM)
        att = lax.dot_general(qg[:, lanes], kg[:, lanes], (((1,), (1,)), ((), ())), preferred_element_type=F32)
        att = jnp.where(mask, att, 0.0).astype(BF)
        s_old = state[h]
        o = jnp.dot(att, vb[:, lanes], preferred_element_type=F32)
        o_ref[:, lanes] = o + lax.dot_general(qg[:, lanes], s_old.astype(BF), (((1,), (1,)), ((), ())),
                                              preferred_element_type=F32)
        state[h] = s_old * dec[:, lanes] + jnp.dot(vb[:, lanes].T, k_end[:, lanes], preferred_element_type=F32)
        yield


def _gla_body(q0_ref, f0_ref, v0_ref, q1_ref, f1_ref, v1_ref, lb_ref, o0_ref, o1_ref, state):
    @pl.when(pl.program_id(1) == 0)
    def _():
        state[...] = jnp.zeros_like(state)

    _alternate(_gla_chunk(0, q0_ref, f0_ref, v0_ref, lb_ref[0], o0_ref, state.at[0]),
               _gla_chunk(1, q1_ref, f1_ref, v1_ref, lb_ref[1], o1_ref, state.at[1]))


def gla_scan(q, f_raw, v, lb, n_seq, ctx_chunks):
    n = q.shape[0]
    n_chunks = n // n_seq // GLA_CHUNK
    row = _scan_row(n_chunks, ctx_chunks)
    tok = lambda d: pl.BlockSpec((GLA_CHUNK, HG_WIDTH), lambda b, c: (row(b, d, c), 0))
    ins = lambda d: [tok(d), pl.BlockSpec((1, GLA_CHUNK, HG_WIDTH), lambda b, c: (d, row(b, d, c), 0)), tok(d)]
    sds = jax.ShapeDtypeStruct((n, HG_WIDTH), F32)
    return pl.pallas_call(
        _gla_body, grid=(n_seq, n_chunks),
        in_specs=ins(0) + ins(1) + [pl.BlockSpec((2, 1, HG_WIDTH), lambda b, c: (0, 0, 0))],
        out_specs=[tok(0), tok(1)],
        out_shape=[sds, sds],
        scratch_shapes=[pltpu.VMEM((2, HG_HEADS, HG_VDIM, HG_KDIM), F32)],
        compiler_params=pltpu.CompilerParams(dimension_semantics=("arbitrary", "arbitrary")),
        name="gla_scan",
    )(q, f_raw, v, q, f_raw, v, lb)


def _group_rmsnorm(y, groups):
    width = y.shape[1] // groups
    parts = []
    for g in range(groups):
        seg = y[:, g * width:(g + 1) * width]
        parts.append(seg * lax.rsqrt(jnp.mean(seg * seg, axis=-1, keepdims=True) + NORM_EPS))
    return jnp.concatenate(parts, axis=1)


def _even_out_body(x_ref, gate_ref, ya_ref, z_ref, xs_ref, y0_ref, y1_ref, dskip_ref, nw_ref, w_ref, o_ref):
    yb = (y0_ref[...] + y1_ref[...] + dskip_ref[...] * xs_ref[...]) * _silu(z_ref[...])
    yb = _group_rmsnorm(yb, SSD_GROUPS) * nw_ref[...]
    mix = jnp.concatenate([ya_ref[...], yb], axis=1).astype(BF)
    o_ref[...] = x_ref[...] + gate_ref[0, 0] * jnp.dot(mix, w_ref[...], preferred_element_type=F32)


def even_out(x, mod_gate, ya, z, xs, y0, y1, d_skip, norm_w, w_out, tiles_per_seq, ctx_tiles):
    n = x.shape[0]
    return pl.pallas_call(
        _even_out_body, grid=(n // TOKEN_TILE,),
        in_specs=[_tok_spec(D_MODEL), _mod_spec(tiles_per_seq, ctx_tiles), _tok_spec(SC_WIDTH), _tok_spec(SSD_WIDTH),
                  _tok_spec(SSD_WIDTH), _tok_spec(SSD_WIDTH), _tok_spec(SSD_WIDTH), _row_spec(SSD_WIDTH),
                  _row_spec(SSD_WIDTH), _RESIDENT],
        out_specs=_tok_spec(D_MODEL),
        out_shape=jax.ShapeDtypeStruct((n, D_MODEL), F32),
        compiler_params=pltpu.CompilerParams(vmem_limit_bytes=MIX_VMEM_BYTES),
        name="even_out",
    )(x, mod_gate, ya, z, xs, y0, y1, jnp.repeat(d_skip, SSD_HEAD_DIM).reshape(1, SSD_WIDTH),
      norm_w.reshape(1, SSD_WIDTH), w_out.astype(BF))


def _odd_in_body(x_ref, shift_ref, scale_ref, nw_ref, w_ref, q_ref, f_ref, v_ref, g_ref):
    h = _norm_mod(x_ref[...], nw_ref[...], shift_ref[0, 0], scale_ref[0, 0]).astype(BF)
    proj = lambda j: jnp.dot(h, w_ref[:, j * HG_WIDTH:(j + 1) * HG_WIDTH], preferred_element_type=F32)
    q_ref[...] = _silu(proj(0)) * (HG_KDIM ** -0.5)
    f_ref[0] = proj(1)
    f_ref[1] = proj(2)
    v_ref[...] = proj(3)
    g_ref[...] = proj(4)


def odd_in(x, mod_shift, mod_scale, norm_w, w_in, tiles_per_seq, ctx_tiles):
    n = x.shape[0]
    mod = _mod_spec(tiles_per_seq, ctx_tiles)
    tok = _tok_spec(HG_WIDTH)
    sds = jax.ShapeDtypeStruct((n, HG_WIDTH), F32)
    return pl.pallas_call(
        _odd_in_body, grid=(n // TOKEN_TILE,),
        in_specs=[_tok_spec(D_MODEL), mod, mod, _row_spec(D_MODEL), _RESIDENT],
        out_specs=[tok, pl.BlockSpec((2, TOKEN_TILE, HG_WIDTH), lambda i: (0, i, 0)), tok, tok],
        out_shape=[sds, jax.ShapeDtypeStruct((2, n, HG_WIDTH), F32), sds, sds],
        compiler_params=pltpu.CompilerParams(vmem_limit_bytes=MIX_VMEM_BYTES),
        name="odd_in",
    )(x, mod_shift, mod_scale, norm_w.reshape(1, D_MODEL), w_in.astype(BF))


def _odd_out_body(x_ref, gate_ref, o0_ref, o1_ref, g_ref, nw_ref, w_ref, out_ref):
    o = _group_rmsnorm(o0_ref[...] + o1_ref[...], HG_HEADS) * nw_ref[...] * _silu(g_ref[...])
    out_ref[...] = x_ref[...] + gate_ref[0, 0] * jnp.dot(o.astype(BF), w_ref[...], preferred_element_type=F32)


def odd_out(x, mod_gate, o0, o1, g, norm_w, w_out, tiles_per_seq, ctx_tiles):
    n = x.shape[0]
    return pl.pallas_call(
        _odd_out_body, grid=(n // TOKEN_TILE,),
        in_specs=[_tok_spec(D_MODEL), _mod_spec(tiles_per_seq, ctx_tiles), _tok_spec(HG_WIDTH), _tok_spec(HG_WIDTH),
                  _tok_spec(HG_WIDTH), _row_spec(HG_WIDTH), _RESIDENT],
        out_specs=_tok_spec(D_MODEL),
        out_shape=jax.ShapeDtypeStruct((n, D_MODEL), F32),
        compiler_params=pltpu.CompilerParams(vmem_limit_bytes=MIX_VMEM_BYTES),
        name="odd_out",
    )(x, mod_gate, o0, o1, g, norm_w.reshape(1, HG_WIDTH), w_out.astype(BF))


def _top16(vals, ids, n_ids):
    top_s, top_i = [], []
    for r in range(PEER_TOPK):
        m = jnp.max(vals, axis=0, keepdims=True)
        pos = jnp.min(jnp.where(vals == m, ids, float(n_ids)), axis=0, keepdims=True)
        if r + 1 < PEER_TOPK:
            vals = jnp.where(ids == pos, -jnp.inf, vals)
        top_s.append(m)
        top_i.append(pos)
    return jnp.concatenate(top_s, axis=0), jnp.concatenate(top_i, axis=0).astype(jnp.int32)


def _pick_row(table, row):
    r = lax.broadcasted_iota(jnp.int32, table.shape, 0)
    return jnp.sum(jnp.where(r == row, table, 0), axis=0, keepdims=True)


def _route_body(x_ref, shift_ref, scale_ref, nw_ref, wq_ref, keys_ref, h_ref, idx_ref, gate_ref,
                q_scr, idx_scr, gate_scr):
    h = _norm_mod(x_ref[...], nw_ref[...], shift_ref[0, 0], scale_ref[0, 0])
    h_ref[...] = h
    q_scr[...] = jnp.dot(h.astype(BF), wq_ref[...], preferred_element_type=F32)

    def route(hd, tok0):
        tops = []
        for a in range(2):
            col = pl.multiple_of(hd * PEER_QDIM + a * PEER_HALF, PEER_HALF)
            qs = q_scr[pl.ds(tok0, LANES), pl.ds(col, PEER_HALF)].astype(BF)
            sc = lax.dot_general(keys_ref[hd * 2 + a], qs, (((1,), (1,)), ((), ())),
                                 preferred_element_type=F32)
            tops.append(_top16(sc, lax.broadcasted_iota(jnp.int32, sc.shape, 0).astype(F32), PEER_NKEYS))
        (s0, i0), (s1, i1) = tops
        sub = lax.broadcasted_iota(jnp.int32, (8, LANES), 0).astype(F32)
        chunks = [(s0[0:8] + s1[0:1], sub * PEER_TOPK), (s0[8:16] + s1[0:1], (sub + 8) * PEER_TOPK)]
        chunks += [(s0[0:8] + s1[b:b + 1], sub * PEER_TOPK + b) for b in range(1, 8)]
        chunks.append((s0[0:1] + s1[8:16], sub + 8))
        best, pos = _top16(jnp.concatenate([c[0] for c in chunks], axis=0),
                           jnp.concatenate([c[1] for c in chunks], axis=0), PEER_TOPK * PEER_TOPK)
        ids = jnp.concatenate(
            [_pick_row(i0, pos[r:r + 1] >> 4) * PEER_NKEYS + _pick_row(i1, pos[r:r + 1] & (PEER_TOPK - 1))
             for r in range(PEER_TOPK)], axis=0)
        e = jnp.exp(best - best[0:1])
        r0 = pl.multiple_of(hd * PEER_TOPK, PEER_TOPK)
        idx_scr[pl.ds(r0, PEER_TOPK), pl.ds(tok0, LANES)] = ids * ROW_WORDS
        gate_scr[pl.ds(r0, PEER_TOPK), pl.ds(tok0, LANES)] = e / jnp.sum(e, axis=0, keepdims=True)

    def head_group(hg, carry):
        for hh in range(ROUTE_HEADS_PER_STEP):
            for lg in range(TOKEN_TILE // LANES):
                route(hg * ROUTE_HEADS_PER_STEP + hh, lg * LANES)
        return carry

    lax.fori_loop(0, PEER_HEADS // ROUTE_HEADS_PER_STEP, head_group, 0)
    idx_ref[...] = idx_scr[...].T
    gate_ref[...] = gate_scr[...].T


PACK_ROWS = 512


def _pack_body(tab_ref, out_ref):
    bits = pltpu.bitcast(tab_ref[0].astype(BF).astype(F32), jnp.int32)
    for s in range(ROW_WORDS):
        lo = bits[:, 2 * s * LANES:(2 * s + 1) * LANES]
        hi = bits[:, (2 * s + 1) * LANES:(2 * s + 2) * LANES]
        out_ref[pl.ds(s, PACK_ROWS, stride=ROW_WORDS), :] = lax.shift_right_logical(lo, 16) | hi


def pack_table(tabs, layer):
    E = tabs.shape[1]
    return pl.pallas_call(
        _pack_body, grid=(E // PACK_ROWS,),
        in_specs=[pl.BlockSpec((1, PACK_ROWS, D_MODEL), lambda i: (layer, i, 0))],
        out_specs=pl.BlockSpec((PACK_ROWS * ROW_WORDS, LANES), lambda i: (i, 0)),
        out_shape=jax.ShapeDtypeStruct((E * ROW_WORDS, LANES), jnp.int32),
        name="pack_table",
    )(tabs)


def _gathered_blocks(idx_ref, tab_ref, t):
    row = idx_ref.at[t]
    slabs = [tab_ref[pl.ds(pl.multiple_of(row[k], ROW_WORDS), ROW_WORDS), :] for k in range(PEER_SEL)]
    return pltpu.bitcast(jnp.concatenate(slabs, axis=0), BF)


def _own_block_mask():
    shape = (ROW_BLOCKS, ROW_BLOCKS * PEER_SEL)
    return lax.broadcasted_iota(jnp.int32, shape, 1) % ROW_BLOCKS == lax.broadcasted_iota(jnp.int32, shape, 0)


def _peer_u_body(idx_ref, h_ref, gate_ref, tab_ref, act_ref, part_scr):
    own_block = _own_block_mask()

    def step(s, carry):
        for i in range(PEER_U_GROUP):
            t = s * PEER_U_GROUP + i
            hrow = h_ref[pl.ds(t, 1), :]
            lhs = jnp.concatenate([hrow[:, m * LANES:(m + 1) * LANES] for m in range(ROW_BLOCKS)], axis=0).astype(BF)
            prod = lax.dot_general(lhs, _gathered_blocks(idx_ref, tab_ref, t), (((1,), (1,)), ((), ())),
                                   preferred_element_type=F32)
            part_scr[pl.ds(t, 1), :] = jnp.sum(jnp.where(own_block, prod, 0.0), axis=0, keepdims=True)
        return carry

    lax.fori_loop(0, TOKEN_TILE // PEER_U_GROUP, step, 0)
    width = ROW_BLOCKS * PEER_SEL
    fold = (lax.broadcasted_iota(jnp.int32, (width, PEER_SEL), 0) // ROW_BLOCKS
            == lax.broadcasted_iota(jnp.int32, (width, PEER_SEL), 1))
    d = _select_dot(part_scr[...], fold, False)
    act_ref[...] = 0.5 * d * (1.0 + lax.erf(d * (2.0 ** -0.5))) * gate_ref[...]


def _peer_v_body(idx_ref, act_ref, x_ref, gate_ref, tab_ref, fw_ref, out_ref, actx_scr, blk_scr, *, final_norm):
    width = ROW_BLOCKS * PEER_SEL
    spread = (lax.broadcasted_iota(jnp.int32, (PEER_SEL, width), 1) // ROW_BLOCKS
              == lax.broadcasted_iota(jnp.int32, (PEER_SEL, width), 0)).astype(F32).astype(BF)
    actx_scr[...] = jnp.dot(act_ref[...].astype(BF), spread, preferred_element_type=F32)
    own_block = _own_block_mask()

    def step(s, carry):
        for i in range(PEER_V_GROUP):
            t = s * PEER_V_GROUP + i
            lhs = jnp.where(own_block, jnp.broadcast_to(actx_scr[pl.ds(t, 1), :], (ROW_BLOCKS, width)), 0.0).astype(BF)
            blk_scr[pl.ds(pl.multiple_of(t * ROW_BLOCKS, ROW_BLOCKS), ROW_BLOCKS), :] = jnp.dot(
                lhs, _gathered_blocks(idx_ref, tab_ref, t), preferred_element_type=F32)
        return carry

    lax.fori_loop(0, TOKEN_TILE // PEER_V_GROUP, step, 0)
    acc = jnp.concatenate([blk_scr[pl.ds(m, TOKEN_TILE, stride=ROW_BLOCKS), :] for m in range(ROW_BLOCKS)], axis=1)
    y = x_ref[...] + gate_ref[0, 0] * acc
    if final_norm:
        y = y * lax.rsqrt(jnp.mean(y * y, axis=-1, keepdims=True) + NORM_EPS) * fw_ref[...]
    out_ref[...] = y


def peer_ffn(x, x_spec, n, mods, mod_spec, norm_w, wq, keys, u_pack, v_pack, final_norm_w=None):
    grid = (n // TOKEN_TILE,)
    sel_spec = _tok_spec(PEER_SEL)
    h, idx_words, gate = pl.pallas_call(
        _route_body, grid=grid,
        in_specs=[x_spec, mod_spec, mod_spec, _row_spec(D_MODEL),
                  pl.BlockSpec((D_MODEL, PEER_QW), lambda i: (0, 0)),
                  pl.BlockSpec((PEER_HEADS * 2, PEER_NKEYS, PEER_HALF), lambda i: (0, 0, 0))],
        out_specs=[_tok_spec(D_MODEL), sel_spec, sel_spec],
        out_shape=[jax.ShapeDtypeStruct((n, D_MODEL), F32),
                   jax.ShapeDtypeStruct((n, PEER_SEL), jnp.int32),
                   jax.ShapeDtypeStruct((n, PEER_SEL), F32)],
        scratch_shapes=[pltpu.VMEM((TOKEN_TILE, PEER_QW), F32),
                        pltpu.VMEM((PEER_SEL, TOKEN_TILE), jnp.int32),
                        pltpu.VMEM((PEER_SEL, TOKEN_TILE), F32)],
        compiler_params=pltpu.CompilerParams(vmem_limit_bytes=MIX_VMEM_BYTES),
        name="peer_route",
    )(x, mods[3], mods[4], norm_w.reshape(1, D_MODEL), wq.astype(BF),
      keys.reshape(PEER_HEADS * 2, PEER_NKEYS, PEER_HALF).astype(BF))

    params = pltpu.CompilerParams(vmem_limit_bytes=PEER_VMEM_BYTES)
    idx_spec = pl.BlockSpec((TOKEN_TILE, PEER_SEL), lambda i: (i, 0), memory_space=pltpu.SMEM)
    act = pl.pallas_call(
        _peer_u_body, grid=grid,
        in_specs=[idx_spec, _tok_spec(D_MODEL), sel_spec, _RESIDENT],
        out_specs=sel_spec,
        out_shape=jax.ShapeDtypeStruct((n, PEER_SEL), F32),
        scratch_shapes=[pltpu.VMEM((TOKEN_TILE, ROW_BLOCKS * PEER_SEL), F32)],
        compiler_params=params, name="peer_u",
    )(idx_words, h, gate, u_pack)
    fw = jnp.ones((D_MODEL,), F32) if final_norm_w is None else final_norm_w
    return pl.pallas_call(
        functools.partial(_peer_v_body, final_norm=final_norm_w is not None), grid=grid,
        in_specs=[idx_spec, sel_spec, x_spec, mod_spec, _RESIDENT, _row_spec(D_MODEL)],
        out_specs=_tok_spec(D_MODEL),
        out_shape=jax.ShapeDtypeStruct((n, D_MODEL), F32),
        scratch_shapes=[pltpu.VMEM((TOKEN_TILE, ROW_BLOCKS * PEER_SEL), F32),
                        pltpu.VMEM((TOKEN_TILE * ROW_BLOCKS, LANES), F32)],
        compiler_params=params, name="peer_v",
    )(idx_words, act, x, mods[5], v_pack, fw.reshape(1, D_MODEL))


def even_layer(stream, mods, n_seq, tiles_per_seq, ctx_tiles, norm_w, w_in, w_out, sc_conv_w, ssd_conv_w,
               ssd_conv_b, dt_bias, a_log, d_skip, ssd_norm_w):
    ya, z, xs, bm, cm, dt_raw = even_in(stream, mods[0], mods[1], norm_w, w_in, sc_conv_w, ssd_conv_w, ssd_conv_b,
                                        tiles_per_seq, ctx_tiles)
    y0, y1 = ssd_scan(xs, bm, cm, dt_raw, dt_bias, -jnp.exp(a_log), n_seq, ctx_tiles * TOKEN_TILE // SSD_CHUNK)
    return even_out(stream, mods[2], ya, z, xs, y0, y1, d_skip, ssd_norm_w, w_out, tiles_per_seq, ctx_tiles)


def odd_layer(stream, mods, n_seq, tiles_per_seq, ctx_tiles, norm_w, w_in, w_out, lower_bound, hg_norm_w):
    q, f_raw, v, g = odd_in(stream, mods[0], mods[1], norm_w, w_in, tiles_per_seq, ctx_tiles)
    o0, o1 = gla_scan(q, f_raw, v, lower_bound.reshape(2, 1, HG_WIDTH), n_seq, ctx_tiles * TOKEN_TILE // GLA_CHUNK)
    return odd_out(stream, mods[2], o0, o1, g, hg_norm_w, w_out, tiles_per_seq, ctx_tiles)


def kernel(x, c, ctx, c_ctx, ada_w, ada_b, norm_mix_w, norm_ffn_w, norm_f_w, ev_w_in, ev_w_out,
           sc_conv_w, ssd_conv_w, ssd_conv_b, ssd_dt_bias, ssd_a_log, ssd_d, ssd_norm_w,
           od_w_in, od_w_out, hg_lb_logits, hg_norm_w, peer_wq, peer_keys, peer_u, peer_v):
    Bsz, seq, D = x.shape
    ctx_len = ctx.shape[1]
    assert D == D_MODEL and ctx_len == TOKEN_TILE and seq % TOKEN_TILE == 0 and TOKEN_TILE % GRID_W == 0
    lb_sm = jax.nn.softmax(hg_lb_logits.astype(F32), axis=0)
    lower_bounds = jnp.cumsum(lb_sm, axis=0) - lb_sm[0]
    cond = jnp.concatenate([c, c_ctx[None, :], jnp.zeros((7 - Bsz % 8, D), F32)], axis=0)

    ctx_tiles = ctx_len // TOKEN_TILE
    tiles_per_seq = ctx_tiles + seq // TOKEN_TILE
    stream = jnp.concatenate([ctx, x], axis=1).reshape(Bsz * (ctx_len + seq), D)
    for layer in range(DEPTH):
        last = layer == DEPTH - 1
        j = layer // 2
        m = ada_modulation(cond, ada_w[layer], ada_b[layer]).reshape(-1, 6, D)
        mods = [jnp.stack([jnp.broadcast_to(m[Bsz, k], (Bsz, D)), m[:Bsz, k]], axis=1)[:, :, None, :] for k in range(6)]
        if layer % 2 == 0:
            stream = even_layer(stream, mods, Bsz, tiles_per_seq, ctx_tiles, norm_mix_w[layer], ev_w_in[j], ev_w_out[j],
                                sc_conv_w[j], ssd_conv_w[j], ssd_conv_b[j], ssd_dt_bias[j], ssd_a_log[j], ssd_d[j],
                                ssd_norm_w[j])
        else:
            stream = odd_layer(stream, mods, Bsz, tiles_per_seq, ctx_tiles, norm_mix_w[layer], od_w_in[j], od_w_out[j],
                               lower_bounds[layer], hg_norm_w[j])
        u_pack, v_pack = pack_table(peer_u, layer), pack_table(peer_v, layer)
        if not last:
            stream = peer_ffn(stream, _tok_spec(D), stream.shape[0], mods, _mod_spec(tiles_per_seq, ctx_tiles),
                              norm_ffn_w[layer], peer_wq[layer], peer_keys[layer], u_pack, v_pack)
        else:
            lat_tiles = seq // TOKEN_TILE
            lat_spec = pl.BlockSpec((TOKEN_TILE, D),
                                    lambda i: ((i // lat_tiles) * tiles_per_seq + ctx_tiles + i % lat_tiles, 0))
            out = peer_ffn(stream, lat_spec, Bsz * seq, mods, _mod_spec(lat_tiles, 0), norm_ffn_w[layer],
                           peer_wq[layer], peer_keys[layer], u_pack, v_pack, final_norm_w=norm_f_w)
    return out.reshape(Bsz, seq, D)
```
